```python
import functools
import jax, jax.numpy as jnp
from jax import lax
import numpy as np

D_MODEL = 1024
BATCH = 32
SEQ = 2048
DEPTH = 1
DEC_BATCH = 128
DEC_SEQ = 8
PAST_LEN = 16384
PAGE_SIZE = 128

GDN_HEADS = 4
GDN_DK = 128
GDN_DV = 128
CONV_W = 4
GDN_CHUNK = 64
MLA_HEADS = 4
Q_LORA = 256
KV_LORA = 256
QK_NOPE = 128
QK_ROPE = 64
V_DIM = 128
ROPE_THETA = 10000.0
QBLK = 128
N_EXPERTS = 64
TOP_K = 8
N_GROUPS = 8
TOPK_GROUPS = 4
D_EXPERT = 256
D_SHARED = 256
ROUTED_SCALE = 2.5
EXPERT_BLOCK = 128
PLE_DIM = 256
EPS = 1e-6

GDN_QK_W = GDN_HEADS * GDN_DK
GDN_V_W = GDN_HEADS * GDN_DV
CONV_CH = 2 * GDN_QK_W + GDN_V_W
MLA_V_W = MLA_HEADS * V_DIM
MIX_WIDTH = GDN_V_W + MLA_V_W
MLA_Q_W = MLA_HEADS * (QK_NOPE + QK_ROPE)
MLA_SCALE = (QK_NOPE + QK_ROPE) ** -0.5
IN_SIZES = (CONV_CH, GDN_V_W, GDN_HEADS, GDN_HEADS, Q_LORA, KV_LORA, QK_ROPE)
IN_SPLITS = [int(s) for s in np.cumsum(IN_SIZES)[:-1]]
IN_WIDTH = int(sum(IN_SIZES))

kernel_name = 'hybrid_gdn_mla_moe_ple_step'


def rmsnorm(x, g):
    xf = x.astype(jnp.float32)
    y = xf * lax.rsqrt(jnp.mean(xf * xf, axis=-1, keepdims=True) + EPS)
    return (y * g.astype(jnp.float32)).astype(x.dtype)


def l2norm(x):
    return x * lax.rsqrt(jnp.sum(x * x, axis=-1, keepdims=True) + EPS)


def rope(x, pos):
    half = x.shape[-1] // 2
    inv = ROPE_THETA ** (-jnp.arange(half, dtype=jnp.float32) / half)
    ang = pos.astype(jnp.float32)[:, None] * inv[None, :]
    cos = jnp.cos(ang)[None, :, None, :]
    sin = jnp.sin(ang)[None, :, None, :]
    xf = x.astype(jnp.float32)
    x1, x2 = xf[..., :half], xf[..., half:]
    return jnp.concatenate([x1 * cos - x2 * sin, x2 * cos + x1 * sin], axis=-1).astype(x.dtype)


def short_conv(x_new, buf, w):
    t = x_new.shape[1]
    xx = jnp.concatenate([buf.astype(x_new.dtype), x_new], axis=1)
    y = sum(xx[:, j:j + t] * w[j] for j in range(CONV_W))
    return jax.nn.silu(y), xx[:, t:]


def gated_delta_chunked(q, k, v, beta, g, s0):
    b, t, h, dk = q.shape
    dv = v.shape[-1]
    n = -(-t // GDN_CHUNK)
    pad = n * GDN_CHUNK - t

    def to_chunks(x):
        x = jnp.pad(x, [(0, 0), (0, pad)] + [(0, 0)] * (x.ndim - 2))
        x = jnp.moveaxis(x, 2, 1)
        return x.reshape((b, h, n, GDN_CHUNK) + x.shape[3:])

    qc, kc, vc, bc, gc = (to_chunks(a) for a in (q, k, v, beta, g))
    gc = jnp.cumsum(gc, axis=-1)
    idx = jnp.arange(GDN_CHUNK)
    incl = idx[:, None] >= idx[None, :]
    strict = idx[:, None] > idx[None, :]
    decay = jnp.exp(jnp.where(incl, gc[..., :, None] - gc[..., None, :], -jnp.inf))
    kb = kc * bc[..., None]
    lower = jnp.where(strict, jnp.einsum('bhncd,bhnsd->bhncs', kb, kc) * decay, 0.0)
    m = lower + jnp.eye(GDN_CHUNK, dtype=lower.dtype)
    rhs = jnp.concatenate([vc * bc[..., None], kb * jnp.exp(gc)[..., None]], axis=-1)
    sol = lax.linalg.triangular_solve(m, rhs, left_side=True, lower=True, unit_diagonal=True)
    u, w = sol[..., :dv], sol[..., dv:]
    qk = jnp.where(incl, jnp.einsum('bhncd,bhnsd->bhncs', qc, kc) * decay, 0.0)

    def step(s, xs):
        q_i, k_i, u_i, w_i, g_i, qk_i = xs
        v_new = u_i - jnp.einsum('bhcd,bhde->bhce', w_i, s)
        o_i = (jnp.einsum('bhcd,bhde->bhce', q_i * jnp.exp(g_i)[..., None], s)
               + jnp.einsum('bhcs,bhse->bhce', qk_i, v_new))
        g_last = g_i[..., -1]
        s = (s * jnp.exp(g_last)[..., None, None]
             + jnp.einsum('bhcd,bhce->bhde', k_i * jnp.exp(g_last[..., None] - g_i)[..., None], v_new))
        return s, o_i

    xs = tuple(jnp.moveaxis(a, 2, 0) for a in (qc, kc, u, w, gc, qk))
    s_fin, o = lax.scan(step, s0, xs)
    o = jnp.moveaxis(o, 0, 2).reshape(b, h, n * GDN_CHUNK, dv)[:, :, :t]
    return jnp.moveaxis(o, 1, 2), s_fin


def mla_prompt_attend(q_lat, q_rope, ckv, krope):
    b, s, h, r = q_lat.shape
    nb = s // QBLK
    ql = jnp.moveaxis(q_lat.reshape(b, nb, QBLK, h, r), 1, 0)
    qr = jnp.moveaxis(q_rope.reshape(b, nb, QBLK, h, -1), 1, 0)
    kpos = jnp.arange(s)

    def block(args):
        ql_b, qr_b, i = args
        sc = (jnp.einsum('bthr,bsr->bhts', ql_b, ckv)
              + jnp.einsum('bthe,bse->bhts', qr_b, krope)).astype(jnp.float32) * MLA_SCALE
        qpos = i * QBLK + jnp.arange(QBLK)
        sc = jnp.where(kpos[None, :] <= qpos[:, None], sc, -jnp.inf)
        p = jax.nn.softmax(sc, axis=-1).astype(ckv.dtype)
        return jnp.einsum('bhts,bsr->bthr', p, ckv)

    o = lax.map(block, (ql, qr, jnp.arange(nb)))
    return jnp.moveaxis(o, 0, 1).reshape(b, s, h, r)


def mla_paged_attend(q_lat, q_rope, ckv, krope, cache_ckv, cache_krope, page_table):
    b, t = q_lat.shape[:2]
    past = page_table.shape[1] * cache_ckv.shape[1]
    ckv_past = cache_ckv[page_table].reshape(b, past, -1)
    kr_past = cache_krope[page_table].reshape(b, past, -1)
    s_past = (jnp.einsum('bthr,bsr->bhts', q_lat, ckv_past)
              + jnp.einsum('bthe,bse->bhts', q_rope, kr_past)).astype(jnp.float32) * MLA_SCALE
    s_new = (jnp.einsum('bthr,bsr->bhts', q_lat, ckv)
             + jnp.einsum('bthe,bse->bhts', q_rope, krope)).astype(jnp.float32) * MLA_SCALE
    causal = jnp.tril(jnp.ones((t, t), dtype=bool))
    s_new = jnp.where(causal, s_new, -jnp.inf)
    p = jax.nn.softmax(jnp.concatenate([s_past, s_new], axis=-1), axis=-1).astype(ckv.dtype)
    return (jnp.einsum('bhts,bsr->bthr', p[..., :past], ckv_past)
            + jnp.einsum('bhts,bsr->bthr', p[..., past:], ckv))


def swiglu(x, wg, wu, wd):
    return (jax.nn.silu(x @ wg) * (x @ wu)) @ wd


def route(x2, w_router, router_bias):
    n = x2.shape[0]
    scores = jax.nn.sigmoid((x2 @ w_router).astype(jnp.float32))
    biased = scores + router_bias.astype(jnp.float32)
    grouped = biased.reshape(n, N_GROUPS, N_EXPERTS // N_GROUPS)
    group_score = lax.top_k(grouped, 2)[0].sum(-1)
    _, gidx = lax.top_k(group_score, TOPK_GROUPS)
    gmask = jax.nn.one_hot(gidx, N_GROUPS, dtype=jnp.float32).sum(1) > 0
    emask = jnp.repeat(gmask, N_EXPERTS // N_GROUPS, axis=1)
    _, idx = lax.top_k(jnp.where(emask, biased, -jnp.inf), TOP_K)
    w = jnp.take_along_axis(scores, idx, axis=1)
    w = w / (w.sum(-1, keepdims=True) + 1e-20) * ROUTED_SCALE
    return idx, w


def routed_experts(x2, idx, wts, wg, wu, wd):
    n, d = x2.shape
    a = n * TOP_K
    flat_e = idx.reshape(a)
    flat_tok = (jnp.arange(a, dtype=jnp.int32) // TOP_K).astype(jnp.int32)
    flat_w = wts.reshape(a)
    order = jnp.argsort(flat_e)
    se = flat_e[order]
    counts = jnp.bincount(flat_e, length=N_EXPERTS)
    pcounts = (counts + EXPERT_BLOCK - 1) // EXPERT_BLOCK * EXPERT_BLOCK
    start = jnp.cumsum(counts) - counts
    pend = jnp.cumsum(pcounts)
    pstart = pend - pcounts
    dest = pstart[se] + jnp.arange(a) - start[se]
    nblk = (a + N_EXPERTS * (EXPERT_BLOCK - 1) + EXPERT_BLOCK - 1) // EXPERT_BLOCK
    rows = nblk * EXPERT_BLOCK
    row_tok = jnp.full((rows,), n, jnp.int32).at[dest].set(flat_tok[order])
    row_w = jnp.zeros((rows,), jnp.float32).at[dest].set(flat_w[order])
    blk_e = jnp.minimum(jnp.searchsorted(pend, jnp.arange(nblk) * EXPERT_BLOCK, side='right'),
                        N_EXPERTS - 1)
    xpad = jnp.concatenate([x2, jnp.zeros((1, d), x2.dtype)], axis=0)

    def step(out, args):
        tok, rw, e = args
        xb = xpad[tok]
        yb = swiglu(xb, wg[e], wu[e], wd[e]) * rw[:, None].astype(x2.dtype)
        return out.at[tok].add(yb), None

    out, _ = lax.scan(step, jnp.zeros((n + 1, d), x2.dtype),
                      (row_tok.reshape(nblk, EXPERT_BLOCK), row_w.reshape(nblk, EXPERT_BLOCK), blk_e))
    return out[:n]


def moe_ffn(x, w_router, router_bias, w_exp_gate, w_exp_up, w_exp_down, w_sh_gate, w_sh_up, w_sh_down):
    b, t, d = x.shape
    x2 = x.reshape(b * t, d)
    idx, wts = route(x2, w_router, router_bias)
    y = routed_experts(x2, idx, wts, w_exp_gate, w_exp_up, w_exp_down) + swiglu(x2, w_sh_gate, w_sh_up, w_sh_down)
    return y.reshape(b, t, d)


def hybrid_layer(h, p_l, pos, gdn_state, conv_buf, attend, lw):
    b, t, _ = h.shape
    f32 = jnp.float32
    u = rmsnorm(h, lw['g_mix'])
    proj = u @ lw['w_in']
    qkv_pre, z, b_in, a_in, c_q, c_kv, k_r = jnp.split(proj, IN_SPLITS, axis=-1)

    qkv, conv_new = short_conv(qkv_pre, conv_buf, lw['gdn_conv_w'])
    q, k, v = jnp.split(qkv, [GDN_QK_W, 2 * GDN_QK_W], axis=-1)
    q = l2norm(q.reshape(b, t, GDN_HEADS, GDN_DK).astype(f32)) * GDN_DK ** -0.5
    k = l2norm(k.reshape(b, t, GDN_HEADS, GDN_DK).astype(f32))
    v = v.reshape(b, t, GDN_HEADS, GDN_DV).astype(f32)
    beta = jax.nn.sigmoid(b_in.astype(f32))
    g = -jnp.exp(lw['gdn_a_log'].astype(f32)) * jax.nn.softplus(a_in.astype(f32) + lw['gdn_dt_bias'].astype(f32))
    o, gdn_new = gated_delta_chunked(q, k, v, beta, g, gdn_state.astype(f32))
    o = rmsnorm(o, lw['gdn_norm']) * jax.nn.silu(z.reshape(b, t, GDN_HEADS, GDN_DV).astype(f32))
    o_gdn = o.reshape(b, t, GDN_V_W).astype(h.dtype)

    qh = (rmsnorm(c_q, lw['mla_g_q']) @ lw['mla_w_uq']).reshape(b, t, MLA_HEADS, QK_NOPE + QK_ROPE)
    q_nope, q_rope = qh[..., :QK_NOPE], rope(qh[..., QK_NOPE:], pos)
    ckv = rmsnorm(c_kv, lw['mla_g_kv'])
    krope = rope(k_r[:, :, None, :], pos)[:, :, 0]
    q_lat = jnp.einsum('bthn,rhn->bthr', q_nope, lw['mla_w_uk'])
    o_lat = attend(q_lat, q_rope, ckv, krope)
    o_mla = jnp.einsum('bthr,rhv->bthv', o_lat, lw['mla_w_uv']).reshape(b, t, MLA_V_W)

    h = h + jnp.concatenate([o_gdn, o_mla], axis=-1) @ lw['w_out']
    h = h + moe_ffn(rmsnorm(h, lw['g_ffn']), lw['w_router'], lw['router_bias'],
                    lw['w_exp_gate'], lw['w_exp_up'], lw['w_exp_down'],
                    lw['w_sh_gate'], lw['w_sh_up'], lw['w_sh_down'])
    gate = jax.nn.sigmoid(rmsnorm(h, lw['g_ple']) @ lw['w_ple_gate'])
    h = h + gate * (p_l.astype(h.dtype) @ lw['w_ple_proj'])
    return h, ckv, krope, gdn_new, conv_new


def setup_inputs(seed: int = 0) -> dict:
    key = jax.random.key(seed)
    ks = iter(jax.random.split(key, 48))
    f32 = jnp.float32

    def nrm(shape, scale):
        return jax.random.normal(next(ks), shape, f32) * scale

    def gain(shape):
        return 1.0 + nrm(shape, 0.02)

    n_pages = PAST_LEN // PAGE_SIZE
    n_pool = (DEC_BATCH * n_pages * 5) // 4
    x_prompt = nrm((BATCH, SEQ, D_MODEL), 1.0)
    x_sample = nrm((DEC_BATCH, DEC_SEQ, D_MODEL), 1.0)
    p_prompt = nrm((DEPTH, BATCH, SEQ, PLE_DIM), 1.0)
    p_sample = nrm((DEPTH, DEC_BATCH, DEC_SEQ, PLE_DIM), 1.0)
    cache_ckv = nrm((DEPTH, n_pool, PAGE_SIZE, KV_LORA), 1.0)
    cache_krope = nrm((DEPTH, n_pool, PAGE_SIZE, QK_ROPE), 1.0)
    state_gdn = nrm((DEPTH, DEC_BATCH, GDN_HEADS, GDN_DK, GDN_DV), 0.05)
    state_conv = nrm((DEPTH, DEC_BATCH, CONV_W - 1, CONV_CH), 1.0)
    page_table = jax.random.permutation(next(ks), n_pool)[:DEC_BATCH * n_pages].reshape(
        DEC_BATCH, n_pages).astype(jnp.int32)
    a_log = jnp.log(jax.random.uniform(next(ks), (DEPTH, GDN_HEADS), f32, 1.0, 16.0))
    dt = jnp.exp(jax.random.uniform(next(ks), (DEPTH, GDN_HEADS), f32, np.log(1e-3), np.log(1e-1)))
    dt_bias = dt + jnp.log(-jnp.expm1(-dt))
    return {
        'x_prompt': x_prompt,
        'x_sample': x_sample,
        'p_prompt': p_prompt,
        'p_sample': p_sample,
        'cache_ckv': cache_ckv,
        'cache_krope': cache_krope,
        'state_gdn': state_gdn,
        'state_conv': state_conv,
        'page_table': page_table,
        'g_mix': gain((DEPTH, D_MODEL)),
        'w_in': nrm((DEPTH, D_MODEL, IN_WIDTH), D_MODEL ** -0.5),
        'gdn_conv_w': nrm((DEPTH, CONV_W, CONV_CH), CONV_W ** -0.5),
        'gdn_a_log': a_log,
        'gdn_dt_bias': dt_bias,
        'gdn_norm': gain((DEPTH, GDN_DV)),
        'mla_g_q': gain((DEPTH, Q_LORA)),
        'mla_w_uq': nrm((DEPTH, Q_LORA, MLA_Q_W), Q_LORA ** -0.5),
        'mla_g_kv': gain((DEPTH, KV_LORA)),
        'mla_w_uk': nrm((DEPTH, KV_LORA, MLA_HEADS, QK_NOPE), KV_LORA ** -0.5),
        'mla_w_uv': nrm((DEPTH, KV_LORA, MLA_HEADS, V_DIM), KV_LORA ** -0.5),
        'w_out': nrm((DEPTH, MIX_WIDTH, D_MODEL), MIX_WIDTH ** -0.5),
        'g_ffn': gain((DEPTH, D_MODEL)),
        'w_router': nrm((DEPTH, D_MODEL, N_EXPERTS), D_MODEL ** -0.5),
        'router_bias': nrm((DEPTH, N_EXPERTS), 0.01),
        'w_exp_gate': nrm((DEPTH, N_EXPERTS, D_MODEL, D_EXPERT), D_MODEL ** -0.5),
        'w_exp_up': nrm((DEPTH, N_EXPERTS, D_MODEL, D_EXPERT), D_MODEL ** -0.5),
        'w_exp_down': nrm((DEPTH, N_EXPERTS, D_EXPERT, D_MODEL), D_EXPERT ** -0.5),
        'w_sh_gate': nrm((DEPTH, D_MODEL, D_SHARED), D_MODEL ** -0.5),
        'w_sh_up': nrm((DEPTH, D_MODEL, D_SHARED), D_MODEL ** -0.5),
        'w_sh_down': nrm((DEPTH, D_SHARED, D_MODEL), D_SHARED ** -0.5),
        'g_ple': gain((DEPTH, D_MODEL)),
        'w_ple_gate': nrm((DEPTH, D_MODEL, D_MODEL), D_MODEL ** -0.5),
        'w_ple_proj': nrm((DEPTH, PLE_DIM, D_MODEL), PLE_DIM ** -0.5),
        'g_final': gain((D_MODEL,)),
    }


def reference(x_prompt, x_sample, p_prompt, p_sample, cache_ckv, cache_krope, state_gdn, state_conv,
              page_table, g_mix, w_in, gdn_conv_w, gdn_a_log, gdn_dt_bias, gdn_norm, mla_g_q, mla_w_uq,
              mla_g_kv, mla_w_uk, mla_w_uv, w_out, g_ffn, w_router, router_bias, w_exp_gate, w_exp_up,
              w_exp_down, w_sh_gate, w_sh_up, w_sh_down, g_ple, w_ple_gate, w_ple_proj, g_final):
    hp, hs = x_prompt, x_sample
    bp, tp = hp.shape[0], hp.shape[1]
    ts = hs.shape[1]
    past = page_table.shape[1] * cache_ckv.shape[2]
    pos_p = jnp.arange(tp)
    pos_s = past + jnp.arange(ts)
    ckv_p, kr_p, gdn_p, conv_p = [], [], [], []
    ckv_s, kr_s, gdn_s, conv_s = [], [], [], []
    for i in range(DEPTH):
        lw = dict(g_mix=g_mix[i], w_in=w_in[i], gdn_conv_w=gdn_conv_w[i], gdn_a_log=gdn_a_log[i],
                  gdn_dt_bias=gdn_dt_bias[i], gdn_norm=gdn_norm[i], mla_g_q=mla_g_q[i], mla_w_uq=mla_w_uq[i],
                  mla_g_kv=mla_g_kv[i], mla_w_uk=mla_w_uk[i], mla_w_uv=mla_w_uv[i], w_out=w_out[i],
                  g_ffn=g_ffn[i], w_router=w_router[i], router_bias=router_bias[i], w_exp_gate=w_exp_gate[i],
                  w_exp_up=w_exp_up[i], w_exp_down=w_exp_down[i], w_sh_gate=w_sh_gate[i], w_sh_up=w_sh_up[i],
                  w_sh_down=w_sh_down[i], g_ple=g_ple[i], w_ple_gate=w_ple_gate[i], w_ple_proj=w_ple_proj[i])
        gdn0 = jnp.zeros((bp, GDN_HEADS, GDN_DK, GDN_DV), state_gdn.dtype)
        conv0 = jnp.zeros((bp, CONV_W - 1, CONV_CH), state_conv.dtype)
        hp, c1, k1, s1, v1 = hybrid_layer(hp, p_prompt[i], pos_p, gdn0, conv0, mla_prompt_attend, lw)
        attend_s = functools.partial(mla_paged_attend, cache_ckv=cache_ckv[i], cache_krope=cache_krope[i],
                                     page_table=page_table)
        hs, c2, k2, s2, v2 = hybrid_layer(hs, p_sample[i], pos_s, state_gdn[i], state_conv[i], attend_s, lw)
        ckv_p.append(c1); kr_p.append(k1); gdn_p.append(s1.astype(state_gdn.dtype)); conv_p.append(v1.astype(state_conv.dtype))
        ckv_s.append(c2); kr_s.append(k2); gdn_s.append(s2.astype(state_gdn.dtype)); conv_s.append(v2.astype(state_conv.dtype))
    y_prompt = rmsnorm(hp, g_final)
    y_sample = rmsnorm(hs, g_final)
    return (y_prompt, y_sample,
            jnp.stack(ckv_p), jnp.stack(kr_p), jnp.stack(gdn_p), jnp.stack(conv_p),
            jnp.stack(ckv_s), jnp.stack(kr_s), jnp.stack(gdn_s), jnp.stack(conv_s))
```

```python
import functools

import jax
import jax.numpy as jnp
import numpy as np
from jax import lax
from jax.experimental import pallas as pl
from jax.experimental.pallas import tpu as pltpu

EPS = 1e-6
ROPE_THETA = 10000.0
GDN_HEADS = 4
GDN_DK = 128
GDN_DV = 128
CONV_W = 4
MLA_HEADS = 4
Q_LORA = 256
KV_LORA = 256
QK_NOPE = 128
QK_ROPE = 64
V_DIM = 128
N_EXPERTS = 64
TOP_K = 8
N_GROUPS = 8
TOPK_GROUPS = 4
GROUP_SIZE = N_EXPERTS // N_GROUPS
D_EXPERT = 256
D_SHARED = 256
ROUTED_SCALE = 2.5
MLA_SCALE = (QK_NOPE + QK_ROPE) ** -0.5

GDN_QK_W = GDN_HEADS * GDN_DK
GDN_V_W = GDN_HEADS * GDN_DV
CONV_CH = 2 * GDN_QK_W + GDN_V_W
LANE = 128
SUBLANE = 8
QK_PAD = KV_LORA + LANE
VMEM_LIMIT = 56 * 1024 * 1024

C_QKV = 0
C_Z = C_QKV + CONV_CH
C_CQ = C_Z + GDN_V_W
C_CKV = C_CQ + Q_LORA
C_KR = C_CKV + KV_LORA
C_KRR = C_KR + QK_ROPE
C_BA = C_KRR + QK_ROPE
W1_WIDTH = C_BA + LANE

F32 = jnp.float32
BF16 = jnp.bfloat16
HI = lax.Precision.HIGHEST
NT = (((1,), (1,)), ((), ()))
TN = (((0,), (0,)), ((), ()))


def _cparams(sem):
    return pltpu.CompilerParams(dimension_semantics=sem, vmem_limit_bytes=VMEM_LIMIT)


def _rms(x, g):
    return x * lax.rsqrt(jnp.mean(x * x, axis=-1, keepdims=True) + EPS) * g


def _sigmoid(x):
    return 1.0 / (1.0 + jnp.exp(-x))


def _silu(x):
    return x * _sigmoid(x)


def _bdot(a, b):
    return jnp.dot(a.astype(BF16), b.astype(BF16), preferred_element_type=F32)


def _full(shape):
    return pl.BlockSpec(shape, lambda *_: (0,) * len(shape))


def _in_proj_kernel(x_ref, gmix_ref, w1_ref, alog_ref, dtb_ref, gq_ref, wuq_ref, wuk_ref, gkv_ref,
                    cos_ref, sin_ref,
                    qkv_ref, z_ref, gb_ref, ckv_ref, krope_ref, kv_ref, q_ref):
    u = _rms(x_ref[...], gmix_ref[...]).astype(BF16)
    qkv_ref[...] = jnp.dot(u, w1_ref[:, C_QKV:C_Z], preferred_element_type=F32)
    z_ref[...] = jnp.dot(u, w1_ref[:, C_Z:C_CQ], preferred_element_type=F32)
    rest = jnp.dot(u, w1_ref[:, C_CQ:W1_WIDTH], preferred_element_type=F32)
    c_q = rest[:, 0:Q_LORA]
    c_kv = rest[:, C_CKV - C_CQ:C_KR - C_CQ]
    kr2 = rest[:, C_KR - C_CQ:C_BA - C_CQ]
    ba = rest[:, C_BA - C_CQ:]

    lane = lax.broadcasted_iota(jnp.int32, ba.shape, 1)
    beta = _sigmoid(ba)
    sp_in = ba + dtb_ref[...]
    softplus = jnp.maximum(sp_in, 0.0) + jnp.log(1.0 + jnp.exp(-jnp.abs(sp_in)))
    g = -jnp.exp(alog_ref[...]) * softplus
    gb_ref[...] = jnp.where(lane < GDN_HEADS, beta, g)

    cos = cos_ref[...]
    sin = sin_ref[...]
    ckv = _rms(c_kv, gkv_ref[...])
    ckv_ref[...] = ckv
    krope = kr2[:, :QK_ROPE] * cos[:, :QK_ROPE] + kr2[:, QK_ROPE:] * sin[:, :QK_ROPE]
    krope_ref[...] = krope
    zpad = jnp.zeros((ckv.shape[0], LANE - QK_ROPE), F32)
    kv_ref[...] = jnp.concatenate([ckv, krope, zpad], axis=1).astype(BF16)

    cqn = _rms(c_q, gq_ref[...]).astype(BF16)
    qh = jnp.dot(cqn, wuq_ref[...], preferred_element_type=F32)
    nope_w = MLA_HEADS * QK_NOPE
    rope_w = MLA_HEADS * QK_ROPE
    q_rope = qh[:, nope_w:nope_w + rope_w] * cos + qh[:, nope_w + rope_w:] * sin
    parts = []
    for h in range(MLA_HEADS):
        q_nope = qh[:, h * QK_NOPE:(h + 1) * QK_NOPE].astype(BF16)
        q_lat = jnp.dot(q_nope, wuk_ref[h], preferred_element_type=F32)
        parts += [q_lat, q_rope[:, h * QK_ROPE:(h + 1) * QK_ROPE], zpad]
    q_ref[...] = (jnp.concatenate(parts, axis=1) * MLA_SCALE).astype(BF16)


def _in_proj(x, wts, cos, sin, tm):
    n, d = x.shape
    period = cos.shape[0]
    nper = period // tm
    tok = lambda w: pl.BlockSpec((tm, w), lambda i: (i, 0))
    pos = pl.BlockSpec((tm, cos.shape[1]), lambda i: (i % nper, 0))
    out_shapes = (
        jax.ShapeDtypeStruct((n, CONV_CH), F32),
        jax.ShapeDtypeStruct((n, GDN_V_W), F32),
        jax.ShapeDtypeStruct((n, LANE), F32),
        jax.ShapeDtypeStruct((n, KV_LORA), F32),
        jax.ShapeDtypeStruct((n, QK_ROPE), F32),
        jax.ShapeDtypeStruct((n, QK_PAD), BF16),
        jax.ShapeDtypeStruct((n, MLA_HEADS * QK_PAD), BF16),
    )
    return pl.pallas_call(
        _in_proj_kernel,
        grid=(n // tm,),
        in_specs=[tok(d), _full((1, d)), _full(wts['w1'].shape), _full((1, LANE)), _full((1, LANE)),
                  _full((1, Q_LORA)), _full(wts['wuq'].shape), _full(wts['wuk'].shape), _full((1, KV_LORA)),
                  pos, pos],
        out_specs=[tok(CONV_CH), tok(GDN_V_W), tok(LANE), tok(KV_LORA), tok(QK_ROPE), tok(QK_PAD),
                   tok(MLA_HEADS * QK_PAD)],
        out_shape=out_shapes,
        compiler_params=_cparams(("arbitrary",)),
        name="in_proj",
    )(x, wts['g_mix'], wts['w1'], wts['alog'], wts['dtb'], wts['g_q'], wts['wuq'], wts['wuk'], wts['g_kv'],
      cos, sin)


def _gdn_kernel(qkv_ref, z_ref, gb_ref, s0_ref, conv0_ref, cw_ref, gn_ref,
                o_ref, sout_ref, s_scr, xbuf, *, chunk):
    c = pl.program_id(1)
    nc = pl.num_programs(1)
    pre = SUBLANE
    tail = CONV_W - 1

    @pl.when(c == 0)
    def _():
        s_scr[...] = s0_ref[...]
        xbuf[0:pre, :] = conv0_ref[...]

    xbuf[pre:pre + chunk, :] = qkv_ref[...]
    y = xbuf[pre - tail:pre - tail + chunk, :] * cw_ref[0:1, :]
    for j in range(1, CONV_W):
        y = y + xbuf[pre - tail + j:pre - tail + j + chunk, :] * cw_ref[j:j + 1, :]
    y = _silu(y)
    xbuf[pre - tail:pre, :] = xbuf[pre + chunk - tail:pre + chunk, :]

    gb = gb_ref[...]
    row = lax.broadcasted_iota(jnp.int32, (chunk, chunk), 0)
    col = lax.broadcasted_iota(jnp.int32, (chunk, chunk), 1)
    incl = row >= col
    strict = row > col
    gc_all = jnp.dot(incl.astype(F32), gb, precision=HI, preferred_element_type=F32)
    gc_rows = lax.dot_general(gb, (row <= col).astype(F32), TN, precision=HI,
                              preferred_element_type=F32)
    eye = (row == col).astype(F32)

    for h in range(GDN_HEADS):
        q = y[:, h * GDN_DK:(h + 1) * GDN_DK]
        k = y[:, GDN_QK_W + h * GDN_DK:GDN_QK_W + (h + 1) * GDN_DK]
        v = y[:, 2 * GDN_QK_W + h * GDN_DV:2 * GDN_QK_W + (h + 1) * GDN_DV]
        qn = q * lax.rsqrt(jnp.sum(q * q, axis=-1, keepdims=True) + EPS) * (GDN_DK ** -0.5)
        kn = k * lax.rsqrt(jnp.sum(k * k, axis=-1, keepdims=True) + EPS)
        beta = gb[:, h:h + 1]
        gcol = gc_all[:, GDN_HEADS + h:GDN_HEADS + h + 1]
        grow = gc_rows[GDN_HEADS + h:GDN_HEADS + h + 1, :]
        glast = gc_all[chunk - 1:chunk, GDN_HEADS + h:GDN_HEADS + h + 1]
        decay = jnp.exp(jnp.where(incl, gcol - grow, -jnp.inf))
        kb = kn * beta
        kk = lax.dot_general(kb, kn, NT, precision=HI, preferred_element_type=F32)
        lower = jnp.where(strict, kk * decay, 0.0)
        inv = eye
        s = 1
        while s < chunk:
            sel = ((row // s) % 2 == 1) & ((col // s) == (row // s) - 1)
            blk = jnp.where(sel, lower, 0.0)
            t = jnp.dot(inv, blk, precision=HI, preferred_element_type=F32)
            inv = inv - jnp.dot(t, inv, precision=HI, preferred_element_type=F32)
            s *= 2
        egc = jnp.exp(gcol)
        u = jnp.dot(inv, v * beta, precision=HI, preferred_element_type=F32)
        w = jnp.dot(inv, kb * egc, precision=HI, preferred_element_type=F32)
        qk = lax.dot_general(qn, kn, NT, precision=HI, preferred_element_type=F32)
        qk = jnp.where(incl, qk * decay, 0.0)
        sh = s_scr[h]
        v_new = u - jnp.dot(w, sh, precision=HI, preferred_element_type=F32)
        o = (jnp.dot(qn * egc, sh, precision=HI, preferred_element_type=F32)
             + jnp.dot(qk, v_new, precision=HI, preferred_element_type=F32))
        kdec = kn * jnp.exp(glast - gcol)
        s_scr[h] = sh * jnp.exp(glast) + lax.dot_general(kdec, v_new, TN, precision=HI,
                                                          preferred_element_type=F32)
        on = _rms(o, gn_ref[...])
        zh = z_ref[:, h * GDN_DV:(h + 1) * GDN_DV]
        o_ref[:, h * GDN_DV:(h + 1) * GDN_DV] = (on * _silu(zh)).astype(o_ref.dtype)

    @pl.when(c == nc - 1)
    def _():
        sout_ref[...] = s_scr[...]


def _gdn(qkv, z, gb, s0, conv0, cw, gn, chunk):
    b, t, _ = qkv.shape
    assert t % chunk == 0
    nc = t // chunk
    tokb = lambda w: pl.BlockSpec((None, chunk, w), lambda i, c: (i, c, 0))
    state = pl.BlockSpec((None, GDN_HEADS, GDN_DK, GDN_DV), lambda i, c: (i, 0, 0, 0))
    return pl.pallas_call(
        functools.partial(_gdn_kernel, chunk=chunk),
        grid=(b, nc),
        in_specs=[tokb(CONV_CH), tokb(GDN_V_W), tokb(LANE), state,
                  pl.BlockSpec((None, SUBLANE, CONV_CH), lambda i, c: (i, 0, 0)),
                  _full((SUBLANE, CONV_CH)), _full((1, GDN_DV))],
        out_specs=[tokb(GDN_V_W), state],
        out_shape=(jax.ShapeDtypeStruct((b, t, GDN_V_W), BF16),
                   jax.ShapeDtypeStruct((b, GDN_HEADS, GDN_DK, GDN_DV), F32)),
        scratch_shapes=[pltpu.VMEM((GDN_HEADS, GDN_DK, GDN_DV), F32),
                        pltpu.VMEM((SUBLANE + chunk, CONV_CH), F32)],
        compiler_params=_cparams(("arbitrary", "arbitrary")),
        name="gdn",
    )(qkv, z, gb, s0, conv0, cw, gn)


def _stack_heads(q):
    return jnp.concatenate([q[:, h * QK_PAD:(h + 1) * QK_PAD] for h in range(MLA_HEADS)], axis=0)


def _online_softmax_step(s, vals, m_scr, l_scr, acc_scr):
    m_prev = m_scr[...]
    m_new = jnp.maximum(m_prev, jnp.max(s, axis=-1, keepdims=True))
    alpha = jnp.exp(m_prev - m_new)
    p = jnp.exp(s - m_new)
    l_scr[...] = alpha * l_scr[...] + jnp.sum(p, axis=-1, keepdims=True)
    acc_scr[...] = alpha * acc_scr[...] + jnp.dot(p.astype(BF16), vals, preferred_element_type=F32)
    m_scr[...] = m_new


def _unstack_store(o_ref, acc_scr, l_scr, rows):
    o = acc_scr[...] / l_scr[...]
    for h in range(MLA_HEADS):
        o_ref[:, h * KV_LORA:(h + 1) * KV_LORA] = o[h * rows:(h + 1) * rows, :].astype(o_ref.dtype)


def _prompt_attn_kernel(q_ref, kv_ref, o_ref, qs_scr, m_scr, l_scr, acc_scr, *, tq, tk):
    i = pl.program_id(1)
    j = pl.program_id(2)
    nj = pl.num_programs(2)

    @pl.when(j == 0)
    def _():
        qs_scr[...] = _stack_heads(q_ref[...])
        m_scr[...] = jnp.full(m_scr.shape, -jnp.inf, F32)
        l_scr[...] = jnp.zeros(l_scr.shape, F32)
        acc_scr[...] = jnp.zeros(acc_scr.shape, F32)

    @pl.when(j * tk <= i * tq + tq - 1)
    def _():
        kv = kv_ref[...]
        s = lax.dot_general(qs_scr[...], kv, NT, preferred_element_type=F32)
        qpos = i * tq + lax.broadcasted_iota(jnp.int32, s.shape, 0) % tq
        kpos = j * tk + lax.broadcasted_iota(jnp.int32, s.shape, 1)
        s = jnp.where(kpos <= qpos, s, -jnp.inf)
        _online_softmax_step(s, kv[:, :KV_LORA], m_scr, l_scr, acc_scr)

    @pl.when(j == nj - 1)
    def _():
        _unstack_store(o_ref, acc_scr, l_scr, tq)


def _prompt_attn(q, kv, tq, tk):
    b, s, _ = q.shape
    rows = MLA_HEADS * tq
    last = lambda i: (i * tq + tq - 1) // tk
    return pl.pallas_call(
        functools.partial(_prompt_attn_kernel, tq=tq, tk=tk),
        grid=(b, s // tq, s // tk),
        in_specs=[pl.BlockSpec((None, tq, MLA_HEADS * QK_PAD), lambda bb, i, j: (bb, i, 0)),
                  pl.BlockSpec((None, tk, QK_PAD), lambda bb, i, j: (bb, jnp.minimum(j, last(i)), 0))],
        out_specs=pl.BlockSpec((None, tq, MLA_HEADS * KV_LORA), lambda bb, i, j: (bb, i, 0)),
        out_shape=jax.ShapeDtypeStruct((b, s, MLA_HEADS * KV_LORA), BF16),
        scratch_shapes=[pltpu.VMEM((rows, QK_PAD), BF16), pltpu.VMEM((rows, 1), F32),
                        pltpu.VMEM((rows, 1), F32), pltpu.VMEM((rows, KV_LORA), F32)],
        compiler_params=_cparams(("arbitrary", "arbitrary", "arbitrary")),
        name="prompt_attn",
    )(q, kv)


def _paged_attn_kernel(pt_ref, q_ref, kvnew_ref, *rest, pages_per_step, t):
    ckv_refs = rest[:pages_per_step]
    kr_refs = rest[pages_per_step:2 * pages_per_step]
    o_ref, qs_scr, m_scr, l_scr, acc_scr = rest[2 * pages_per_step:]
    g = pl.program_id(1)
    ng = pl.num_programs(1)

    @pl.when(g == 0)
    def _():
        qs_scr[...] = _stack_heads(q_ref[...])
        m_scr[...] = jnp.full(m_scr.shape, -jnp.inf, F32)
        l_scr[...] = jnp.zeros(l_scr.shape, F32)
        acc_scr[...] = jnp.zeros(acc_scr.shape, F32)

    qs = qs_scr[...]
    for p in range(pages_per_step):
        ckv = ckv_refs[p][...].astype(BF16)
        kr = kr_refs[p][...].astype(BF16)
        s = (lax.dot_general(qs[:, :KV_LORA], ckv, NT, preferred_element_type=F32)
             + lax.dot_general(qs[:, KV_LORA:KV_LORA + QK_ROPE], kr, NT, preferred_element_type=F32))
        _online_softmax_step(s, ckv, m_scr, l_scr, acc_scr)

    @pl.when(g == ng - 1)
    def _():
        kvn = kvnew_ref[...]
        s = lax.dot_general(qs, kvn, NT, preferred_element_type=F32)
        qpos = lax.broadcasted_iota(jnp.int32, s.shape, 0) % t
        kpos = lax.broadcasted_iota(jnp.int32, s.shape, 1)
        s = jnp.where(kpos <= qpos, s, -jnp.inf)
        _online_softmax_step(s, kvn[:, :KV_LORA], m_scr, l_scr, acc_scr)
        _unstack_store(o_ref, acc_scr, l_scr, t)


def _paged_attn(q, kvnew, cache_ckv, cache_krope, page_table, pages_per_step):
    b, t, _ = q.shape
    n_pages = page_table.shape[1]
    page = cache_ckv.shape[1]
    assert n_pages % pages_per_step == 0
    rows = MLA_HEADS * t

    def page_spec(width, p):
        return pl.BlockSpec((None, page, width),
                            lambda bb, g, pt: (pt[bb * n_pages + g * pages_per_step + p], 0, 0))

    grid_spec = pltpu.PrefetchScalarGridSpec(
        num_scalar_prefetch=1,
        grid=(b, n_pages // pages_per_step),
        in_specs=([pl.BlockSpec((None, t, MLA_HEADS * QK_PAD), lambda bb, g, pt: (bb, 0, 0)),
                   pl.BlockSpec((None, t, QK_PAD), lambda bb, g, pt: (bb, 0, 0))]
                  + [page_spec(KV_LORA, p) for p in range(pages_per_step)]
                  + [page_spec(QK_ROPE, p) for p in range(pages_per_step)]),
        out_specs=pl.BlockSpec((None, t, MLA_HEADS * KV_LORA), lambda bb, g, pt: (bb, 0, 0)),
        scratch_shapes=[pltpu.VMEM((rows, QK_PAD), BF16), pltpu.VMEM((rows, 1), F32),
                        pltpu.VMEM((rows, 1), F32), pltpu.VMEM((rows, KV_LORA), F32)],
    )
    return pl.pallas_call(
        functools.partial(_paged_attn_kernel, pages_per_step=pages_per_step, t=t),
        grid_spec=grid_spec,
        out_shape=jax.ShapeDtypeStruct((b, t, MLA_HEADS * KV_LORA), BF16),
        compiler_params=_cparams(("arbitrary", "arbitrary")),
        name="paged_attn",
    )(page_table.reshape(-1), q, kvnew, *([cache_ckv] * pages_per_step), *([cache_krope] * pages_per_step))


def _pack_bf16_pairs(x):
    w = x.shape[1] // 2
    xb = x.astype(BF16).astype(F32)
    lo = lax.shift_right_logical(lax.bitcast_convert_type(xb[:, :w], jnp.uint32), jnp.uint32(16))
    hi = lax.bitcast_convert_type(xb[:, w:], jnp.uint32) & jnp.uint32(0xFFFF0000)
    return lo | hi


def _unpack_bf16_pairs(p):
    lo = lax.bitcast_convert_type(lax.shift_left(p, jnp.uint32(16)), F32)
    hi = lax.bitcast_convert_type(p & jnp.uint32(0xFFFF0000), F32)
    return jnp.concatenate([lo, hi], axis=1)


def _post_mix_kernel(h_ref, ogdn_ref, olat_ref, wuv_ref, wout_ref, gffn_ref, wrh_ref, wrl_ref, rbias_ref,
                     wsgu_ref, wsd_ref,
                     h1_ref, xp_ref, idx_ref, rank_ref, wtok_ref, cnt_ref, carry):
    i = pl.program_id(0)
    tm = h_ref.shape[0]

    @pl.when(i == 0)
    def _():
        carry[...] = jnp.zeros(carry.shape, F32)

    parts = [ogdn_ref[...]]
    for hh in range(MLA_HEADS):
        o_h = jnp.dot(olat_ref[:, hh * KV_LORA:(hh + 1) * KV_LORA], wuv_ref[hh], preferred_element_type=F32)
        parts.append(o_h.astype(BF16))
    mix = jnp.concatenate(parts, axis=1)
    h1 = h_ref[...] + jnp.dot(mix, wout_ref[...], preferred_element_type=F32)
    u2 = _rms(h1, gffn_ref[...])
    u2b = u2.astype(BF16)
    xp_ref[...] = _pack_bf16_pairs(u2)

    u2l = (u2 - u2b.astype(F32)).astype(BF16)
    logits = (lax.dot_general(wrh_ref[...], u2b, NT, preferred_element_type=F32)
              + lax.dot_general(wrh_ref[...], u2l, NT, preferred_element_type=F32)
              + lax.dot_general(wrl_ref[...], u2b, NT, preferred_element_type=F32))
    scores = _sigmoid(logits)
    biased = scores + rbias_ref[...]
    neg = -jnp.inf

    sub = lax.broadcasted_iota(jnp.int32, (GROUP_SIZE, tm), 0)
    gscore = []
    for gi in range(N_GROUPS):
        vg = biased[gi * GROUP_SIZE:(gi + 1) * GROUP_SIZE, :]
        m1 = jnp.max(vg, axis=0, keepdims=True)
        first = jnp.min(jnp.where(vg == m1, sub, GROUP_SIZE), axis=0, keepdims=True)
        m2 = jnp.max(jnp.where(sub == first, neg, vg), axis=0, keepdims=True)
        gscore.append(m1 + m2)
    gs = jnp.concatenate(gscore, axis=0)
    gid = lax.broadcasted_iota(jnp.int32, gs.shape, 0)
    gsel = jnp.zeros(gs.shape, F32)
    for _ in range(TOPK_GROUPS):
        m = jnp.max(gs, axis=0, keepdims=True)
        pick = jnp.min(jnp.where(gs == m, gid, N_GROUPS), axis=0, keepdims=True)
        hit = gid == pick
        gsel = jnp.where(hit, 1.0, gsel)
        gs = jnp.where(hit, neg, gs)
    emask = jnp.concatenate(
        [jnp.broadcast_to(gsel[gi:gi + 1, :], (GROUP_SIZE, tm)) for gi in range(N_GROUPS)], axis=0)
    cand = jnp.where(emask > 0.5, biased, neg)

    eid = lax.broadcasted_iota(jnp.int32, cand.shape, 0)
    chosen = jnp.zeros(cand.shape, F32)
    idx_rows, w_rows, hits = [], [], []
    for _ in range(TOP_K):
        m = jnp.max(cand, axis=0, keepdims=True)
        pick = jnp.min(jnp.where(cand == m, eid, N_EXPERTS), axis=0, keepdims=True)
        hit = eid == pick
        idx_rows.append(pick)
        w_rows.append(jnp.sum(jnp.where(hit, scores, 0.0), axis=0, keepdims=True))
        hits.append(hit)
        chosen = chosen + hit.astype(F32)
        cand = jnp.where(hit, neg, cand)
    wsel = jnp.concatenate(w_rows, axis=0)
    wsel = wsel / (jnp.sum(wsel, axis=0, keepdims=True) + 1e-20) * ROUTED_SCALE
    idx_ref[...] = jnp.concatenate(idx_rows, axis=0)

    ti = lax.broadcasted_iota(jnp.int32, (tm, tm), 0)
    tj = lax.broadcasted_iota(jnp.int32, (tm, tm), 1)
    before = (ti < tj).astype(BF16)
    prefix = jnp.dot(chosen.astype(BF16), before, preferred_element_type=F32) + carry[:, 0:1]
    rank_rows = [jnp.sum(jnp.where(hit, prefix, 0.0), axis=0, keepdims=True) for hit in hits]
    rank_ref[...] = jnp.concatenate(rank_rows, axis=0).astype(jnp.int32)
    carry[...] = carry[...] + jnp.sum(chosen, axis=1, keepdims=True)
    cnt_ref[...] = carry[...]

    wpad = jnp.concatenate([wsel, jnp.zeros((LANE - TOP_K, tm), F32)], axis=0)
    wtok_ref[...] = wpad.T

    gu = jnp.dot(u2b, wsgu_ref[...], preferred_element_type=F32)
    hs = (_silu(gu[:, :D_SHARED]) * gu[:, D_SHARED:]).astype(BF16)
    h1_ref[...] = h1 + jnp.dot(hs, wsd_ref[...], preferred_element_type=F32)


def _post_mix(h, o_gdn, o_lat, wts, tm):
    n, d = h.shape
    tok = lambda w: pl.BlockSpec((tm, w), lambda i: (i, 0))
    tr = lambda r: pl.BlockSpec((r, tm), lambda i: (0, i))
    return pl.pallas_call(
        _post_mix_kernel,
        grid=(n // tm,),
        in_specs=[tok(d), tok(GDN_V_W), tok(MLA_HEADS * KV_LORA), _full(wts['wuv'].shape), _full((d, d)),
                  _full((1, d)), _full((N_EXPERTS, d)), _full((N_EXPERTS, d)), _full((N_EXPERTS, 1)),
                  _full((d, 2 * D_SHARED)), _full((D_SHARED, d))],
        out_specs=[tok(d), tok(d // 2), tr(TOP_K), tr(TOP_K), tok(LANE), _full((N_EXPERTS, LANE))],
        out_shape=(jax.ShapeDtypeStruct((n, d), F32),
                   jax.ShapeDtypeStruct((n, d // 2), jnp.uint32),
                   jax.ShapeDtypeStruct((TOP_K, n), jnp.int32),
                   jax.ShapeDtypeStruct((TOP_K, n), jnp.int32),
                   jax.ShapeDtypeStruct((n, LANE), F32),
                   jax.ShapeDtypeStruct((N_EXPERTS, LANE), F32)),
        scratch_shapes=[pltpu.VMEM((N_EXPERTS, LANE), F32)],
        compiler_params=_cparams(("arbitrary",)),
        name="post_mix",
    )(h, o_gdn, o_lat, wts['wuv'], wts['w_out'], wts['g_ffn'], wts['wr_hi'], wts['wr_lo'], wts['rbias'],
      wts['ws_gu'], wts['ws_d'])


def _row_copy(src_ref, dst_ref, sem):
    return pltpu.make_async_copy(src_ref, dst_ref, sem)


def _scatter_kernel(dest_ref, x_ref, o_hbm, sem, *, fan):
    nrow = x_ref.shape[0]

    def start(r, carry):
        for kk in range(fan):
            _row_copy(x_ref.at[pl.ds(r, 1)], o_hbm.at[pl.ds(dest_ref[0, 0, r * fan + kk], 1)], sem).start()
        return carry

    lax.fori_loop(0, nrow, start, 0)

    def wait(r, carry):
        for kk in range(fan):
            _row_copy(x_ref.at[pl.ds(0, 1)], o_hbm.at[pl.ds(0, 1)], sem).wait()
        return carry

    lax.fori_loop(0, nrow, wait, 0)


def _scatter_rows(x, dest, fan, ts):
    n, w = x.shape
    return pl.pallas_call(
        functools.partial(_scatter_kernel, fan=fan),
        grid=(n // ts,),
        in_specs=[pl.BlockSpec((1, 1, ts * fan), lambda i: (i, 0, 0), memory_space=pltpu.SMEM),
                  pl.BlockSpec((ts, w), lambda i: (i, 0))],
        out_specs=pl.BlockSpec(memory_space=pl.ANY),
        out_shape=jax.ShapeDtypeStruct((n * fan, w), x.dtype),
        scratch_shapes=[pltpu.SemaphoreType.DMA],
        compiler_params=pltpu.CompilerParams(dimension_semantics=("arbitrary",), has_side_effects=True),
        name="scatter_rows",
    )(dest.reshape(n // ts, 1, ts * fan), x)


def _gather_kernel(src_ref, y_hbm, o_ref, sem):
    nrow = o_ref.shape[0]

    def start(r, carry):
        _row_copy(y_hbm.at[pl.ds(src_ref[0, 0, r], 1)], o_ref.at[pl.ds(r, 1)], sem).start()
        return carry

    lax.fori_loop(0, nrow, start, 0)

    def wait(r, carry):
        _row_copy(y_hbm.at[pl.ds(0, 1)], o_ref.at[pl.ds(0, 1)], sem).wait()
        return carry

    lax.fori_loop(0, nrow, wait, 0)


def _gather_rows(y, src, tr):
    nr = src.shape[0]
    w = y.shape[1]
    return pl.pallas_call(
        _gather_kernel,
        grid=(nr // tr,),
        in_specs=[pl.BlockSpec((1, 1, tr), lambda i: (i, 0, 0), memory_space=pltpu.SMEM),
                  pl.BlockSpec(memory_space=pl.ANY)],
        out_specs=pl.BlockSpec((tr, w), lambda i: (i, 0)),
        out_shape=jax.ShapeDtypeStruct((nr, w), y.dtype),
        scratch_shapes=[pltpu.SemaphoreType.DMA],
        compiler_params=pltpu.CompilerParams(dimension_semantics=("arbitrary",)),
        name="gather_rows",
    )(src.reshape(nr // tr, 1, tr), y)


def _experts_kernel(vt_ref, ve_ref, seg_ref, nv_ref, x_ref, wgu_ref, wd_ref, o_ref, *, bm):
    v = pl.program_id(0)
    tile = vt_ref[v]
    e = ve_ref[v]
    prev_tile = vt_ref[jnp.maximum(v - 1, 0)]
    first = (v == 0) | (tile != prev_tile)

    @pl.when(v < nv_ref[0])
    def _():
        x = _unpack_bf16_pairs(x_ref[...]).astype(BF16)
        gu = jnp.dot(x, wgu_ref[...], preferred_element_type=F32)
        hid = (_silu(gu[:, :D_EXPERT]) * gu[:, D_EXPERT:]).astype(BF16)
        y = _pack_bf16_pairs(jnp.dot(hid, wd_ref[...], preferred_element_type=F32))
        rowid = tile * bm + lax.broadcasted_iota(jnp.int32, y.shape, 0)
        mine = (rowid >= seg_ref[e]) & (rowid < seg_ref[e + 1])

        @pl.when(first)
        def _():
            o_ref[...] = jnp.where(mine, y, jnp.uint32(0))

        @pl.when(jnp.logical_not(first))
        def _():
            o_ref[...] = jnp.where(mine, y, o_ref[...])


def _experts(xs, visit_tile, visit_expert, seg, n_visits, wgu, wd, bm):
    rows, w = xs.shape
    nv_max = visit_tile.shape[0]
    d = wd.shape[2]
    grid_spec = pltpu.PrefetchScalarGridSpec(
        num_scalar_prefetch=4,
        grid=(nv_max,),
        in_specs=[pl.BlockSpec((bm, w), lambda v, vt, ve, sg, nv: (vt[v], 0)),
                  pl.BlockSpec((None, d, 2 * D_EXPERT), lambda v, vt, ve, sg, nv: (ve[v], 0, 0)),
                  pl.BlockSpec((None, D_EXPERT, d), lambda v, vt, ve, sg, nv: (ve[v], 0, 0))],
        out_specs=pl.BlockSpec((bm, w), lambda v, vt, ve, sg, nv: (vt[v], 0)),
    )
    return pl.pallas_call(
        functools.partial(_experts_kernel, bm=bm),
        grid_spec=grid_spec,
        out_shape=jax.ShapeDtypeStruct((rows, w), jnp.uint32),
        compiler_params=_cparams(("arbitrary",)),
        name="experts",
    )(visit_tile, visit_expert, seg, n_visits, xs, wgu, wd)


def _visit_list(counts, rows, bm):
    ntiles = rows // bm
    nv_max = ntiles + N_EXPERTS - 1
    ends = jnp.cumsum(counts)
    starts = ends - counts
    first_tile = starts // bm
    ntile_e = jnp.where(counts > 0, (ends - 1) // bm - first_tile + 1, 0)
    vend = jnp.cumsum(ntile_e)
    vstart = vend - ntile_e
    n_visits = vend[-1]
    v = jnp.arange(nv_max, dtype=jnp.int32)
    ve = jnp.minimum(jnp.searchsorted(vend, v, side='right'), N_EXPERTS - 1).astype(jnp.int32)
    vt = (first_tile[ve] + v - vstart[ve]).astype(jnp.int32)
    last = jnp.maximum(n_visits - 1, 0)
    ve = jnp.where(v < n_visits, ve, ve[last])
    vt = jnp.where(v < n_visits, vt, vt[last])
    seg = jnp.concatenate([starts, ends[-1:]]).astype(jnp.int32)
    return vt, ve, seg, n_visits.reshape(1).astype(jnp.int32), starts


def _final_kernel(h1_ref, yk_ref, wtok_ref, p_ref, gple_ref, wgate_ref, wproj_ref, gfin_ref, o_ref):
    d = h1_ref.shape[1]
    wtok = wtok_ref[...]
    h2 = h1_ref[...]
    for kk in range(TOP_K):
        yk = _unpack_bf16_pairs(yk_ref[:, kk * (d // 2):(kk + 1) * (d // 2)])
        h2 = h2 + yk * wtok[:, kk:kk + 1]
    n = _rms(h2, gple_ref[...]).astype(BF16)
    gate = _sigmoid(jnp.dot(n, wgate_ref[...], preferred_element_type=F32))
    pp = jnp.dot(p_ref[...].astype(BF16), wproj_ref[...], preferred_element_type=F32)
    o_ref[...] = _rms(h2 + gate * pp, gfin_ref[...])


def _final(h1, yk, wtok, p, wts, tm):
    n, d = h1.shape
    tok = lambda w: pl.BlockSpec((tm, w), lambda i: (i, 0))
    return pl.pallas_call(
        _final_kernel,
        grid=(n // tm,),
        in_specs=[tok(d), tok(TOP_K * d // 2), tok(LANE), tok(p.shape[1]), _full((1, d)), _full((d, d)),
                  _full((p.shape[1], d)), _full((1, d))],
        out_specs=tok(d),
        out_shape=jax.ShapeDtypeStruct((n, d), F32),
        compiler_params=_cparams(("arbitrary",)),
        name="final",
    )(h1, yk, wtok, p, wts['g_ple'], wts['w_ple_gate'], wts['w_ple_proj'], wts['g_final'])


def _rotate_half_cols(w):
    half = QK_ROPE // 2
    wh = w.reshape(w.shape[0], -1, 2, half)
    return jnp.concatenate([-wh[:, :, 1:2], wh[:, :, 0:1]], axis=2).reshape(w.shape)


def _prep_weights(g_mix, w_in, gdn_conv_w, gdn_a_log, gdn_dt_bias, gdn_norm, mla_g_q, mla_w_uq, mla_g_kv,
                  mla_w_uk, mla_w_uv, w_out, g_ffn, w_router, router_bias, w_exp_gate, w_exp_up, w_exp_down,
                  w_sh_gate, w_sh_up, w_sh_down, g_ple, w_ple_gate, w_ple_proj, g_final):
    d = w_in.shape[0]
    splits = np.cumsum([CONV_CH, GDN_V_W, GDN_HEADS, GDN_HEADS, Q_LORA, KV_LORA])
    w_qkv, w_z, w_b, w_a, w_cq, w_ckv, w_kr = jnp.split(w_in, [int(s) for s in splits], axis=1)
    ba_pad = jnp.zeros((d, LANE - 2 * GDN_HEADS), w_in.dtype)
    w1 = jnp.concatenate([w_qkv, w_z, w_cq, w_ckv, w_kr, _rotate_half_cols(w_kr), w_b, w_a, ba_pad], axis=1)
    assert w1.shape[1] == W1_WIDTH
    lane_pad = lambda v: jnp.pad(v.astype(F32), (GDN_HEADS, LANE - 2 * GDN_HEADS)).reshape(1, LANE)
    uq = mla_w_uq.reshape(Q_LORA, MLA_HEADS, QK_NOPE + QK_ROPE)
    uq_nope = uq[:, :, :QK_NOPE].reshape(Q_LORA, MLA_HEADS * QK_NOPE)
    uq_rope = uq[:, :, QK_NOPE:].reshape(Q_LORA, MLA_HEADS * QK_ROPE)
    wuq = jnp.concatenate([uq_nope, uq_rope, _rotate_half_cols(uq_rope)], axis=1)
    wr_t = w_router.T.astype(F32)
    wr_hi = wr_t.astype(BF16)
    row = lambda v: v.astype(F32).reshape(1, -1)
    return dict(
        g_mix=row(g_mix), w1=w1.astype(BF16), alog=lane_pad(gdn_a_log), dtb=lane_pad(gdn_dt_bias),
        g_q=row(mla_g_q), wuq=wuq.astype(BF16), wuk=jnp.transpose(mla_w_uk, (1, 2, 0)).astype(BF16),
        g_kv=row(mla_g_kv),
        conv_w=jnp.pad(gdn_conv_w.astype(F32), ((0, SUBLANE - CONV_W), (0, 0))), gdn_norm=row(gdn_norm),
        wuv=jnp.transpose(mla_w_uv, (1, 0, 2)).astype(BF16), w_out=w_out.astype(BF16), g_ffn=row(g_ffn),
        wr_hi=wr_hi, wr_lo=(wr_t - wr_hi.astype(F32)).astype(BF16),
        rbias=router_bias.astype(F32).reshape(N_EXPERTS, 1),
        ws_gu=jnp.concatenate([w_sh_gate, w_sh_up], axis=1).astype(BF16), ws_d=w_sh_down.astype(BF16),
        we_gu=jnp.concatenate([w_exp_gate, w_exp_up], axis=2).astype(BF16), we_d=w_exp_down.astype(BF16),
        g_ple=row(g_ple), w_ple_gate=w_ple_gate.astype(BF16), w_ple_proj=w_ple_proj.astype(BF16),
        g_final=row(g_final),
    )


def _rope_tables(pos, reps):
    half = QK_ROPE // 2
    inv = ROPE_THETA ** (-jnp.arange(half, dtype=F32) / half)
    ang = pos.astype(F32)[:, None] * inv[None, :]
    cos = jnp.concatenate([jnp.cos(ang)] * 2, axis=1)
    sin = jnp.concatenate([jnp.sin(ang)] * 2, axis=1)
    cos = jnp.tile(jnp.tile(cos, (1, MLA_HEADS)), (reps, 1))
    sin = jnp.tile(jnp.tile(sin, (1, MLA_HEADS)), (reps, 1))
    return cos, sin


def _tile(n, pref):
    t = min(n, pref)
    assert n % t == 0
    return t


def _layer(x, p, pos, gdn_state, conv_buf, attend, wts, chunk):
    b, t, d = x.shape
    n = b * t
    tm = _tile(n, 256)
    if t >= tm:
        assert t % tm == 0
        cos, sin = _rope_tables(pos, 1)
    else:
        assert tm % t == 0
        cos, sin = _rope_tables(pos, tm // t)
    x2 = x.reshape(n, d)
    qkv, z, gb, ckv, krope, kv, q = _in_proj(x2, wts, cos, sin, tm)

    conv0 = jnp.pad(conv_buf.astype(F32), ((0, 0), (SUBLANE - (CONV_W - 1), 0), (0, 0)))
    o_gdn, gdn_new = _gdn(qkv.reshape(b, t, CONV_CH), z.reshape(b, t, GDN_V_W), gb.reshape(b, t, LANE),
                          gdn_state.astype(F32), conv0, wts['conv_w'], wts['gdn_norm'], chunk)
    conv_new = qkv.reshape(b, t, CONV_CH)[:, t - (CONV_W - 1):, :]

    o_lat = attend(q.reshape(b, t, MLA_HEADS * QK_PAD), kv.reshape(b, t, QK_PAD))

    h1, xp, idx_t, rank_t, wtok, counts = _post_mix(x2, o_gdn.reshape(n, GDN_V_W),
                                                    o_lat.reshape(n, MLA_HEADS * KV_LORA), wts, tm)
    rows = n * TOP_K
    bm = _tile(rows, 256)
    cnt = counts[:, 0].astype(jnp.int32)
    vt, ve, seg, n_visits, starts = _visit_list(cnt, rows, bm)
    dest = (starts[idx_t] + rank_t).astype(jnp.int32).T.reshape(rows)
    xs = _scatter_rows(xp, dest, TOP_K, _tile(n, 256))
    ys = _experts(xs, vt, ve, seg, n_visits, wts['we_gu'], wts['we_d'], bm)
    yk = _gather_rows(ys, dest, _tile(rows, 512))
    y = _final(h1, yk.reshape(n, TOP_K * d // 2), wtok, p.reshape(n, -1), wts, tm)
    return y.reshape(b, t, d), ckv.reshape(b, t, KV_LORA), krope.reshape(b, t, QK_ROPE), gdn_new, conv_new


def kernel(x_prompt, x_sample, p_prompt, p_sample, cache_ckv, cache_krope, state_gdn, state_conv, page_table,
           g_mix, w_in, gdn_conv_w, gdn_a_log, gdn_dt_bias, gdn_norm, mla_g_q, mla_w_uq, mla_g_kv, mla_w_uk,
           mla_w_uv, w_out, g_ffn, w_router, router_bias, w_exp_gate, w_exp_up, w_exp_down, w_sh_gate,
           w_sh_up, w_sh_down, g_ple, w_ple_gate, w_ple_proj, g_final):
    depth = w_in.shape[0]
    assert depth == 1, "the final norm is fused into the (single) layer"
    bp, tp, _ = x_prompt.shape
    bs, ts, _ = x_sample.shape
    past = page_table.shape[1] * cache_ckv.shape[2]
    wts = _prep_weights(g_mix[0], w_in[0], gdn_conv_w[0], gdn_a_log[0], gdn_dt_bias[0], gdn_norm[0], mla_g_q[0],
                        mla_w_uq[0], mla_g_kv[0], mla_w_uk[0], mla_w_uv[0], w_out[0], g_ffn[0], w_router[0],
                        router_bias[0], w_exp_gate[0], w_exp_up[0], w_exp_down[0], w_sh_gate[0], w_sh_up[0],
                        w_sh_down[0], g_ple[0], w_ple_gate[0], w_ple_proj[0], g_final)

    gdn0 = jnp.zeros((bp, GDN_HEADS, GDN_DK, GDN_DV), state_gdn.dtype)
    conv0 = jnp.zeros((bp, CONV_W - 1, CONV_CH), state_conv.dtype)
    attend_p = functools.partial(_prompt_attn, tq=_tile(tp, 256), tk=_tile(tp, 512))
    yp, c1, k1, s1, v1 = _layer(x_prompt, p_prompt[0], jnp.arange(tp), gdn0, conv0, attend_p, wts,
                                chunk=_tile(tp, 64))

    def attend_s(q, kv):
        return _paged_attn(q, kv, cache_ckv[0], cache_krope[0], page_table,
                           pages_per_step=_tile(page_table.shape[1], 8))

    ys, c2, k2, s2, v2 = _layer(x_sample, p_sample[0], past + jnp.arange(ts), state_gdn[0], state_conv[0],
                                attend_s, wts, chunk=ts)
    st = lambda a, ref: a.astype(ref.dtype)[None]
    return (yp, ys, c1[None], k1[None], st(s1, state_gdn), st(v1, state_conv),
            c2[None], k2[None], st(s2, state_gdn), st(v2, state_conv))
```

```python
import functools

import jax
import jax.numpy as jnp
import numpy as np
from jax import lax
from jax.experimental import pallas as pl
from jax.experimental.pallas import tpu as pltpu

EPS = 1e-6
ROPE_THETA = 10000.0
GDN_HEADS = 4
GDN_DK = 128
GDN_DV = 128
CONV_W = 4
MLA_HEADS = 4
Q_LORA = 256
KV_LORA = 256
QK_NOPE = 128
QK_ROPE = 64
V_DIM = 128
N_EXPERTS = 64
TOP_K = 8
N_GROUPS = 8
TOPK_GROUPS = 4
GROUP_SIZE = N_EXPERTS // N_GROUPS
D_EXPERT = 256
D_SHARED = 256
ROUTED_SCALE = 2.5
MLA_SCALE = (QK_NOPE + QK_ROPE) ** -0.5

GDN_QK_W = GDN_HEADS * GDN_DK
GDN_V_W = GDN_HEADS * GDN_DV
CONV_CH = 2 * GDN_QK_W + GDN_V_W
LANE = 128
SUBLANE = 8
QK_PAD = KV_LORA + LANE
VMEM_LIMIT = 56 * 1024 * 1024

C_QKV = 0
C_Z = C_QKV + CONV_CH
C_CQ = C_Z + GDN_V_W
C_CKV = C_CQ + Q_LORA
C_KR = C_CKV + KV_LORA
C_KRR = C_KR + QK_ROPE
C_BA = C_KRR + QK_ROPE
W1_WIDTH = C_BA + LANE

F32 = jnp.float32
BF16 = jnp.bfloat16
HI = lax.Precision.HIGHEST
NT = (((1,), (1,)), ((), ()))
TN = (((0,), (0,)), ((), ()))


def _cparams(sem):
    return pltpu.CompilerParams(dimension_semantics=sem, vmem_limit_bytes=VMEM_LIMIT)


def _rms(x, g):
    return x * lax.rsqrt(jnp.mean(x * x, axis=-1, keepdims=True) + EPS) * g


def _sigmoid(x):
    return 1.0 / (1.0 + jnp.exp(-x))


def _silu(x):
    return x * _sigmoid(x)


def _full(shape):
    return pl.BlockSpec(shape, lambda *_: (0,) * len(shape))


def _in_proj_kernel(x_ref, gmix_ref, w1_ref, alog_ref, dtb_ref, gq_ref, wuq_ref, wuk_ref, gkv_ref,
                    cos_ref, sin_ref,
                    qkv_ref, z_ref, gb_ref, ckv_ref, krope_ref, kv_ref, q_ref):
    u = _rms(x_ref[...], gmix_ref[...]).astype(BF16)
    qkv_ref[...] = jnp.dot(u, w1_ref[:, C_QKV:C_Z], preferred_element_type=F32)
    z_ref[...] = jnp.dot(u, w1_ref[:, C_Z:C_CQ], preferred_element_type=F32)
    rest = jnp.dot(u, w1_ref[:, C_CQ:W1_WIDTH], preferred_element_type=F32)
    c_q = rest[:, 0:Q_LORA]
    c_kv = rest[:, C_CKV - C_CQ:C_KR - C_CQ]
    kr2 = rest[:, C_KR - C_CQ:C_BA - C_CQ]
    ba = rest[:, C_BA - C_CQ:]

    lane = lax.broadcasted_iota(jnp.int32, ba.shape, 1)
    beta = _sigmoid(ba)
    sp_in = ba + dtb_ref[...]
    softplus = jnp.maximum(sp_in, 0.0) + jnp.log(1.0 + jnp.exp(-jnp.abs(sp_in)))
    g = -jnp.exp(alog_ref[...]) * softplus
    gb_ref[...] = jnp.where(lane < GDN_HEADS, beta, g)

    cos = cos_ref[...]
    sin = sin_ref[...]
    ckv = _rms(c_kv, gkv_ref[...])
    ckv_ref[...] = ckv
    krope = kr2[:, :QK_ROPE] * cos[:, :QK_ROPE] + kr2[:, QK_ROPE:] * sin[:, :QK_ROPE]
    krope_ref[...] = krope
    zpad = jnp.zeros((ckv.shape[0], LANE - QK_ROPE), F32)
    kv_ref[...] = jnp.concatenate([ckv, krope, zpad], axis=1).astype(BF16)

    cqn = _rms(c_q, gq_ref[...]).astype(BF16)
    qh = jnp.dot(cqn, wuq_ref[...], preferred_element_type=F32)
    nope_w = MLA_HEADS * QK_NOPE
    rope_w = MLA_HEADS * QK_ROPE
    q_rope = qh[:, nope_w:nope_w + rope_w] * cos + qh[:, nope_w + rope_w:] * sin
    parts = []
    for h in range(MLA_HEADS):
        q_nope = qh[:, h * QK_NOPE:(h + 1) * QK_NOPE].astype(BF16)
        q_lat = jnp.dot(q_nope, wuk_ref[h], preferred_element_type=F32)
        parts += [q_lat, q_rope[:, h * QK_ROPE:(h + 1) * QK_ROPE], zpad]
    q_ref[...] = (jnp.concatenate(parts, axis=1) * MLA_SCALE).astype(BF16)


def _in_proj(x, wts, cos, sin, tm):
    n, d = x.shape
    period = cos.shape[0]
    nper = period // tm
    tok = lambda w: pl.BlockSpec((tm, w), lambda i: (i, 0))
    pos = pl.BlockSpec((tm, cos.shape[1]), lambda i: (i % nper, 0))
    out_shapes = (
        jax.ShapeDtypeStruct((n, CONV_CH), F32),
        jax.ShapeDtypeStruct((n, GDN_V_W), F32),
        jax.ShapeDtypeStruct((n, LANE), F32),
        jax.ShapeDtypeStruct((n, KV_LORA), F32),
        jax.ShapeDtypeStruct((n, QK_ROPE), F32),
        jax.ShapeDtypeStruct((n, QK_PAD), BF16),
        jax.ShapeDtypeStruct((n, MLA_HEADS * QK_PAD), BF16),
    )
    return pl.pallas_call(
        _in_proj_kernel,
        grid=(n // tm,),
        in_specs=[tok(d), _full((1, d)), _full(wts['w1'].shape), _full((1, LANE)), _full((1, LANE)),
                  _full((1, Q_LORA)), _full(wts['wuq'].shape), _full(wts['wuk'].shape), _full((1, KV_LORA)),
                  pos, pos],
        out_specs=[tok(CONV_CH), tok(GDN_V_W), tok(LANE), tok(KV_LORA), tok(QK_ROPE), tok(QK_PAD),
                   tok(MLA_HEADS * QK_PAD)],
        out_shape=out_shapes,
        compiler_params=_cparams(("arbitrary",)),
        name="in_proj",
    )(x, wts['g_mix'], wts['w1'], wts['alog'], wts['dtb'], wts['g_q'], wts['wuq'], wts['wuk'], wts['g_kv'],
      cos, sin)


def _gdn_kernel(qkv_ref, z_ref, gb_ref, s0_ref, conv0_ref, *rest, chunk):
    consts = rest[:-4]
    o_ref, sout_ref, s_scr, xbuf = rest[-4:]
    c = pl.program_id(1)

    @pl.when(c == 0)
    def _():
        s_scr[...] = s0_ref[...]
        xbuf[:, 0:SUBLANE, :] = conv0_ref[...]

    for bi in range(qkv_ref.shape[0]):
        _gdn_sequence_step(qkv_ref.at[bi], z_ref.at[bi], gb_ref.at[bi], *consts,
                           o_ref.at[bi], s_scr.at[bi], xbuf.at[bi], chunk=chunk)

    @pl.when(c == pl.num_programs(1) - 1)
    def _():
        sout_ref[...] = s_scr[...]


def _gdn_sequence_step(qkv_ref, z_ref, gb_ref, cw_ref, gn_ref,
                       cmat_ref, gsel_ref, ssel_ref, bsel_ref, place_ref, triu_ref, mask_ref, hsel_ref,
                       o_ref, s_scr, xbuf, *, chunk):
    pre = SUBLANE
    tail = CONV_W - 1

    xbuf[pre:pre + chunk, :] = qkv_ref[...]
    y = xbuf[pre - tail:pre - tail + chunk, :] * cw_ref[0:1, :]
    for j in range(1, CONV_W):
        y = y + xbuf[pre - tail + j:pre - tail + j + chunk, :] * cw_ref[j:j + 1, :]
    y = _silu(y)
    xbuf[pre - tail:pre, :] = xbuf[pre + chunk - tail:pre + chunk, :]

    stack = lambda a, off: jnp.concatenate(
        [a[:, off + h * GDN_DK:off + (h + 1) * GDN_DK] for h in range(GDN_HEADS)], axis=0)
    q = stack(y, 0)
    k = stack(y, GDN_QK_W)
    v = stack(y, 2 * GDN_QK_W)
    qn = q * lax.rsqrt(jnp.sum(q * q, axis=-1, keepdims=True) + EPS) * (GDN_DK ** -0.5)
    kn = k * lax.rsqrt(jnp.sum(k * k, axis=-1, keepdims=True) + EPS)

    gb = gb_ref[...]
    r = GDN_HEADS * chunk
    lane_of = lambda a, sel: jnp.sum(a * sel, axis=-1, keepdims=True)
    cums = jnp.dot(cmat_ref[...], gb, precision=HI, preferred_element_type=F32)
    gsel = gsel_ref[...]
    gcol = lane_of(cums[0:r], gsel)
    glast = lane_of(cums[r:2 * r], gsel)
    gstate = lane_of(cums[2 * r:], ssel_ref[...])
    beta = lane_of(jnp.concatenate([gb] * GDN_HEADS, axis=0), bsel_ref[...])
    spread = jnp.dot(gb, place_ref[...], precision=HI, preferred_element_type=F32)
    grow = jnp.sum(spread * triu_ref[...], axis=0, keepdims=True)

    incl = mask_ref[0]
    decay = jnp.exp(jnp.where(incl > 0.5, gcol - grow, -jnp.inf))
    kb = kn * beta
    knb = kn.astype(BF16)
    kk = lax.dot_general(kb.astype(BF16), knb, NT, preferred_element_type=F32)
    lower = kk * decay * mask_ref[1]
    inv = mask_ref[2] - lower * mask_ref[3]
    for lvl in range(4, mask_ref.shape[0]):
        ih = inv.astype(BF16)
        il = (inv - ih.astype(F32)).astype(BF16)
        blk = (lower * mask_ref[lvl]).astype(BF16)
        t2 = jnp.dot(jnp.concatenate([ih, il], axis=0), blk, preferred_element_type=F32)
        t = t2[:r] + t2[r:]
        th = t.astype(BF16)
        tl = (t - th.astype(F32)).astype(BF16)
        c2 = jnp.dot(jnp.concatenate([th, tl], axis=0), ih, preferred_element_type=F32)
        inv = inv - (c2[:r] + c2[r:] + jnp.dot(th, il, preferred_element_type=F32))
    egc = jnp.exp(gcol)
    ih = inv.astype(BF16)
    il = (inv - ih.astype(F32)).astype(BF16)
    rhs = jnp.concatenate([v * beta, kb * egc], axis=1)
    rh = rhs.astype(BF16)
    rl = (rhs - rh.astype(F32)).astype(BF16)
    uw2 = jnp.dot(jnp.concatenate([ih, il], axis=0), rh, preferred_element_type=F32)
    uw = uw2[:r] + uw2[r:] + jnp.dot(ih, rl, preferred_element_type=F32)
    u = uw[:, :GDN_DV]
    w = uw[:, GDN_DV:]
    qk = lax.dot_general(qn.astype(BF16), knb, NT, preferred_element_type=F32) * decay

    hsel = hsel_ref[...]
    wide = lambda a: (jnp.concatenate([a] * GDN_HEADS, axis=1) * hsel).astype(BF16)
    st = s_scr[...]
    stb = st.astype(BF16)
    ws_qs = jnp.dot(jnp.concatenate([wide(w), wide(qn * egc)], axis=0), stb, preferred_element_type=F32)
    v_new = u - ws_qs[:r]
    vnb = v_new.astype(BF16)
    o = ws_qs[r:] + jnp.dot(qk.astype(BF16), vnb, preferred_element_type=F32)
    kdec = wide(kn * jnp.exp(glast - gcol))
    s_scr[...] = st * jnp.exp(gstate) + lax.dot_general(kdec, vnb, TN, preferred_element_type=F32)

    out = _rms(o, gn_ref[...]) * _silu(stack(z_ref[...], 0))
    for h in range(GDN_HEADS):
        o_ref[:, h * GDN_DV:(h + 1) * GDN_DV] = out[h * chunk:(h + 1) * chunk, :].astype(o_ref.dtype)


def _gdn_constants(chunk):
    r = GDN_HEADS * chunk
    ri = np.arange(r)
    head, tok = ri // chunk, ri % chunk
    ti = np.arange(chunk)
    lane = np.arange(LANE)
    srow = np.arange(GDN_HEADS * GDN_DK)
    cmat = np.concatenate([(ti[None, :] <= tok[:, None]), np.ones((r, chunk), bool),
                           np.ones((GDN_HEADS * GDN_DK, chunk), bool)], axis=0)
    gsel = lane[None, :] == GDN_HEADS + head[:, None]
    ssel = lane[None, :] == GDN_HEADS + (srow // GDN_DK)[:, None]
    bsel = lane[None, :] == head[:, None]
    place = lane[:, None] == GDN_HEADS + head[None, :]
    triu = ti[:, None] <= tok[None, :]
    same = head[:, None] == head[None, :]
    masks = [same & (tok[:, None] >= tok[None, :]), same & (tok[:, None] > tok[None, :]),
             ri[:, None] == ri[None, :]]
    s = 1
    while s < chunk:
        masks.append(same & ((tok[:, None] // s) % 2 == 1) & ((tok[None, :] // s) == (tok[:, None] // s) - 1))
        s *= 2
    hsel = (np.arange(GDN_HEADS * GDN_DK)[None, :] // GDN_DK) == head[:, None]
    f = lambda a: jnp.asarray(np.asarray(a, np.float32))
    return f(cmat), f(gsel), f(ssel), f(bsel), f(place), f(triu), f(np.stack(masks)), f(hsel)


def _gdn(qkv, z, gb, s0, conv0, cw, gn, chunk):
    b, t, _ = qkv.shape
    assert t % chunk == 0 and chunk % SUBLANE == 0
    nc = t // chunk
    nb = 2 if b % 2 == 0 else 1
    consts = _gdn_constants(chunk)
    tokb = lambda w: pl.BlockSpec((nb, chunk, w), lambda i, c: (i, c, 0))
    state = pl.BlockSpec((nb, GDN_HEADS * GDN_DK, GDN_DV), lambda i, c: (i, 0, 0))
    o, s_new = pl.pallas_call(
        functools.partial(_gdn_kernel, chunk=chunk),
        grid=(b // nb, nc),
        in_specs=[tokb(CONV_CH), tokb(GDN_V_W), tokb(LANE), state,
                  pl.BlockSpec((nb, SUBLANE, CONV_CH), lambda i, c: (i, 0, 0)),
                  _full((SUBLANE, CONV_CH)), _full((1, GDN_DV))] + [_full(a.shape) for a in consts],
        out_specs=[tokb(GDN_V_W), state],
        out_shape=(jax.ShapeDtypeStruct((b, t, GDN_V_W), BF16),
                   jax.ShapeDtypeStruct((b, GDN_HEADS * GDN_DK, GDN_DV), F32)),
        scratch_shapes=[pltpu.VMEM((nb, GDN_HEADS * GDN_DK, GDN_DV), F32),
                        pltpu.VMEM((nb, SUBLANE + chunk, CONV_CH), F32)],
        compiler_params=_cparams(("arbitrary", "arbitrary")),
        name="gdn",
    )(qkv, z, gb, s0.reshape(b, GDN_HEADS * GDN_DK, GDN_DV), conv0, cw, gn, *consts)
    return o, s_new.reshape(b, GDN_HEADS, GDN_DK, GDN_DV)


def _stack_heads(q):
    return jnp.concatenate([q[:, h * QK_PAD:(h + 1) * QK_PAD] for h in range(MLA_HEADS)], axis=0)


def _online_softmax_step(s, vals, m_scr, l_scr, acc_scr):
    m_prev = m_scr[...]
    m_new = jnp.maximum(m_prev, jnp.max(s, axis=-1, keepdims=True))
    alpha = jnp.exp(m_prev - m_new)
    p = jnp.exp(s - m_new)
    l_scr[...] = alpha * l_scr[...] + jnp.sum(p, axis=-1, keepdims=True)
    pb = p.astype(BF16)
    if isinstance(vals, (list, tuple)):
        w = s.shape[1] // len(vals)
        pv = sum(jnp.dot(pb[:, i * w:(i + 1) * w], vi, preferred_element_type=F32) for i, vi in enumerate(vals))
    else:
        pv = jnp.dot(pb, vals, preferred_element_type=F32)
    acc_scr[...] = alpha * acc_scr[...] + pv
    m_scr[...] = m_new


def _unstack_store(o_ref, acc_scr, l_scr, rows):
    o = acc_scr[...] / l_scr[...]
    for h in range(MLA_HEADS):
        o_ref[:, h * KV_LORA:(h + 1) * KV_LORA] = o[h * rows:(h + 1) * rows, :].astype(o_ref.dtype)


def _prompt_attn_kernel(q_ref, kv_ref, o_ref, qs_scr, m_scr, l_scr, acc_scr, *, tq, tk):
    i = pl.program_id(1)
    j = pl.program_id(2)
    nj = pl.num_programs(2)

    @pl.when(j == 0)
    def _():
        qs_scr[...] = _stack_heads(q_ref[...])
        m_scr[...] = jnp.full(m_scr.shape, -jnp.inf, F32)
        l_scr[...] = jnp.zeros(l_scr.shape, F32)
        acc_scr[...] = jnp.zeros(acc_scr.shape, F32)

    @pl.when(j * tk <= i * tq + tq - 1)
    def _():
        kv = kv_ref[...]
        s = lax.dot_general(qs_scr[...], kv, NT, preferred_element_type=F32)
        qpos = i * tq + lax.broadcasted_iota(jnp.int32, s.shape, 0) % tq
        kpos = j * tk + lax.broadcasted_iota(jnp.int32, s.shape, 1)
        s = jnp.where(kpos <= qpos, s, -jnp.inf)
        _online_softmax_step(s, kv[:, :KV_LORA], m_scr, l_scr, acc_scr)

    @pl.when(j == nj - 1)
    def _():
        _unstack_store(o_ref, acc_scr, l_scr, tq)


def _prompt_attn(q, kv, tq, tk):
    b, s, _ = q.shape
    rows = MLA_HEADS * tq
    last = lambda i: (i * tq + tq - 1) // tk
    return pl.pallas_call(
        functools.partial(_prompt_attn_kernel, tq=tq, tk=tk),
        grid=(b, s // tq, s // tk),
        in_specs=[pl.BlockSpec((None, tq, MLA_HEADS * QK_PAD), lambda bb, i, j: (bb, i, 0)),
                  pl.BlockSpec((None, tk, QK_PAD), lambda bb, i, j: (bb, jnp.minimum(j, last(i)), 0))],
        out_specs=pl.BlockSpec((None, tq, MLA_HEADS * KV_LORA), lambda bb, i, j: (bb, i, 0)),
        out_shape=jax.ShapeDtypeStruct((b, s, MLA_HEADS * KV_LORA), BF16),
        scratch_shapes=[pltpu.VMEM((rows, QK_PAD), BF16), pltpu.VMEM((rows, 1), F32),
                        pltpu.VMEM((rows, 1), F32), pltpu.VMEM((rows, KV_LORA), F32)],
        compiler_params=_cparams(("arbitrary", "arbitrary", "arbitrary")),
        name="prompt_attn",
    )(q, kv)


def _paged_attn_kernel(pt_ref, q_ref, kvnew_ref, *rest, pages_per_step, t):
    ckv_refs = rest[:pages_per_step]
    kr_refs = rest[pages_per_step:2 * pages_per_step]
    o_ref, qs_scr, m_scr, l_scr, acc_scr = rest[2 * pages_per_step:]
    g = pl.program_id(1)
    ng = pl.num_programs(1)

    @pl.when(g == 0)
    def _():
        qs_scr[...] = _stack_heads(q_ref[...])
        m_scr[...] = jnp.full(m_scr.shape, -jnp.inf, F32)
        l_scr[...] = jnp.zeros(l_scr.shape, F32)
        acc_scr[...] = jnp.zeros(acc_scr.shape, F32)

    qs = qs_scr[...]
    q_lat = qs[:, :KV_LORA]
    q_rope = qs[:, KV_LORA:KV_LORA + QK_ROPE]
    pages, scores = [], []
    for p in range(pages_per_step):
        ckv = ckv_refs[p][...].astype(BF16)
        kr_t = kr_refs[p][...].astype(BF16)
        pages.append(ckv)
        scores.append(lax.dot_general(q_lat, ckv, NT, preferred_element_type=F32)
                      + jnp.dot(q_rope, kr_t, preferred_element_type=F32))
    _online_softmax_step(jnp.concatenate(scores, axis=1), pages, m_scr, l_scr, acc_scr)

    @pl.when(g == ng - 1)
    def _():
        kvn = kvnew_ref[...]
        s = lax.dot_general(qs, kvn, NT, preferred_element_type=F32)
        qpos = lax.broadcasted_iota(jnp.int32, s.shape, 0) % t
        kpos = lax.broadcasted_iota(jnp.int32, s.shape, 1)
        s = jnp.where(kpos <= qpos, s, -jnp.inf)
        _online_softmax_step(s, kvn[:, :KV_LORA], m_scr, l_scr, acc_scr)
        _unstack_store(o_ref, acc_scr, l_scr, t)


def _paged_attn(q, kvnew, cache_ckv, cache_krope_t, page_table, pages_per_step):
    b, t, _ = q.shape
    n_pages = page_table.shape[1]
    page = cache_ckv.shape[1]
    assert n_pages % pages_per_step == 0
    rows = MLA_HEADS * t

    def page_spec(shape, p):
        return pl.BlockSpec((None,) + shape,
                            lambda bb, g, pt: (pt[bb * n_pages + g * pages_per_step + p], 0, 0))

    grid_spec = pltpu.PrefetchScalarGridSpec(
        num_scalar_prefetch=1,
        grid=(b, n_pages // pages_per_step),
        in_specs=([pl.BlockSpec((None, t, MLA_HEADS * QK_PAD), lambda bb, g, pt: (bb, 0, 0)),
                   pl.BlockSpec((None, t, QK_PAD), lambda bb, g, pt: (bb, 0, 0))]
                  + [page_spec((page, KV_LORA), p) for p in range(pages_per_step)]
                  + [page_spec((QK_ROPE, page), p) for p in range(pages_per_step)]),
        out_specs=pl.BlockSpec((None, t, MLA_HEADS * KV_LORA), lambda bb, g, pt: (bb, 0, 0)),
        scratch_shapes=[pltpu.VMEM((rows, QK_PAD), BF16), pltpu.VMEM((rows, 1), F32),
                        pltpu.VMEM((rows, 1), F32), pltpu.VMEM((rows, KV_LORA), F32)],
    )
    return pl.pallas_call(
        functools.partial(_paged_attn_kernel, pages_per_step=pages_per_step, t=t),
        grid_spec=grid_spec,
        out_shape=jax.ShapeDtypeStruct((b, t, MLA_HEADS * KV_LORA), BF16),
        compiler_params=_cparams(("arbitrary", "arbitrary")),
        name="paged_attn",
    )(page_table.reshape(-1), q, kvnew, *([cache_ckv] * pages_per_step), *([cache_krope_t] * pages_per_step))


def _pack_bf16_pairs(x):
    w = x.shape[1] // 2
    xb = x.astype(BF16).astype(F32)
    lo = lax.shift_right_logical(lax.bitcast_convert_type(xb[:, :w], jnp.uint32), jnp.uint32(16))
    hi = lax.bitcast_convert_type(xb[:, w:], jnp.uint32) & jnp.uint32(0xFFFF0000)
    return lo | hi


def _unpack_bf16_pairs(p):
    lo = lax.bitcast_convert_type(lax.shift_left(p, jnp.uint32(16)), F32)
    hi = lax.bitcast_convert_type(p & jnp.uint32(0xFFFF0000), F32)
    return jnp.concatenate([lo, hi], axis=1)


def _post_mix_kernel(h_ref, ogdn_ref, olat_ref, wuv_ref, wout_ref, gffn_ref, wrh_ref, wrl_ref, rbias_ref,
                     wsgu_ref, wsd_ref,
                     h1_ref, xp_ref, idx_ref, rank_ref, wtok_ref, cnt_ref, carry):
    i = pl.program_id(0)
    tm = h_ref.shape[0]

    @pl.when(i == 0)
    def _():
        carry[...] = jnp.zeros(carry.shape, F32)

    parts = [ogdn_ref[...]]
    for hh in range(MLA_HEADS):
        o_h = jnp.dot(olat_ref[:, hh * KV_LORA:(hh + 1) * KV_LORA], wuv_ref[hh], preferred_element_type=F32)
        parts.append(o_h.astype(BF16))
    mix = jnp.concatenate(parts, axis=1)
    h1 = h_ref[...] + jnp.dot(mix, wout_ref[...], preferred_element_type=F32)
    u2 = _rms(h1, gffn_ref[...])
    u2b = u2.astype(BF16)
    xp_ref[...] = _pack_bf16_pairs(u2)

    u2l = (u2 - u2b.astype(F32)).astype(BF16)
    logits = (lax.dot_general(wrh_ref[...], u2b, NT, preferred_element_type=F32)
              + lax.dot_general(wrh_ref[...], u2l, NT, preferred_element_type=F32)
              + lax.dot_general(wrl_ref[...], u2b, NT, preferred_element_type=F32))
    scores = _sigmoid(logits)
    biased = scores + rbias_ref[...]
    neg = -jnp.inf

    sub = lax.broadcasted_iota(jnp.int32, (GROUP_SIZE, tm), 0)
    gscore = []
    for gi in range(N_GROUPS):
        vg = biased[gi * GROUP_SIZE:(gi + 1) * GROUP_SIZE, :]
        m1 = jnp.max(vg, axis=0, keepdims=True)
        first = jnp.min(jnp.where(vg == m1, sub, GROUP_SIZE), axis=0, keepdims=True)
        m2 = jnp.max(jnp.where(sub == first, neg, vg), axis=0, keepdims=True)
        gscore.append(m1 + m2)
    gs = jnp.concatenate(gscore, axis=0)
    gid = lax.broadcasted_iota(jnp.int32, gs.shape, 0)
    gsel = jnp.zeros(gs.shape, F32)
    for _ in range(TOPK_GROUPS):
        m = jnp.max(gs, axis=0, keepdims=True)
        pick = jnp.min(jnp.where(gs == m, gid, N_GROUPS), axis=0, keepdims=True)
        hit = gid == pick
        gsel = jnp.where(hit, 1.0, gsel)
        gs = jnp.where(hit, neg, gs)
    emask = jnp.concatenate(
        [jnp.broadcast_to(gsel[gi:gi + 1, :], (GROUP_SIZE, tm)) for gi in range(N_GROUPS)], axis=0)
    cand = jnp.where(emask > 0.5, biased, neg)

    eid = lax.broadcasted_iota(jnp.int32, cand.shape, 0)
    chosen = jnp.zeros(cand.shape, F32)
    idx_rows, w_rows, hits = [], [], []
    for _ in range(TOP_K):
        m = jnp.max(cand, axis=0, keepdims=True)
        pick = jnp.min(jnp.where(cand == m, eid, N_EXPERTS), axis=0, keepdims=True)
        hit = eid == pick
        idx_rows.append(pick)
        w_rows.append(jnp.sum(jnp.where(hit, scores, 0.0), axis=0, keepdims=True))
        hits.append(hit)
        chosen = chosen + hit.astype(F32)
        cand = jnp.where(hit, neg, cand)
    wsel = jnp.concatenate(w_rows, axis=0)
    wsel = wsel / (jnp.sum(wsel, axis=0, keepdims=True) + 1e-20) * ROUTED_SCALE
    idx_ref[...] = jnp.concatenate(idx_rows, axis=0)

    ti = lax.broadcasted_iota(jnp.int32, (tm, tm), 0)
    tj = lax.broadcasted_iota(jnp.int32, (tm, tm), 1)
    before = (ti < tj).astype(BF16)
    prefix = jnp.dot(chosen.astype(BF16), before, preferred_element_type=F32) + carry[:, 0:1]
    rank_rows = [jnp.sum(jnp.where(hit, prefix, 0.0), axis=0, keepdims=True) for hit in hits]
    rank_ref[...] = jnp.concatenate(rank_rows, axis=0).astype(jnp.int32)
    carry[...] = carry[...] + jnp.sum(chosen, axis=1, keepdims=True)
    cnt_ref[...] = carry[...]

    wpad = jnp.concatenate([wsel, jnp.zeros((LANE - TOP_K, tm), F32)], axis=0)
    wtok_ref[...] = wpad.T

    gu = jnp.dot(u2b, wsgu_ref[...], preferred_element_type=F32)
    hs = (_silu(gu[:, :D_SHARED]) * gu[:, D_SHARED:]).astype(BF16)
    h1_ref[...] = h1 + jnp.dot(hs, wsd_ref[...], preferred_element_type=F32)


def _post_mix(h, o_gdn, o_lat, wts, tm):
    n, d = h.shape
    tok = lambda w: pl.BlockSpec((tm, w), lambda i: (i, 0))
    tr = lambda r: pl.BlockSpec((r, tm), lambda i: (0, i))
    return pl.pallas_call(
        _post_mix_kernel,
        grid=(n // tm,),
        in_specs=[tok(d), tok(GDN_V_W), tok(MLA_HEADS * KV_LORA), _full(wts['wuv'].shape), _full((d, d)),
                  _full((1, d)), _full((N_EXPERTS, d)), _full((N_EXPERTS, d)), _full((N_EXPERTS, 1)),
                  _full((d, 2 * D_SHARED)), _full((D_SHARED, d))],
        out_specs=[tok(d), tok(d // 2), tr(TOP_K), tr(TOP_K), tok(LANE), _full((N_EXPERTS, LANE))],
        out_shape=(jax.ShapeDtypeStruct((n, d), F32),
                   jax.ShapeDtypeStruct((n, d // 2), jnp.uint32),
                   jax.ShapeDtypeStruct((TOP_K, n), jnp.int32),
                   jax.ShapeDtypeStruct((TOP_K, n), jnp.int32),
                   jax.ShapeDtypeStruct((n, LANE), F32),
                   jax.ShapeDtypeStruct((N_EXPERTS, LANE), F32)),
        scratch_shapes=[pltpu.VMEM((N_EXPERTS, LANE), F32)],
        compiler_params=_cparams(("arbitrary",)),
        name="post_mix",
    )(h, o_gdn, o_lat, wts['wuv'], wts['w_out'], wts['g_ffn'], wts['wr_hi'], wts['wr_lo'], wts['rbias'],
      wts['ws_gu'], wts['ws_d'])


def _dest_kernel(idx_ref, rank_ref, starts_ref, o_ref):
    eid = lax.broadcasted_iota(jnp.int32, (N_EXPERTS, idx_ref.shape[1]), 0)
    starts = starts_ref[...]
    rows = []
    for kk in range(TOP_K):
        hit = eid == idx_ref[kk:kk + 1, :]
        rows.append(jnp.sum(jnp.where(hit, starts, 0.0), axis=0, keepdims=True))
    o_ref[...] = jnp.concatenate(rows, axis=0).astype(jnp.int32) + rank_ref[...]


def _dest_rows(idx_t, rank_t, starts, tn):
    k, n = idx_t.shape
    assert n * k < 2 ** 24
    tr = pl.BlockSpec((k, tn), lambda i: (0, i))
    return pl.pallas_call(
        _dest_kernel,
        grid=(n // tn,),
        in_specs=[tr, tr, _full((N_EXPERTS, 1))],
        out_specs=tr,
        out_shape=jax.ShapeDtypeStruct((k, n), jnp.int32),
        compiler_params=_cparams(("arbitrary",)),
        name="dest_rows",
    )(idx_t, rank_t, starts.astype(F32).reshape(N_EXPERTS, 1))


def _scatter_kernel(dest_ref, x_ref, o_hbm, sem):
    fan, nrow = dest_ref.shape

    def start(r, carry):
        for kk in range(fan):
            pltpu.make_async_copy(x_ref.at[pl.ds(r, 1)], o_hbm.at[pl.ds(dest_ref[kk, r], 1)],
                                  sem).start(priority=kk % 2)
        return carry

    lax.fori_loop(0, nrow, start, 0)
    whole = o_hbm.at[pl.ds(0, fan * nrow)]
    pltpu.make_async_copy(whole, whole, sem).wait()


def _scatter_rows(x, dest_t, ts):
    n, w = x.shape
    fan = dest_t.shape[0]
    return pl.pallas_call(
        _scatter_kernel,
        grid=(n // ts,),
        in_specs=[pl.BlockSpec((fan, ts), lambda i: (0, i), memory_space=pltpu.SMEM),
                  pl.BlockSpec((ts, w), lambda i: (i, 0))],
        out_specs=pl.BlockSpec(memory_space=pl.ANY),
        out_shape=jax.ShapeDtypeStruct((n * fan, w), x.dtype),
        scratch_shapes=[pltpu.SemaphoreType.DMA],
        compiler_params=pltpu.CompilerParams(dimension_semantics=("arbitrary",), has_side_effects=True),
        name="scatter_rows",
    )(dest_t, x)


def _experts_kernel(vt_ref, ve_ref, seg_ref, nv_ref, x_ref, wgu_ref, wd_ref, o_ref, *, bm):
    v = pl.program_id(0)
    tile = vt_ref[v]
    e = ve_ref[v]
    prev_tile = vt_ref[jnp.maximum(v - 1, 0)]
    first = (v == 0) | (tile != prev_tile)

    @pl.when(v < nv_ref[0])
    def _():
        x = _unpack_bf16_pairs(x_ref[...]).astype(BF16)
        gu = jnp.dot(x, wgu_ref[...], preferred_element_type=F32)
        hid = (_silu(gu[:, :D_EXPERT]) * gu[:, D_EXPERT:]).astype(BF16)
        y = _pack_bf16_pairs(jnp.dot(hid, wd_ref[...], preferred_element_type=F32))
        rowid = tile * bm + lax.broadcasted_iota(jnp.int32, y.shape, 0)
        mine = (rowid >= seg_ref[e]) & (rowid < seg_ref[e + 1])

        @pl.when(first)
        def _():
            o_ref[...] = jnp.where(mine, y, jnp.uint32(0))

        @pl.when(jnp.logical_not(first))
        def _():
            o_ref[...] = jnp.where(mine, y, o_ref[...])


def _experts(xs, visit_tile, visit_expert, seg, n_visits, wgu, wd, bm):
    rows, w = xs.shape
    nv_max = visit_tile.shape[0]
    d = wd.shape[2]
    grid_spec = pltpu.PrefetchScalarGridSpec(
        num_scalar_prefetch=4,
        grid=(nv_max,),
        in_specs=[pl.BlockSpec((bm, w), lambda v, vt, ve, sg, nv: (vt[v], 0)),
                  pl.BlockSpec((None, d, 2 * D_EXPERT), lambda v, vt, ve, sg, nv: (ve[v], 0, 0)),
                  pl.BlockSpec((None, D_EXPERT, d), lambda v, vt, ve, sg, nv: (ve[v], 0, 0))],
        out_specs=pl.BlockSpec((bm, w), lambda v, vt, ve, sg, nv: (vt[v], 0)),
    )
    return pl.pallas_call(
        functools.partial(_experts_kernel, bm=bm),
        grid_spec=grid_spec,
        out_shape=jax.ShapeDtypeStruct((rows, w), jnp.uint32),
        compiler_params=_cparams(("arbitrary",)),
        name="experts",
    )(visit_tile, visit_expert, seg, n_visits, xs, wgu, wd)


def _visit_list(counts, rows, bm):
    ntiles = rows // bm
    nv_max = ntiles + N_EXPERTS - 1
    ends = jnp.cumsum(counts)
    starts = ends - counts
    first_tile = starts // bm
    ntile_e = jnp.where(counts > 0, (ends - 1) // bm - first_tile + 1, 0)
    vend = jnp.cumsum(ntile_e)
    vstart = vend - ntile_e
    n_visits = vend[-1]
    v = jnp.arange(nv_max, dtype=jnp.int32)
    ve = jnp.minimum(jnp.sum(vend[None, :] <= v[:, None], axis=1), N_EXPERTS - 1).astype(jnp.int32)
    vt = (first_tile[ve] + v - vstart[ve]).astype(jnp.int32)
    last = jnp.maximum(n_visits - 1, 0)
    ve = jnp.where(v < n_visits, ve, ve[last])
    vt = jnp.where(v < n_visits, vt, vt[last])
    seg = jnp.concatenate([starts, ends[-1:]]).astype(jnp.int32)
    return vt, ve, seg, n_visits.reshape(1).astype(jnp.int32), starts


def _final_kernel(dcur_ref, dnext_ref, h1_ref, wtok_ref, p_ref, gple_ref, wgate_ref, wproj_ref, gfin_ref,
                  ys_hbm, o_ref, ybuf, sem):
    i = pl.program_id(0)
    nsteps = pl.num_programs(0)
    tm = h1_ref.shape[0]
    slot = i % 2

    def fetch(dest_ref, into):
        def body(r, carry):
            for kk in range(TOP_K):
                pltpu.make_async_copy(ys_hbm.at[pl.ds(dest_ref[kk, r], 1)],
                                      ybuf.at[into, pl.ds(kk * tm + r, 1)],
                                      sem.at[into]).start(priority=kk % 2)
            return carry
        lax.fori_loop(0, tm, body, 0)

    @pl.when(i == 0)
    def _():
        fetch(dcur_ref, 0)

    @pl.when(i + 1 < nsteps)
    def _():
        fetch(dnext_ref, 1 - slot)

    pltpu.make_async_copy(ys_hbm.at[pl.ds(0, TOP_K * tm)], ybuf.at[slot], sem.at[slot]).wait()

    wtok = wtok_ref[...]
    h2 = h1_ref[...]
    for kk in range(TOP_K):
        yk = _unpack_bf16_pairs(ybuf[slot, kk * tm:(kk + 1) * tm, :])
        h2 = h2 + yk * wtok[:, kk:kk + 1]
    n = _rms(h2, gple_ref[...]).astype(BF16)
    gate = _sigmoid(jnp.dot(n, wgate_ref[...], preferred_element_type=F32))
    pp = jnp.dot(p_ref[...].astype(BF16), wproj_ref[...], preferred_element_type=F32)
    o_ref[...] = _rms(h2 + gate * pp, gfin_ref[...])


def _final(h1, ys, dest_t, wtok, p, wts, tm):
    n, d = h1.shape
    nsteps = n // tm
    assert ys.shape[0] >= TOP_K * tm
    tok = lambda w: pl.BlockSpec((tm, w), lambda i: (i, 0))
    dspec = lambda f: pl.BlockSpec((TOP_K, tm), f, memory_space=pltpu.SMEM)
    return pl.pallas_call(
        _final_kernel,
        grid=(nsteps,),
        in_specs=[dspec(lambda i: (0, i)), dspec(lambda i: (0, jnp.minimum(i + 1, nsteps - 1))),
                  tok(d), tok(LANE), tok(p.shape[1]), _full((1, d)), _full((d, d)),
                  _full((p.shape[1], d)), _full((1, d)), pl.BlockSpec(memory_space=pl.ANY)],
        out_specs=tok(d),
        out_shape=jax.ShapeDtypeStruct((n, d), F32),
        scratch_shapes=[pltpu.VMEM((2, TOP_K * tm, ys.shape[1]), ys.dtype), pltpu.SemaphoreType.DMA((2,))],
        compiler_params=_cparams(("arbitrary",)),
        name="final",
    )(dest_t, dest_t, h1, wtok, p, wts['g_ple'], wts['w_ple_gate'], wts['w_ple_proj'], wts['g_final'], ys)


def _rotate_half_cols(w):
    half = QK_ROPE // 2
    wh = w.reshape(w.shape[0], -1, 2, half)
    return jnp.concatenate([-wh[:, :, 1:2], wh[:, :, 0:1]], axis=2).reshape(w.shape)


def _prep_weights(g_mix, w_in, gdn_conv_w, gdn_a_log, gdn_dt_bias, gdn_norm, mla_g_q, mla_w_uq, mla_g_kv,
                  mla_w_uk, mla_w_uv, w_out, g_ffn, w_router, router_bias, w_exp_gate, w_exp_up, w_exp_down,
                  w_sh_gate, w_sh_up, w_sh_down, g_ple, w_ple_gate, w_ple_proj, g_final):
    d = w_in.shape[0]
    splits = np.cumsum([CONV_CH, GDN_V_W, GDN_HEADS, GDN_HEADS, Q_LORA, KV_LORA])
    w_qkv, w_z, w_b, w_a, w_cq, w_ckv, w_kr = jnp.split(w_in, [int(s) for s in splits], axis=1)
    ba_pad = jnp.zeros((d, LANE - 2 * GDN_HEADS), w_in.dtype)
    w1 = jnp.concatenate([w_qkv, w_z, w_cq, w_ckv, w_kr, _rotate_half_cols(w_kr), w_b, w_a, ba_pad], axis=1)
    assert w1.shape[1] == W1_WIDTH
    lane_pad = lambda v: jnp.pad(v.astype(F32), (GDN_HEADS, LANE - 2 * GDN_HEADS)).reshape(1, LANE)
    uq = mla_w_uq.reshape(Q_LORA, MLA_HEADS, QK_NOPE + QK_ROPE)
    uq_nope = uq[:, :, :QK_NOPE].reshape(Q_LORA, MLA_HEADS * QK_NOPE)
    uq_rope = uq[:, :, QK_NOPE:].reshape(Q_LORA, MLA_HEADS * QK_ROPE)
    wuq = jnp.concatenate([uq_nope, uq_rope, _rotate_half_cols(uq_rope)], axis=1)
    wr_t = w_router.T.astype(F32)
    wr_hi = wr_t.astype(BF16)
    row = lambda v: v.astype(F32).reshape(1, -1)
    return dict(
        g_mix=row(g_mix), w1=w1.astype(BF16), alog=lane_pad(gdn_a_log), dtb=lane_pad(gdn_dt_bias),
        g_q=row(mla_g_q), wuq=wuq.astype(BF16), wuk=jnp.transpose(mla_w_uk, (1, 2, 0)).astype(BF16),
        g_kv=row(mla_g_kv),
        conv_w=jnp.pad(gdn_conv_w.astype(F32), ((0, SUBLANE - CONV_W), (0, 0))), gdn_norm=row(gdn_norm),
        wuv=jnp.transpose(mla_w_uv, (1, 0, 2)).astype(BF16), w_out=w_out.astype(BF16), g_ffn=row(g_ffn),
        wr_hi=wr_hi, wr_lo=(wr_t - wr_hi.astype(F32)).astype(BF16),
        rbias=router_bias.astype(F32).reshape(N_EXPERTS, 1),
        ws_gu=jnp.concatenate([w_sh_gate, w_sh_up], axis=1).astype(BF16), ws_d=w_sh_down.astype(BF16),
        we_gu=jnp.concatenate([w_exp_gate, w_exp_up], axis=2).astype(BF16), we_d=w_exp_down.astype(BF16),
        g_ple=row(g_ple), w_ple_gate=w_ple_gate.astype(BF16), w_ple_proj=w_ple_proj.astype(BF16),
        g_final=row(g_final),
    )


def _rope_tables(pos, reps):
    half = QK_ROPE // 2
    inv = ROPE_THETA ** (-jnp.arange(half, dtype=F32) / half)
    ang = pos.astype(F32)[:, None] * inv[None, :]
    cos = jnp.concatenate([jnp.cos(ang)] * 2, axis=1)
    sin = jnp.concatenate([jnp.sin(ang)] * 2, axis=1)
    cos = jnp.tile(jnp.tile(cos, (1, MLA_HEADS)), (reps, 1))
    sin = jnp.tile(jnp.tile(sin, (1, MLA_HEADS)), (reps, 1))
    return cos, sin


def _tile(n, pref):
    t = min(n, pref)
    assert n % t == 0
    return t


def _layer(x, p, pos, gdn_state, conv_buf, attend, wts, chunk):
    b, t, d = x.shape
    n = b * t
    tm = _tile(n, 256)
    if t >= tm:
        assert t % tm == 0
        cos, sin = _rope_tables(pos, 1)
    else:
        assert tm % t == 0
        cos, sin = _rope_tables(pos, tm // t)
    x2 = x.reshape(n, d)
    qkv, z, gb, ckv, krope, kv, q = _in_proj(x2, wts, cos, sin, tm)

    conv0 = jnp.pad(conv_buf.astype(F32), ((0, 0), (SUBLANE - (CONV_W - 1), 0), (0, 0)))
    o_gdn, gdn_new = _gdn(qkv.reshape(b, t, CONV_CH), z.reshape(b, t, GDN_V_W), gb.reshape(b, t, LANE),
                          gdn_state.astype(F32), conv0, wts['conv_w'], wts['gdn_norm'], chunk)
    conv_new = qkv.reshape(b, t, CONV_CH)[:, t - (CONV_W - 1):, :]

    o_lat = attend(q.reshape(b, t, MLA_HEADS * QK_PAD), kv.reshape(b, t, QK_PAD))

    h1, xp, idx_t, rank_t, wtok, counts = _post_mix(x2, o_gdn.reshape(n, GDN_V_W),
                                                    o_lat.reshape(n, MLA_HEADS * KV_LORA), wts, tm)
    rows = n * TOP_K
    bm = _tile(rows, 256)
    cnt = counts[:, 0].astype(jnp.int32)
    vt, ve, seg, n_visits, starts = _visit_list(cnt, rows, bm)
    dest_t = _dest_rows(idx_t, rank_t, starts, _tile(n, 2048))
    xs = _scatter_rows(xp, dest_t, tm)
    ys = _experts(xs, vt, ve, seg, n_visits, wts['we_gu'], wts['we_d'], bm)
    y = _final(h1, ys, dest_t, wtok, p.reshape(n, -1), wts, tm)
    return y.reshape(b, t, d), ckv.reshape(b, t, KV_LORA), krope.reshape(b, t, QK_ROPE), gdn_new, conv_new


def kernel(x_prompt, x_sample, p_prompt, p_sample, cache_ckv, cache_krope, state_gdn, state_conv, page_table,
           g_mix, w_in, gdn_conv_w, gdn_a_log, gdn_dt_bias, gdn_norm, mla_g_q, mla_w_uq, mla_g_kv, mla_w_uk,
           mla_w_uv, w_out, g_ffn, w_router, router_bias, w_exp_gate, w_exp_up, w_exp_down, w_sh_gate,
           w_sh_up, w_sh_down, g_ple, w_ple_gate, w_ple_proj, g_final):
    depth = w_in.shape[0]
    assert depth == 1, "the final norm is fused into the (single) layer"
    bp, tp, _ = x_prompt.shape
    bs, ts, _ = x_sample.shape
    past = page_table.shape[1] * cache_ckv.shape[2]
    wts = _prep_weights(g_mix[0], w_in[0], gdn_conv_w[0], gdn_a_log[0], gdn_dt_bias[0], gdn_norm[0], mla_g_q[0],
                        mla_w_uq[0], mla_g_kv[0], mla_w_uk[0], mla_w_uv[0], w_out[0], g_ffn[0], w_router[0],
                        router_bias[0], w_exp_gate[0], w_exp_up[0], w_exp_down[0], w_sh_gate[0], w_sh_up[0],
                        w_sh_down[0], g_ple[0], w_ple_gate[0], w_ple_proj[0], g_final)

    gdn0 = jnp.zeros((bp, GDN_HEADS, GDN_DK, GDN_DV), state_gdn.dtype)
    conv0 = jnp.zeros((bp, CONV_W - 1, CONV_CH), state_conv.dtype)
    attend_p = functools.partial(_prompt_attn, tq=_tile(tp, 256), tk=_tile(tp, 512))
    yp, c1, k1, s1, v1 = _layer(x_prompt, p_prompt[0], jnp.arange(tp), gdn0, conv0, attend_p, wts,
                                chunk=_tile(tp, 64))

    cache_krope_t = jnp.swapaxes(cache_krope[0], 1, 2)

    def attend_s(q, kv):
        return _paged_attn(q, kv, cache_ckv[0], cache_krope_t, page_table,
                           pages_per_step=_tile(page_table.shape[1], 16))

    ys, c2, k2, s2, v2 = _layer(x_sample, p_sample[0], past + jnp.arange(ts), state_gdn[0], state_conv[0],
                                attend_s, wts, chunk=ts)
    st = lambda a, ref: a.astype(ref.dtype)[None]
    return (yp, ys, c1[None], k1[None], st(s1, state_gdn), st(v1, state_conv),
            c2[None], k2[None], st(s2, state_gdn), st(v2, state_conv))
```

```python
import functools

import jax
import jax.numpy as jnp
import numpy as np
from jax import lax
from jax.experimental import pallas as pl
from jax.experimental.pallas import tpu as pltpu

EPS = 1e-6
ROPE_THETA = 10000.0
GDN_HEADS = 4
GDN_DK = 128
GDN_DV = 128
CONV_W = 4
MLA_HEADS = 4
Q_LORA = 256
KV_LORA = 256
QK_NOPE = 128
QK_ROPE = 64
V_DIM = 128
N_EXPERTS = 64
TOP_K = 8
N_GROUPS = 8
TOPK_GROUPS = 4
GROUP_SIZE = N_EXPERTS // N_GROUPS
D_EXPERT = 256
D_SHARED = 256
ROUTED_SCALE = 2.5
MLA_SCALE = (QK_NOPE + QK_ROPE) ** -0.5

GDN_QK_W = GDN_HEADS * GDN_DK
GDN_V_W = GDN_HEADS * GDN_DV
CONV_CH = 2 * GDN_QK_W + GDN_V_W
LANE = 128
SUBLANE = 8
QK_PAD = KV_LORA + LANE
VMEM_LIMIT = 56 * 1024 * 1024
D_MODEL = 1024
ROW_SLAB = D_MODEL // 2 // LANE

C_QKV = 0
C_Z = C_QKV + CONV_CH
C_CQ = C_Z + GDN_V_W
C_CKV = C_CQ + Q_LORA
C_KR = C_CKV + KV_LORA
C_KRR = C_KR + QK_ROPE
C_BA = C_KRR + QK_ROPE
W1_WIDTH = C_BA + LANE

F32 = jnp.float32
BF16 = jnp.bfloat16
HI = lax.Precision.HIGHEST
NT = (((1,), (1,)), ((), ()))
TN = (((0,), (0,)), ((), ()))


def _cparams(sem):
    return pltpu.CompilerParams(dimension_semantics=sem, vmem_limit_bytes=VMEM_LIMIT)


def _rms(x, g):
    return x * lax.rsqrt(jnp.mean(x * x, axis=-1, keepdims=True) + EPS) * g


def _sigmoid(x):
    return 1.0 / (1.0 + jnp.exp(-x))


def _silu(x):
    return x * _sigmoid(x)


def _full(shape):
    return pl.BlockSpec(shape, lambda *_: (0,) * len(shape))


def _in_proj_kernel(x_ref, gmix_ref, w1_ref, alog_ref, dtb_ref, gq_ref, wuq_ref, wuk_ref, gkv_ref,
                    cos_ref, sin_ref,
                    qkv_ref, z_ref, gb_ref, ckv_ref, krope_ref, kv_ref, q_ref):
    u = _rms(x_ref[...], gmix_ref[...]).astype(BF16)
    qkv_ref[...] = jnp.dot(u, w1_ref[:, C_QKV:C_Z], preferred_element_type=F32)
    z_ref[...] = jnp.dot(u, w1_ref[:, C_Z:C_CQ], preferred_element_type=F32)
    rest = jnp.dot(u, w1_ref[:, C_CQ:W1_WIDTH], preferred_element_type=F32)
    c_q = rest[:, 0:Q_LORA]
    c_kv = rest[:, C_CKV - C_CQ:C_KR - C_CQ]
    kr2 = rest[:, C_KR - C_CQ:C_BA - C_CQ]
    ba = rest[:, C_BA - C_CQ:]

    lane = lax.broadcasted_iota(jnp.int32, ba.shape, 1)
    beta = _sigmoid(ba)
    sp_in = ba + dtb_ref[...]
    softplus = jnp.maximum(sp_in, 0.0) + jnp.log(1.0 + jnp.exp(-jnp.abs(sp_in)))
    g = -jnp.exp(alog_ref[...]) * softplus
    gb_ref[...] = jnp.where(lane < GDN_HEADS, beta, g)

    cos = cos_ref[...]
    sin = sin_ref[...]
    ckv = _rms(c_kv, gkv_ref[...])
    ckv_ref[...] = ckv
    krope = kr2[:, :QK_ROPE] * cos[:, :QK_ROPE] + kr2[:, QK_ROPE:] * sin[:, :QK_ROPE]
    krope_ref[...] = krope
    zpad = jnp.zeros((ckv.shape[0], LANE - QK_ROPE), F32)
    kv_ref[...] = jnp.concatenate([ckv, krope, zpad], axis=1).astype(BF16)

    cqn = _rms(c_q, gq_ref[...]).astype(BF16)
    qh = jnp.dot(cqn, wuq_ref[...], preferred_element_type=F32)
    nope_w = MLA_HEADS * QK_NOPE
    rope_w = MLA_HEADS * QK_ROPE
    q_rope = qh[:, nope_w:nope_w + rope_w] * cos + qh[:, nope_w + rope_w:] * sin
    parts = []
    for h in range(MLA_HEADS):
        q_nope = qh[:, h * QK_NOPE:(h + 1) * QK_NOPE].astype(BF16)
        q_lat = jnp.dot(q_nope, wuk_ref[h], preferred_element_type=F32)
        parts += [q_lat, q_rope[:, h * QK_ROPE:(h + 1) * QK_ROPE], zpad]
    q_ref[...] = (jnp.concatenate(parts, axis=1) * MLA_SCALE).astype(BF16)


def _in_proj(x, wts, cos, sin, tm):
    n, d = x.shape
    period = cos.shape[0]
    nper = period // tm
    tok = lambda w: pl.BlockSpec((tm, w), lambda i: (i, 0))
    pos = pl.BlockSpec((tm, cos.shape[1]), lambda i: (i % nper, 0))
    out_shapes = (
        jax.ShapeDtypeStruct((n, CONV_CH), F32),
        jax.ShapeDtypeStruct((n, GDN_V_W), F32),
        jax.ShapeDtypeStruct((n, LANE), F32),
        jax.ShapeDtypeStruct((n, KV_LORA), F32),
        jax.ShapeDtypeStruct((n, QK_ROPE), F32),
        jax.ShapeDtypeStruct((n, QK_PAD), BF16),
        jax.ShapeDtypeStruct((n, MLA_HEADS * QK_PAD), BF16),
    )
    return pl.pallas_call(
        _in_proj_kernel,
        grid=(n // tm,),
        in_specs=[tok(d), _full((1, d)), _full(wts['w1'].shape), _full((1, LANE)), _full((1, LANE)),
                  _full((1, Q_LORA)), _full(wts['wuq'].shape), _full(wts['wuk'].shape), _full((1, KV_LORA)),
                  pos, pos],
        out_specs=[tok(CONV_CH), tok(GDN_V_W), tok(LANE), tok(KV_LORA), tok(QK_ROPE), tok(QK_PAD),
                   tok(MLA_HEADS * QK_PAD)],
        out_shape=out_shapes,
        compiler_params=_cparams(("arbitrary",)),
        name="in_proj",
    )(x, wts['g_mix'], wts['w1'], wts['alog'], wts['dtb'], wts['g_q'], wts['wuq'], wts['wuk'], wts['g_kv'],
      cos, sin)


def _gdn_kernel(qkv_ref, z_ref, gb_ref, s0_ref, conv0_ref, *rest, chunk):
    consts = rest[:-4]
    o_ref, sout_ref, s_scr, xbuf = rest[-4:]
    c = pl.program_id(1)

    @pl.when(c == 0)
    def _():
        s_scr[...] = s0_ref[...]
        xbuf[:, 0:SUBLANE, :] = conv0_ref[...]

    for bi in range(qkv_ref.shape[0]):
        _gdn_sequence_step(qkv_ref.at[bi], z_ref.at[bi], gb_ref.at[bi], *consts,
                           o_ref.at[bi], s_scr.at[bi], xbuf.at[bi], chunk=chunk)

    @pl.when(c == pl.num_programs(1) - 1)
    def _():
        sout_ref[...] = s_scr[...]


def _gdn_sequence_step(qkv_ref, z_ref, gb_ref, cw_ref, gn_ref,
                       cmat_ref, gsel_ref, ssel_ref, bsel_ref, place_ref, triu_ref, mask_ref, hsel_ref,
                       o_ref, s_scr, xbuf, *, chunk):
    pre = SUBLANE
    tail = CONV_W - 1

    xbuf[pre:pre + chunk, :] = qkv_ref[...]
    y = xbuf[pre - tail:pre - tail + chunk, :] * cw_ref[0:1, :]
    for j in range(1, CONV_W):
        y = y + xbuf[pre - tail + j:pre - tail + j + chunk, :] * cw_ref[j:j + 1, :]
    y = _silu(y)
    xbuf[pre - tail:pre, :] = xbuf[pre + chunk - tail:pre + chunk, :]

    stack = lambda a, off: jnp.concatenate(
        [a[:, off + h * GDN_DK:off + (h + 1) * GDN_DK] for h in range(GDN_HEADS)], axis=0)
    q = stack(y, 0)
    k = stack(y, GDN_QK_W)
    v = stack(y, 2 * GDN_QK_W)
    qn = q * lax.rsqrt(jnp.sum(q * q, axis=-1, keepdims=True) + EPS) * (GDN_DK ** -0.5)
    kn = k * lax.rsqrt(jnp.sum(k * k, axis=-1, keepdims=True) + EPS)

    gb = gb_ref[...]
    r = GDN_HEADS * chunk
    lane_of = lambda a, sel: jnp.sum(a * sel, axis=-1, keepdims=True)
    cums = jnp.dot(cmat_ref[...], gb, precision=HI, preferred_element_type=F32)
    gsel = gsel_ref[...]
    gcol = lane_of(cums[0:r], gsel)
    glast = lane_of(cums[r:2 * r], gsel)
    gstate = lane_of(cums[2 * r:], ssel_ref[...])
    beta = lane_of(jnp.concatenate([gb] * GDN_HEADS, axis=0), bsel_ref[...])
    spread = jnp.dot(gb, place_ref[...], precision=HI, preferred_element_type=F32)
    grow = jnp.sum(spread * triu_ref[...], axis=0, keepdims=True)

    incl = mask_ref[0]
    decay = jnp.exp(jnp.where(incl > 0.5, gcol - grow, -jnp.inf))
    kb = kn * beta
    knb = kn.astype(BF16)
    kk = lax.dot_general(kb.astype(BF16), knb, NT, preferred_element_type=F32)
    lower = kk * decay * mask_ref[1]
    inv = mask_ref[2] - lower * mask_ref[3]
    for lvl in range(4, mask_ref.shape[0]):
        ih = inv.astype(BF16)
        il = (inv - ih.astype(F32)).astype(BF16)
        blk = (lower * mask_ref[lvl]).astype(BF16)
        t2 = jnp.dot(jnp.concatenate([ih, il], axis=0), blk, preferred_element_type=F32)
        t = t2[:r] + t2[r:]
        th = t.astype(BF16)
        tl = (t - th.astype(F32)).astype(BF16)
        c2 = jnp.dot(jnp.concatenate([th, tl], axis=0), ih, preferred_element_type=F32)
        inv = inv - (c2[:r] + c2[r:] + jnp.dot(th, il, preferred_element_type=F32))
    egc = jnp.exp(gcol)
    ih = inv.astype(BF16)
    il = (inv - ih.astype(F32)).astype(BF16)
    rhs = jnp.concatenate([v * beta, kb * egc], axis=1)
    rh = rhs.astype(BF16)
    rl = (rhs - rh.astype(F32)).astype(BF16)
    uw2 = jnp.dot(jnp.concatenate([ih, il], axis=0), rh, preferred_element_type=F32)
    uw = uw2[:r] + uw2[r:] + jnp.dot(ih, rl, preferred_element_type=F32)
    u = uw[:, :GDN_DV]
    w = uw[:, GDN_DV:]
    qk = lax.dot_general(qn.astype(BF16), knb, NT, preferred_element_type=F32) * decay

    hsel = hsel_ref[...]
    wide = lambda a: (jnp.concatenate([a] * GDN_HEADS, axis=1) * hsel).astype(BF16)
    st = s_scr[...]
    stb = st.astype(BF16)
    ws_qs = jnp.dot(jnp.concatenate([wide(w), wide(qn * egc)], axis=0), stb, preferred_element_type=F32)
    v_new = u - ws_qs[:r]
    vnb = v_new.astype(BF16)
    o = ws_qs[r:] + jnp.dot(qk.astype(BF16), vnb, preferred_element_type=F32)
    kdec = wide(kn * jnp.exp(glast - gcol))
    s_scr[...] = st * jnp.exp(gstate) + lax.dot_general(kdec, vnb, TN, preferred_element_type=F32)

    out = _rms(o, gn_ref[...]) * _silu(stack(z_ref[...], 0))
    for h in range(GDN_HEADS):
        o_ref[:, h * GDN_DV:(h + 1) * GDN_DV] = out[h * chunk:(h + 1) * chunk, :].astype(o_ref.dtype)


def _gdn_constants(chunk):
    r = GDN_HEADS * chunk
    ri = np.arange(r)
    head, tok = ri // chunk, ri % chunk
    ti = np.arange(chunk)
    lane = np.arange(LANE)
    srow = np.arange(GDN_HEADS * GDN_DK)
    cmat = np.concatenate([(ti[None, :] <= tok[:, None]), np.ones((r, chunk), bool),
                           np.ones((GDN_HEADS * GDN_DK, chunk), bool)], axis=0)
    gsel = lane[None, :] == GDN_HEADS + head[:, None]
    ssel = lane[None, :] == GDN_HEADS + (srow // GDN_DK)[:, None]
    bsel = lane[None, :] == head[:, None]
    place = lane[:, None] == GDN_HEADS + head[None, :]
    triu = ti[:, None] <= tok[None, :]
    same = head[:, None] == head[None, :]
    masks = [same & (tok[:, None] >= tok[None, :]), same & (tok[:, None] > tok[None, :]),
             ri[:, None] == ri[None, :]]
    s = 1
    while s < chunk:
        masks.append(same & ((tok[:, None] // s) % 2 == 1) & ((tok[None, :] // s) == (tok[:, None] // s) - 1))
        s *= 2
    hsel = (np.arange(GDN_HEADS * GDN_DK)[None, :] // GDN_DK) == head[:, None]
    f = lambda a: jnp.asarray(np.asarray(a, np.float32))
    return f(cmat), f(gsel), f(ssel), f(bsel), f(place), f(triu), f(np.stack(masks)), f(hsel)


def _gdn(qkv, z, gb, s0, conv0, cw, gn, chunk):
    b, t, _ = qkv.shape
    assert t % chunk == 0 and chunk % SUBLANE == 0
    nc = t // chunk
    nb = 2 if b % 2 == 0 else 1
    consts = _gdn_constants(chunk)
    tokb = lambda w: pl.BlockSpec((nb, chunk, w), lambda i, c: (i, c, 0))
    state = pl.BlockSpec((nb, GDN_HEADS * GDN_DK, GDN_DV), lambda i, c: (i, 0, 0))
    o, s_new = pl.pallas_call(
        functools.partial(_gdn_kernel, chunk=chunk),
        grid=(b // nb, nc),
        in_specs=[tokb(CONV_CH), tokb(GDN_V_W), tokb(LANE), state,
                  pl.BlockSpec((nb, SUBLANE, CONV_CH), lambda i, c: (i, 0, 0)),
                  _full((SUBLANE, CONV_CH)), _full((1, GDN_DV))] + [_full(a.shape) for a in consts],
        out_specs=[tokb(GDN_V_W), state],
        out_shape=(jax.ShapeDtypeStruct((b, t, GDN_V_W), BF16),
                   jax.ShapeDtypeStruct((b, GDN_HEADS * GDN_DK, GDN_DV), F32)),
        scratch_shapes=[pltpu.VMEM((nb, GDN_HEADS * GDN_DK, GDN_DV), F32),
                        pltpu.VMEM((nb, SUBLANE + chunk, CONV_CH), F32)],
        compiler_params=_cparams(("arbitrary", "arbitrary")),
        name="gdn",
    )(qkv, z, gb, s0.reshape(b, GDN_HEADS * GDN_DK, GDN_DV), conv0, cw, gn, *consts)
    return o, s_new.reshape(b, GDN_HEADS, GDN_DK, GDN_DV)


def _stack_heads(q):
    return jnp.concatenate([q[:, h * QK_PAD:(h + 1) * QK_PAD] for h in range(MLA_HEADS)], axis=0)


def _online_softmax_step(s, vals, m_scr, l_scr, acc_scr):
    m_prev = m_scr[...]
    m_new = jnp.maximum(m_prev, jnp.max(s, axis=-1, keepdims=True))
    alpha = jnp.exp(m_prev - m_new)
    p = jnp.exp(s - m_new)
    l_scr[...] = alpha * l_scr[...] + jnp.sum(p, axis=-1, keepdims=True)
    pb = p.astype(BF16)
    if isinstance(vals, (list, tuple)):
        w = s.shape[1] // len(vals)
        pv = sum(jnp.dot(pb[:, i * w:(i + 1) * w], vi, preferred_element_type=F32) for i, vi in enumerate(vals))
    else:
        pv = jnp.dot(pb, vals, preferred_element_type=F32)
    acc_scr[...] = alpha * acc_scr[...] + pv
    m_scr[...] = m_new


def _unstack_store(o_ref, acc_scr, l_scr, rows):
    o = acc_scr[...] / l_scr[...]
    for h in range(MLA_HEADS):
        o_ref[:, h * KV_LORA:(h + 1) * KV_LORA] = o[h * rows:(h + 1) * rows, :].astype(o_ref.dtype)


def _prompt_attn_kernel(q_ref, kv_ref, o_ref, qs_scr, m_scr, l_scr, acc_scr, *, tq, tk):
    i = pl.program_id(1)
    j = pl.program_id(2)
    nj = pl.num_programs(2)

    @pl.when(j == 0)
    def _():
        qs_scr[...] = _stack_heads(q_ref[...])
        m_scr[...] = jnp.full(m_scr.shape, -jnp.inf, F32)
        l_scr[...] = jnp.zeros(l_scr.shape, F32)
        acc_scr[...] = jnp.zeros(acc_scr.shape, F32)

    needed = j * tk <= i * tq + tq - 1
    unmasked = j * tk + tk - 1 <= i * tq

    @pl.when(unmasked)
    def _():
        kv = kv_ref[...]
        s = lax.dot_general(qs_scr[...], kv, NT, preferred_element_type=F32)
        _online_softmax_step(s, kv[:, :KV_LORA], m_scr, l_scr, acc_scr)

    @pl.when(needed & jnp.logical_not(unmasked))
    def _():
        kv = kv_ref[...]
        s = lax.dot_general(qs_scr[...], kv, NT, preferred_element_type=F32)
        qpos = i * tq + lax.broadcasted_iota(jnp.int32, s.shape, 0) % tq
        kpos = j * tk + lax.broadcasted_iota(jnp.int32, s.shape, 1)
        s = jnp.where(kpos <= qpos, s, -jnp.inf)
        _online_softmax_step(s, kv[:, :KV_LORA], m_scr, l_scr, acc_scr)

    @pl.when(j == nj - 1)
    def _():
        _unstack_store(o_ref, acc_scr, l_scr, tq)


def _prompt_attn(q, kv, tq, tk):
    b, s, _ = q.shape
    rows = MLA_HEADS * tq
    last = lambda i: (i * tq + tq - 1) // tk
    return pl.pallas_call(
        functools.partial(_prompt_attn_kernel, tq=tq, tk=tk),
        grid=(b, s // tq, s // tk),
        in_specs=[pl.BlockSpec((None, tq, MLA_HEADS * QK_PAD), lambda bb, i, j: (bb, i, 0)),
                  pl.BlockSpec((None, tk, QK_PAD), lambda bb, i, j: (bb, jnp.minimum(j, last(i)), 0))],
        out_specs=pl.BlockSpec((None, tq, MLA_HEADS * KV_LORA), lambda bb, i, j: (bb, i, 0)),
        out_shape=jax.ShapeDtypeStruct((b, s, MLA_HEADS * KV_LORA), BF16),
        scratch_shapes=[pltpu.VMEM((rows, QK_PAD), BF16), pltpu.VMEM((rows, 1), F32),
                        pltpu.VMEM((rows, 1), F32), pltpu.VMEM((rows, KV_LORA), F32)],
        compiler_params=_cparams(("arbitrary", "arbitrary", "arbitrary")),
        name="prompt_attn",
    )(q, kv)


def _paged_attn_kernel(pt_ref, q_ref, kvnew_ref, *rest, pages_per_step, t):
    ckv_refs = rest[:pages_per_step]
    kr_refs = rest[pages_per_step:2 * pages_per_step]
    o_ref, qs_scr, m_scr, l_scr, acc_scr = rest[2 * pages_per_step:]
    g = pl.program_id(1)
    ng = pl.num_programs(1)

    @pl.when(g == 0)
    def _():
        qs_scr[...] = _stack_heads(q_ref[...])
        m_scr[...] = jnp.full(m_scr.shape, -jnp.inf, F32)
        l_scr[...] = jnp.zeros(l_scr.shape, F32)
        acc_scr[...] = jnp.zeros(acc_scr.shape, F32)

    qs = qs_scr[...]
    q_lat = qs[:, :KV_LORA]
    q_rope = qs[:, KV_LORA:KV_LORA + QK_ROPE]
    pages, scores = [], []
    for p in range(pages_per_step):
        ckv = ckv_refs[p][...].astype(BF16)
        kr_t = kr_refs[p][...].astype(BF16)
        pages.append(ckv)
        scores.append(lax.dot_general(q_lat, ckv, NT, preferred_element_type=F32)
                      + jnp.dot(q_rope, kr_t, preferred_element_type=F32))
    _online_softmax_step(jnp.concatenate(scores, axis=1), pages, m_scr, l_scr, acc_scr)

    @pl.when(g == ng - 1)
    def _():
        kvn = kvnew_ref[...]
        s = lax.dot_general(qs, kvn, NT, preferred_element_type=F32)
        qpos = lax.broadcasted_iota(jnp.int32, s.shape, 0) % t
        kpos = lax.broadcasted_iota(jnp.int32, s.shape, 1)
        s = jnp.where(kpos <= qpos, s, -jnp.inf)
        _online_softmax_step(s, kvn[:, :KV_LORA], m_scr, l_scr, acc_scr)
        _unstack_store(o_ref, acc_scr, l_scr, t)


def _paged_attn(q, kvnew, cache_ckv, cache_krope_t, page_table, pages_per_step):
    b, t, _ = q.shape
    n_pages = page_table.shape[1]
    page = cache_ckv.shape[1]
    assert n_pages % pages_per_step == 0
    rows = MLA_HEADS * t

    def page_spec(shape, p):
        return pl.BlockSpec((None,) + shape,
                            lambda bb, g, pt: (pt[bb * n_pages + g * pages_per_step + p], 0, 0))

    grid_spec = pltpu.PrefetchScalarGridSpec(
        num_scalar_prefetch=1,
        grid=(b, n_pages // pages_per_step),
        in_specs=([pl.BlockSpec((None, t, MLA_HEADS * QK_PAD), lambda bb, g, pt: (bb, 0, 0)),
                   pl.BlockSpec((None, t, QK_PAD), lambda bb, g, pt: (bb, 0, 0))]
                  + [page_spec((page, KV_LORA), p) for p in range(pages_per_step)]
                  + [page_spec((QK_ROPE, page), p) for p in range(pages_per_step)]),
        out_specs=pl.BlockSpec((None, t, MLA_HEADS * KV_LORA), lambda bb, g, pt: (bb, 0, 0)),
        scratch_shapes=[pltpu.VMEM((rows, QK_PAD), BF16), pltpu.VMEM((rows, 1), F32),
                        pltpu.VMEM((rows, 1), F32), pltpu.VMEM((rows, KV_LORA), F32)],
    )
    return pl.pallas_call(
        functools.partial(_paged_attn_kernel, pages_per_step=pages_per_step, t=t),
        grid_spec=grid_spec,
        out_shape=jax.ShapeDtypeStruct((b, t, MLA_HEADS * KV_LORA), BF16),
        compiler_params=_cparams(("arbitrary", "arbitrary")),
        name="paged_attn",
    )(page_table.reshape(-1), q, kvnew, *([cache_ckv] * pages_per_step), *([cache_krope_t] * pages_per_step))


def _pack_bf16_pairs(x):
    w = x.shape[1] // 2
    xb = x.astype(BF16).astype(F32)
    lo = lax.shift_right_logical(lax.bitcast_convert_type(xb[:, :w], jnp.uint32), jnp.uint32(16))
    hi = lax.bitcast_convert_type(xb[:, w:], jnp.uint32) & jnp.uint32(0xFFFF0000)
    return lo | hi


def _store_row_slabs(ref, first, x):
    t, w = x.shape
    per = w // LANE
    for j in range(per):
        ref[pl.ds(first * per + j, t, stride=per), :] = x[:, j * LANE:(j + 1) * LANE]


def _load_row_slabs(ref, first, t, w):
    per = w // LANE
    return jnp.concatenate([ref[pl.ds(first * per + j, t, stride=per), :] for j in range(per)], axis=1)


def _unpack_bf16_pairs(p):
    lo = lax.bitcast_convert_type(lax.shift_left(p, jnp.uint32(16)), F32)
    hi = lax.bitcast_convert_type(p & jnp.uint32(0xFFFF0000), F32)
    return jnp.concatenate([lo, hi], axis=1)


def _post_mix_kernel(h_ref, ogdn_ref, olat_ref, wuv_ref, wout_ref, gffn_ref, wrh_ref, wrl_ref, rbias_ref,
                     wsgu_ref, wsd_ref,
                     h1_ref, xp_ref, idx_ref, rank_ref, wtok_ref, cnt_ref, carry):
    i = pl.program_id(0)
    tm = h_ref.shape[0]

    @pl.when(i == 0)
    def _():
        carry[...] = jnp.zeros(carry.shape, F32)

    parts = [ogdn_ref[...]]
    for hh in range(MLA_HEADS):
        o_h = jnp.dot(olat_ref[:, hh * KV_LORA:(hh + 1) * KV_LORA], wuv_ref[hh], preferred_element_type=F32)
        parts.append(o_h.astype(BF16))
    mix = jnp.concatenate(parts, axis=1)
    h1 = h_ref[...] + jnp.dot(mix, wout_ref[...], preferred_element_type=F32)
    u2 = _rms(h1, gffn_ref[...])
    u2b = u2.astype(BF16)
    _store_row_slabs(xp_ref, 0, _pack_bf16_pairs(u2))

    u2l = (u2 - u2b.astype(F32)).astype(BF16)
    logits = (lax.dot_general(wrh_ref[...], u2b, NT, preferred_element_type=F32)
              + lax.dot_general(wrh_ref[...], u2l, NT, preferred_element_type=F32)
              + lax.dot_general(wrl_ref[...], u2b, NT, preferred_element_type=F32))
    scores = _sigmoid(logits)
    biased = scores + rbias_ref[...]
    neg = -jnp.inf

    sub = lax.broadcasted_iota(jnp.int32, (GROUP_SIZE, tm), 0)
    gscore = []
    for gi in range(N_GROUPS):
        vg = biased[gi * GROUP_SIZE:(gi + 1) * GROUP_SIZE, :]
        m1 = jnp.max(vg, axis=0, keepdims=True)
        first = jnp.min(jnp.where(vg == m1, sub, GROUP_SIZE), axis=0, keepdims=True)
        m2 = jnp.max(jnp.where(sub == first, neg, vg), axis=0, keepdims=True)
        gscore.append(m1 + m2)
    gs = jnp.concatenate(gscore, axis=0)
    gid = lax.broadcasted_iota(jnp.int32, gs.shape, 0)
    gsel = jnp.zeros(gs.shape, F32)
    for _ in range(TOPK_GROUPS):
        m = jnp.max(gs, axis=0, keepdims=True)
        pick = jnp.min(jnp.where(gs == m, gid, N_GROUPS), axis=0, keepdims=True)
        hit = gid == pick
        gsel = jnp.where(hit, 1.0, gsel)
        gs = jnp.where(hit, neg, gs)
    emask = jnp.concatenate(
        [jnp.broadcast_to(gsel[gi:gi + 1, :], (GROUP_SIZE, tm)) for gi in range(N_GROUPS)], axis=0)
    cand = jnp.where(emask > 0.5, biased, neg)

    eid = lax.broadcasted_iota(jnp.int32, cand.shape, 0)
    chosen = jnp.zeros(cand.shape, F32)
    idx_rows, w_rows, hits = [], [], []
    for _ in range(TOP_K):
        m = jnp.max(cand, axis=0, keepdims=True)
        pick = jnp.min(jnp.where(cand == m, eid, N_EXPERTS), axis=0, keepdims=True)
        hit = eid == pick
        idx_rows.append(pick)
        w_rows.append(jnp.sum(jnp.where(hit, scores, 0.0), axis=0, keepdims=True))
        hits.append(hit)
        chosen = chosen + hit.astype(F32)
        cand = jnp.where(hit, neg, cand)
    wsel = jnp.concatenate(w_rows, axis=0)
    wsel = wsel / (jnp.sum(wsel, axis=0, keepdims=True) + 1e-20) * ROUTED_SCALE
    idx_ref[...] = jnp.concatenate(idx_rows, axis=0)

    ti = lax.broadcasted_iota(jnp.int32, (tm, tm), 0)
    tj = lax.broadcasted_iota(jnp.int32, (tm, tm), 1)
    before = (ti < tj).astype(BF16)
    prefix = jnp.dot(chosen.astype(BF16), before, preferred_element_type=F32) + carry[:, 0:1]
    rank_rows = [jnp.sum(jnp.where(hit, prefix, 0.0), axis=0, keepdims=True) for hit in hits]
    rank_ref[...] = jnp.concatenate(rank_rows, axis=0).astype(jnp.int32)
    carry[...] = carry[...] + jnp.sum(chosen, axis=1, keepdims=True)
    cnt_ref[...] = carry[...]

    wpad = jnp.concatenate([wsel, jnp.zeros((LANE - TOP_K, tm), F32)], axis=0)
    wtok_ref[...] = wpad.T

    gu = jnp.dot(u2b, wsgu_ref[...], preferred_element_type=F32)
    hs = (_silu(gu[:, :D_SHARED]) * gu[:, D_SHARED:]).astype(BF16)
    h1_ref[...] = h1 + jnp.dot(hs, wsd_ref[...], preferred_element_type=F32)


def _post_mix(h, o_gdn, o_lat, wts, tm):
    n, d = h.shape
    tok = lambda w: pl.BlockSpec((tm, w), lambda i: (i, 0))
    tr = lambda r: pl.BlockSpec((r, tm), lambda i: (0, i))
    return pl.pallas_call(
        _post_mix_kernel,
        grid=(n // tm,),
        in_specs=[tok(d), tok(GDN_V_W), tok(MLA_HEADS * KV_LORA), _full(wts['wuv'].shape), _full((d, d)),
                  _full((1, d)), _full((N_EXPERTS, d)), _full((N_EXPERTS, d)), _full((N_EXPERTS, 1)),
                  _full((d, 2 * D_SHARED)), _full((D_SHARED, d))],
        out_specs=[tok(d), pl.BlockSpec((tm * ROW_SLAB, LANE), lambda i: (i, 0)), tr(TOP_K), tr(TOP_K), tok(LANE),
                   _full((N_EXPERTS, LANE))],
        out_shape=(jax.ShapeDtypeStruct((n, d), F32),
                   jax.ShapeDtypeStruct((n * ROW_SLAB, LANE), jnp.uint32),
                   jax.ShapeDtypeStruct((TOP_K, n), jnp.int32),
                   jax.ShapeDtypeStruct((TOP_K, n), jnp.int32),
                   jax.ShapeDtypeStruct((n, LANE), F32),
                   jax.ShapeDtypeStruct((N_EXPERTS, LANE), F32)),
        scratch_shapes=[pltpu.VMEM((N_EXPERTS, LANE), F32)],
        compiler_params=_cparams(("arbitrary",)),
        name="post_mix",
    )(h, o_gdn, o_lat, wts['wuv'], wts['w_out'], wts['g_ffn'], wts['wr_hi'], wts['wr_lo'], wts['rbias'],
      wts['ws_gu'], wts['ws_d'])


def _dest_kernel(idx_ref, rank_ref, starts_ref, o_ref):
    eid = lax.broadcasted_iota(jnp.int32, (N_EXPERTS, idx_ref.shape[1]), 0)
    starts = starts_ref[...]
    rows = []
    for kk in range(TOP_K):
        hit = eid == idx_ref[kk:kk + 1, :]
        rows.append(jnp.sum(jnp.where(hit, starts, 0.0), axis=0, keepdims=True))
    o_ref[...] = jnp.concatenate(rows, axis=0).astype(jnp.int32) + rank_ref[...]


def _dest_rows(idx_t, rank_t, starts, tn):
    k, n = idx_t.shape
    assert n * k < 2 ** 24
    tr = pl.BlockSpec((k, tn), lambda i: (0, i))
    return pl.pallas_call(
        _dest_kernel,
        grid=(n // tn,),
        in_specs=[tr, tr, _full((N_EXPERTS, 1))],
        out_specs=tr,
        out_shape=jax.ShapeDtypeStruct((k, n), jnp.int32),
        compiler_params=_cparams(("arbitrary",)),
        name="dest_rows",
    )(idx_t, rank_t, starts.astype(F32).reshape(N_EXPERTS, 1))


def _slab(ref, row):
    return ref.at[pl.ds(pl.multiple_of(row * ROW_SLAB, ROW_SLAB), ROW_SLAB)]


def _scatter_kernel(dest_ref, x_ref, o_hbm, sem):
    fan, nrow = dest_ref.shape

    def start(r, carry):
        for kk in range(fan):
            pltpu.make_async_copy(_slab(x_ref, r), _slab(o_hbm, dest_ref[kk, r]), sem).start(priority=kk % 2)
        return carry

    lax.fori_loop(0, nrow, start, 0)
    whole = o_hbm.at[pl.ds(0, fan * nrow * ROW_SLAB)]
    pltpu.make_async_copy(whole, whole, sem).wait()


def _scatter_rows(x, dest_t, ts):
    w = x.shape[1]
    n = x.shape[0] // ROW_SLAB
    fan = dest_t.shape[0]
    return pl.pallas_call(
        _scatter_kernel,
        grid=(n // ts,),
        in_specs=[pl.BlockSpec((fan, ts), lambda i: (0, i), memory_space=pltpu.SMEM),
                  pl.BlockSpec((ts * ROW_SLAB, w), lambda i: (i, 0))],
        out_specs=pl.BlockSpec(memory_space=pl.ANY),
        out_shape=jax.ShapeDtypeStruct((n * fan * ROW_SLAB, w), x.dtype),
        scratch_shapes=[pltpu.SemaphoreType.DMA],
        compiler_params=pltpu.CompilerParams(dimension_semantics=("arbitrary",), has_side_effects=True),
        name="scatter_rows",
    )(dest_t, x)


def _experts_kernel(vt_ref, ve_ref, seg_ref, nv_ref, x_ref, wgu_ref, wd_ref, o_ref, *, bm):
    v = pl.program_id(0)
    tile = vt_ref[v]
    e = ve_ref[v]
    prev_tile = vt_ref[jnp.maximum(v - 1, 0)]
    first = (v == 0) | (tile != prev_tile)

    @pl.when(v < nv_ref[0])
    def _():
        w = ROW_SLAB * LANE
        x = _unpack_bf16_pairs(_load_row_slabs(x_ref, 0, bm, w)).astype(BF16)
        gu = jnp.dot(x, wgu_ref[...], preferred_element_type=F32)
        hid = (_silu(gu[:, :D_EXPERT]) * gu[:, D_EXPERT:]).astype(BF16)
        y = _pack_bf16_pairs(jnp.dot(hid, wd_ref[...], preferred_element_type=F32))
        rowid = tile * bm + lax.broadcasted_iota(jnp.int32, y.shape, 0)
        mine = (rowid >= seg_ref[e]) & (rowid < seg_ref[e + 1])

        @pl.when(first)
        def _():
            _store_row_slabs(o_ref, 0, jnp.where(mine, y, jnp.uint32(0)))

        @pl.when(jnp.logical_not(first))
        def _():
            _store_row_slabs(o_ref, 0, jnp.where(mine, y, _load_row_slabs(o_ref, 0, bm, w)))


def _experts(xs, visit_tile, visit_expert, seg, n_visits, wgu, wd, bm):
    rows, w = xs.shape
    nv_max = visit_tile.shape[0]
    d = wd.shape[2]
    grid_spec = pltpu.PrefetchScalarGridSpec(
        num_scalar_prefetch=4,
        grid=(nv_max,),
        in_specs=[pl.BlockSpec((bm * ROW_SLAB, w), lambda v, vt, ve, sg, nv: (vt[v], 0)),
                  pl.BlockSpec((None, d, 2 * D_EXPERT), lambda v, vt, ve, sg, nv: (ve[v], 0, 0)),
                  pl.BlockSpec((None, D_EXPERT, d), lambda v, vt, ve, sg, nv: (ve[v], 0, 0))],
        out_specs=pl.BlockSpec((bm * ROW_SLAB, w), lambda v, vt, ve, sg, nv: (vt[v], 0)),
    )
    return pl.pallas_call(
        functools.partial(_experts_kernel, bm=bm),
        grid_spec=grid_spec,
        out_shape=jax.ShapeDtypeStruct((rows, w), jnp.uint32),
        compiler_params=_cparams(("arbitrary",)),
        name="experts",
    )(visit_tile, visit_expert, seg, n_visits, xs, wgu, wd)


def _visit_list(counts, rows, bm):
    ntiles = rows // bm
    nv_max = ntiles + N_EXPERTS - 1
    ends = jnp.cumsum(counts)
    starts = ends - counts
    first_tile = starts // bm
    ntile_e = jnp.where(counts > 0, (ends - 1) // bm - first_tile + 1, 0)
    vend = jnp.cumsum(ntile_e)
    vstart = vend - ntile_e
    n_visits = vend[-1]
    v = jnp.arange(nv_max, dtype=jnp.int32)
    ve = jnp.minimum(jnp.sum(vend[None, :] <= v[:, None], axis=1), N_EXPERTS - 1).astype(jnp.int32)
    vt = (first_tile[ve] + v - vstart[ve]).astype(jnp.int32)
    last = jnp.maximum(n_visits - 1, 0)
    ve = jnp.where(v < n_visits, ve, ve[last])
    vt = jnp.where(v < n_visits, vt, vt[last])
    seg = jnp.concatenate([starts, ends[-1:]]).astype(jnp.int32)
    return vt, ve, seg, n_visits.reshape(1).astype(jnp.int32), starts


def _final_kernel(dcur_ref, dnext_ref, h1_ref, wtok_ref, p_ref, gple_ref, wgate_ref, wproj_ref, gfin_ref,
                  ys_hbm, o_ref, ybuf, sem):
    i = pl.program_id(0)
    nsteps = pl.num_programs(0)
    tm = h1_ref.shape[0]
    slot = i % 2

    def fetch(dest_ref, into):
        def body(r, carry):
            for kk in range(TOP_K):
                pltpu.make_async_copy(_slab(ys_hbm, dest_ref[kk, r]), _slab(ybuf.at[into], kk * tm + r),
                                      sem.at[into]).start(priority=kk % 2)
            return carry
        lax.fori_loop(0, tm, body, 0)

    @pl.when(i == 0)
    def _():
        fetch(dcur_ref, 0)

    @pl.when(i + 1 < nsteps)
    def _():
        fetch(dnext_ref, 1 - slot)

    pltpu.make_async_copy(ys_hbm.at[pl.ds(0, TOP_K * tm * ROW_SLAB)], ybuf.at[slot], sem.at[slot]).wait()

    wtok = wtok_ref[...]
    h2 = h1_ref[...]
    for kk in range(TOP_K):
        yk = _unpack_bf16_pairs(_load_row_slabs(ybuf.at[slot], kk * tm, tm, ROW_SLAB * LANE))
        h2 = h2 + yk * wtok[:, kk:kk + 1]
    n = _rms(h2, gple_ref[...]).astype(BF16)
    gate = _sigmoid(jnp.dot(n, wgate_ref[...], preferred_element_type=F32))
    pp = jnp.dot(p_ref[...].astype(BF16), wproj_ref[...], preferred_element_type=F32)
    o_ref[...] = _rms(h2 + gate * pp, gfin_ref[...])


def _final(h1, ys, dest_t, wtok, p, wts, tm):
    n, d = h1.shape
    nsteps = n // tm
    assert ys.shape[0] >= TOP_K * tm * ROW_SLAB
    tok = lambda w: pl.BlockSpec((tm, w), lambda i: (i, 0))
    dspec = lambda f: pl.BlockSpec((TOP_K, tm), f, memory_space=pltpu.SMEM)
    return pl.pallas_call(
        _final_kernel,
        grid=(nsteps,),
        in_specs=[dspec(lambda i: (0, i)), dspec(lambda i: (0, jnp.minimum(i + 1, nsteps - 1))),
                  tok(d), tok(LANE), tok(p.shape[1]), _full((1, d)), _full((d, d)),
                  _full((p.shape[1], d)), _full((1, d)), pl.BlockSpec(memory_space=pl.ANY)],
        out_specs=tok(d),
        out_shape=jax.ShapeDtypeStruct((n, d), F32),
        scratch_shapes=[pltpu.VMEM((2, TOP_K * tm * ROW_SLAB, ys.shape[1]), ys.dtype),
                        pltpu.SemaphoreType.DMA((2,))],
        compiler_params=_cparams(("arbitrary",)),
        name="final",
    )(dest_t, dest_t, h1, wtok, p, wts['g_ple'], wts['w_ple_gate'], wts['w_ple_proj'], wts['g_final'], ys)


def _rotate_half_cols(w):
    half = QK_ROPE // 2
    wh = w.reshape(w.shape[0], -1, 2, half)
    return jnp.concatenate([-wh[:, :, 1:2], wh[:, :, 0:1]], axis=2).reshape(w.shape)


def _prep_weights(g_mix, w_in, gdn_conv_w, gdn_a_log, gdn_dt_bias, gdn_norm, mla_g_q, mla_w_uq, mla_g_kv,
                  mla_w_uk, mla_w_uv, w_out, g_ffn, w_router, router_bias, w_exp_gate, w_exp_up, w_exp_down,
                  w_sh_gate, w_sh_up, w_sh_down, g_ple, w_ple_gate, w_ple_proj, g_final):
    d = w_in.shape[0]
    splits = np.cumsum([CONV_CH, GDN_V_W, GDN_HEADS, GDN_HEADS, Q_LORA, KV_LORA])
    w_qkv, w_z, w_b, w_a, w_cq, w_ckv, w_kr = jnp.split(w_in, [int(s) for s in splits], axis=1)
    ba_pad = jnp.zeros((d, LANE - 2 * GDN_HEADS), w_in.dtype)
    w1 = jnp.concatenate([w_qkv, w_z, w_cq, w_ckv, w_kr, _rotate_half_cols(w_kr), w_b, w_a, ba_pad], axis=1)
    assert w1.shape[1] == W1_WIDTH
    lane_pad = lambda v: jnp.pad(v.astype(F32), (GDN_HEADS, LANE - 2 * GDN_HEADS)).reshape(1, LANE)
    uq = mla_w_uq.reshape(Q_LORA, MLA_HEADS, QK_NOPE + QK_ROPE)
    uq_nope = uq[:, :, :QK_NOPE].reshape(Q_LORA, MLA_HEADS * QK_NOPE)
    uq_rope = uq[:, :, QK_NOPE:].reshape(Q_LORA, MLA_HEADS * QK_ROPE)
    wuq = jnp.concatenate([uq_nope, uq_rope, _rotate_half_cols(uq_rope)], axis=1)
    wr_t = w_router.T.astype(F32)
    wr_hi = wr_t.astype(BF16)
    row = lambda v: v.astype(F32).reshape(1, -1)
    return dict(
        g_mix=row(g_mix), w1=w1.astype(BF16), alog=lane_pad(gdn_a_log), dtb=lane_pad(gdn_dt_bias),
        g_q=row(mla_g_q), wuq=wuq.astype(BF16), wuk=jnp.transpose(mla_w_uk, (1, 2, 0)).astype(BF16),
        g_kv=row(mla_g_kv),
        conv_w=jnp.pad(gdn_conv_w.astype(F32), ((0, SUBLANE - CONV_W), (0, 0))), gdn_norm=row(gdn_norm),
        wuv=jnp.transpose(mla_w_uv, (1, 0, 2)).astype(BF16), w_out=w_out.astype(BF16), g_ffn=row(g_ffn),
        wr_hi=wr_hi, wr_lo=(wr_t - wr_hi.astype(F32)).astype(BF16),
        rbias=router_bias.astype(F32).reshape(N_EXPERTS, 1),
        ws_gu=jnp.concatenate([w_sh_gate, w_sh_up], axis=1).astype(BF16), ws_d=w_sh_down.astype(BF16),
        we_gu=jnp.concatenate([w_exp_gate, w_exp_up], axis=2).astype(BF16), we_d=w_exp_down.astype(BF16),
        g_ple=row(g_ple), w_ple_gate=w_ple_gate.astype(BF16), w_ple_proj=w_ple_proj.astype(BF16),
        g_final=row(g_final),
    )


def _rope_tables(pos, reps):
    half = QK_ROPE // 2
    inv = ROPE_THETA ** (-jnp.arange(half, dtype=F32) / half)
    ang = pos.astype(F32)[:, None] * inv[None, :]
    cos = jnp.concatenate([jnp.cos(ang)] * 2, axis=1)
    sin = jnp.concatenate([jnp.sin(ang)] * 2, axis=1)
    cos = jnp.tile(jnp.tile(cos, (1, MLA_HEADS)), (reps, 1))
    sin = jnp.tile(jnp.tile(sin, (1, MLA_HEADS)), (reps, 1))
    return cos, sin


def _tile(n, pref):
    t = min(n, pref)
    assert n % t == 0
    return t


def _layer(x, p, pos, gdn_state, conv_buf, attend, wts, chunk):
    b, t, d = x.shape
    n = b * t
    tm = _tile(n, 256)
    if t >= tm:
        assert t % tm == 0
        cos, sin = _rope_tables(pos, 1)
    else:
        assert tm % t == 0
        cos, sin = _rope_tables(pos, tm // t)
    x2 = x.reshape(n, d)
    qkv, z, gb, ckv, krope, kv, q = _in_proj(x2, wts, cos, sin, tm)

    conv0 = jnp.pad(conv_buf.astype(F32), ((0, 0), (SUBLANE - (CONV_W - 1), 0), (0, 0)))
    o_gdn, gdn_new = _gdn(qkv.reshape(b, t, CONV_CH), z.reshape(b, t, GDN_V_W), gb.reshape(b, t, LANE),
                          gdn_state.astype(F32), conv0, wts['conv_w'], wts['gdn_norm'], chunk)
    conv_new = qkv.reshape(b, t, CONV_CH)[:, t - (CONV_W - 1):, :]

    o_lat = attend(q.reshape(b, t, MLA_HEADS * QK_PAD), kv.reshape(b, t, QK_PAD))

    h1, xp, idx_t, rank_t, wtok, counts = _post_mix(x2, o_gdn.reshape(n, GDN_V_W),
                                                    o_lat.reshape(n, MLA_HEADS * KV_LORA), wts, tm)
    rows = n * TOP_K
    bm = _tile(rows, 512)
    cnt = counts[:, 0].astype(jnp.int32)
    vt, ve, seg, n_visits, starts = _visit_list(cnt, rows, bm)
    dest_t = _dest_rows(idx_t, rank_t, starts, _tile(n, 2048))
    xs = _scatter_rows(xp, dest_t, tm)
    ys = _experts(xs, vt, ve, seg, n_visits, wts['we_gu'], wts['we_d'], bm)
    y = _final(h1, ys, dest_t, wtok, p.reshape(n, -1), wts, tm)
    return y.reshape(b, t, d), ckv.reshape(b, t, KV_LORA), krope.reshape(b, t, QK_ROPE), gdn_new, conv_new


def kernel(x_prompt, x_sample, p_prompt, p_sample, cache_ckv, cache_krope, state_gdn, state_conv, page_table,
           g_mix, w_in, gdn_conv_w, gdn_a_log, gdn_dt_bias, gdn_norm, mla_g_q, mla_w_uq, mla_g_kv, mla_w_uk,
           mla_w_uv, w_out, g_ffn, w_router, router_bias, w_exp_gate, w_exp_up, w_exp_down, w_sh_gate,
           w_sh_up, w_sh_down, g_ple, w_ple_gate, w_ple_proj, g_final):
    depth = w_in.shape[0]
    assert depth == 1, "the final norm is fused into the (single) layer"
    bp, tp, _ = x_prompt.shape
    bs, ts, _ = x_sample.shape
    past = page_table.shape[1] * cache_ckv.shape[2]
    wts = _prep_weights(g_mix[0], w_in[0], gdn_conv_w[0], gdn_a_log[0], gdn_dt_bias[0], gdn_norm[0], mla_g_q[0],
                        mla_w_uq[0], mla_g_kv[0], mla_w_uk[0], mla_w_uv[0], w_out[0], g_ffn[0], w_router[0],
                        router_bias[0], w_exp_gate[0], w_exp_up[0], w_exp_down[0], w_sh_gate[0], w_sh_up[0],
                        w_sh_down[0], g_ple[0], w_ple_gate[0], w_ple_proj[0], g_final)

    gdn0 = jnp.zeros((bp, GDN_HEADS, GDN_DK, GDN_DV), state_gdn.dtype)
    conv0 = jnp.zeros((bp, CONV_W - 1, CONV_CH), state_conv.dtype)
    attend_p = functools.partial(_prompt_attn, tq=_tile(tp, 256), tk=_tile(tp, 512))
    yp, c1, k1, s1, v1 = _layer(x_prompt, p_prompt[0], jnp.arange(tp), gdn0, conv0, attend_p, wts,
                                chunk=_tile(tp, 64))

    cache_krope_t = jnp.swapaxes(cache_krope[0], 1, 2)

    def attend_s(q, kv):
        return _paged_attn(q, kv, cache_ckv[0], cache_krope_t, page_table,
                           pages_per_step=_tile(page_table.shape[1], 16))

    ys, c2, k2, s2, v2 = _layer(x_sample, p_sample[0], past + jnp.arange(ts), state_gdn[0], state_conv[0],
                                attend_s, wts, chunk=ts)
    st = lambda a, ref: a.astype(ref.dtype)[None]
    return (yp, ys, c1[None], k1[None], st(s1, state_gdn), st(v1, state_conv),
            c2[None], k2[None], st(s2, state_gdn), st(v2, state_conv))
```

```python
import functools

import jax
import jax.numpy as jnp
import numpy as np
from jax import lax
from jax.experimental import pallas as pl
from jax.experimental.pallas import tpu as pltpu

EPS = 1e-6
ROPE_THETA = 10000.0
GDN_HEADS = 4
GDN_DK = 128
GDN_DV = 128
CONV_W = 4
MLA_HEADS = 4
Q_LORA = 256
KV_LORA = 256
QK_NOPE = 128
QK_ROPE = 64
V_DIM = 128
N_EXPERTS = 64
TOP_K = 8
N_GROUPS = 8
TOPK_GROUPS = 4
GROUP_SIZE = N_EXPERTS // N_GROUPS
D_EXPERT = 256
D_SHARED = 256
ROUTED_SCALE = 2.5
MLA_SCALE = (QK_NOPE + QK_ROPE) ** -0.5

GDN_QK_W = GDN_HEADS * GDN_DK
GDN_V_W = GDN_HEADS * GDN_DV
CONV_CH = 2 * GDN_QK_W + GDN_V_W
LANE = 128
SUBLANE = 8
QK_PAD = KV_LORA + LANE
VMEM_LIMIT = 56 * 1024 * 1024
D_MODEL = 1024
ROW_SLAB = D_MODEL // 2 // LANE

C_QKV = 0
C_Z = C_QKV + CONV_CH
C_CQ = C_Z + GDN_V_W
C_CKV = C_CQ + Q_LORA
C_KR = C_CKV + KV_LORA
C_KRR = C_KR + QK_ROPE
C_BA = C_KRR + QK_ROPE
W1_WIDTH = C_BA + LANE

F32 = jnp.float32
BF16 = jnp.bfloat16
NT = (((1,), (1,)), ((), ()))
TN = (((0,), (0,)), ((), ()))


def _cparams(sem, **flags):
    return pltpu.CompilerParams(dimension_semantics=sem, vmem_limit_bytes=VMEM_LIMIT, flags=flags or None)


def _rms(x, g):
    return x * lax.rsqrt(jnp.mean(x * x, axis=-1, keepdims=True) + EPS) * g


def _sigmoid(x):
    return 1.0 / (1.0 + jnp.exp(-x))


def _silu(x):
    return x * _sigmoid(x)


def _full(shape):
    return pl.BlockSpec(shape, lambda *_: (0,) * len(shape))


def _in_proj_kernel(x_ref, gmix_ref, w1_ref, alog_ref, dtb_ref, gq_ref, wuq_ref, wuk_ref, gkv_ref,
                    cos_ref, sin_ref,
                    qkv_ref, z_ref, gb_ref, ckv_ref, krope_ref, kv_ref, q_ref):
    u = _rms(x_ref[...], gmix_ref[...]).astype(BF16)
    qkv_ref[...] = jnp.dot(u, w1_ref[:, C_QKV:C_Z], preferred_element_type=F32)
    z_ref[...] = jnp.dot(u, w1_ref[:, C_Z:C_CQ], preferred_element_type=F32)
    rest = jnp.dot(u, w1_ref[:, C_CQ:W1_WIDTH], preferred_element_type=F32)
    c_q = rest[:, 0:Q_LORA]
    c_kv = rest[:, C_CKV - C_CQ:C_KR - C_CQ]
    kr2 = rest[:, C_KR - C_CQ:C_BA - C_CQ]
    ba = rest[:, C_BA - C_CQ:]

    lane = lax.broadcasted_iota(jnp.int32, ba.shape, 1)
    beta = _sigmoid(ba)
    sp_in = ba + dtb_ref[...]
    softplus = jnp.maximum(sp_in, 0.0) + jnp.log(1.0 + jnp.exp(-jnp.abs(sp_in)))
    g = -jnp.exp(alog_ref[...]) * softplus
    gb_ref[...] = jnp.where(lane < GDN_HEADS, beta, g)

    cos = cos_ref[...]
    sin = sin_ref[...]
    ckv = _rms(c_kv, gkv_ref[...])
    ckv_ref[...] = ckv
    krope = kr2[:, :QK_ROPE] * cos[:, :QK_ROPE] + kr2[:, QK_ROPE:] * sin[:, :QK_ROPE]
    krope_ref[...] = krope
    zpad = jnp.zeros((ckv.shape[0], LANE - QK_ROPE), F32)
    kv_ref[...] = jnp.concatenate([ckv, krope, zpad], axis=1).astype(BF16)

    cqn = _rms(c_q, gq_ref[...]).astype(BF16)
    qh = jnp.dot(cqn, wuq_ref[...], preferred_element_type=F32)
    nope_w = MLA_HEADS * QK_NOPE
    rope_w = MLA_HEADS * QK_ROPE
    q_rope = qh[:, nope_w:nope_w + rope_w] * cos + qh[:, nope_w + rope_w:] * sin
    parts = []
    for h in range(MLA_HEADS):
        q_nope = qh[:, h * QK_NOPE:(h + 1) * QK_NOPE].astype(BF16)
        q_lat = jnp.dot(q_nope, wuk_ref[h], preferred_element_type=F32)
        parts += [q_lat, q_rope[:, h * QK_ROPE:(h + 1) * QK_ROPE], zpad]
    q_ref[...] = (jnp.concatenate(parts, axis=1) * MLA_SCALE).astype(BF16)


def _in_proj(x, wts, cos, sin, tm):
    n, d = x.shape
    period = cos.shape[0]
    nper = period // tm
    tok = lambda w: pl.BlockSpec((tm, w), lambda i: (i, 0))
    pos = pl.BlockSpec((tm, cos.shape[1]), lambda i: (i % nper, 0))
    out_shapes = (
        jax.ShapeDtypeStruct((n, CONV_CH), F32),
        jax.ShapeDtypeStruct((n, GDN_V_W), F32),
        jax.ShapeDtypeStruct((n, LANE), F32),
        jax.ShapeDtypeStruct((n, KV_LORA), F32),
        jax.ShapeDtypeStruct((n, QK_ROPE), F32),
        jax.ShapeDtypeStruct((n, QK_PAD), BF16),
        jax.ShapeDtypeStruct((n, MLA_HEADS * QK_PAD), BF16),
    )
    return pl.pallas_call(
        _in_proj_kernel,
        grid=(n // tm,),
        in_specs=[tok(d), _full((1, d)), _full(wts['w1'].shape), _full((1, LANE)), _full((1, LANE)),
                  _full((1, Q_LORA)), _full(wts['wuq'].shape), _full(wts['wuk'].shape), _full((1, KV_LORA)),
                  pos, pos],
        out_specs=[tok(CONV_CH), tok(GDN_V_W), tok(LANE), tok(KV_LORA), tok(QK_ROPE), tok(QK_PAD),
                   tok(MLA_HEADS * QK_PAD)],
        out_shape=out_shapes,
        compiler_params=_cparams(("arbitrary",)),
        name="in_proj",
    )(x, wts['g_mix'], wts['w1'], wts['alog'], wts['dtb'], wts['g_q'], wts['wuq'], wts['wuk'], wts['g_kv'],
      cos, sin)


def _gdn_kernel(qkv_ref, z_ref, gb_ref, s0_ref, conv0_ref, *rest, chunk):
    consts = rest[:-4]
    o_ref, sout_ref, s_scr, xbuf = rest[-4:]
    c = pl.program_id(1)

    @pl.when(c == 0)
    def _():
        s_scr[...] = s0_ref[...]
        xbuf[:, 0:SUBLANE, :] = conv0_ref[...]

    cw_ref, gn_ref = consts[0], consts[1]
    nb, rows = qkv_ref.shape[0], qkv_ref.shape[1]
    chains = {}
    for bi in range(nb):
        y = _gdn_conv(qkv_ref.at[bi], cw_ref, xbuf.at[bi])
        for ci in range(rows // chunk):
            tok = slice(ci * chunk, (ci + 1) * chunk)
            chains[bi, ci] = _gdn_chunk_prep(y[tok], gb_ref[bi, tok, :], *consts[2:], chunk=chunk)
    prepped = dict(zip(chains, _round_robin(list(chains.values()))))
    for ci in range(rows // chunk):
        for bi in range(nb):
            tok = slice(ci * chunk, (ci + 1) * chunk)
            out = _gdn_chunk_state(prepped[bi, ci], z_ref[bi, tok, :], gn_ref, consts[-1], s_scr.at[bi], chunk=chunk)
            for h in range(GDN_HEADS):
                o_ref[bi, tok, h * GDN_DV:(h + 1) * GDN_DV] = out[h * chunk:(h + 1) * chunk, :].astype(o_ref.dtype)

    @pl.when(c == pl.num_programs(1) - 1)
    def _():
        sout_ref[...] = s_scr[...]


def _gdn_conv(qkv_ref, cw_ref, xbuf):
    pre = SUBLANE
    tail = CONV_W - 1
    rows = qkv_ref.shape[0]
    xbuf[pre:pre + rows, :] = qkv_ref[...]
    y = xbuf[pre - tail:pre - tail + rows, :] * cw_ref[0:1, :]
    for j in range(1, CONV_W):
        y = y + xbuf[pre - tail + j:pre - tail + j + rows, :] * cw_ref[j:j + 1, :]
    xbuf[pre - tail:pre, :] = xbuf[pre + rows - tail:pre + rows, :]
    return _silu(y)


def _stack_gdn_heads(a, off):
    return jnp.concatenate([a[:, off + h * GDN_DK:off + (h + 1) * GDN_DK] for h in range(GDN_HEADS)], axis=0)


def _round_robin(gens):
    results = [None] * len(gens)
    live = list(range(len(gens)))
    while live:
        for i in list(live):
            try:
                next(gens[i])
            except StopIteration as done:
                results[i] = done.value
                live.remove(i)
    return results


def _gdn_chunk_prep(y, gb, cmat_ref, gsel_ref, ssel_ref, bsel_ref, place_ref, triu_ref, mask_ref, hsel_ref, *, chunk):
    stack = _stack_gdn_heads
    q = stack(y, 0)
    k = stack(y, GDN_QK_W)
    v = stack(y, 2 * GDN_QK_W)
    qn = q * lax.rsqrt(jnp.sum(q * q, axis=-1, keepdims=True) + EPS) * (GDN_DK ** -0.5)
    kn = k * lax.rsqrt(jnp.sum(k * k, axis=-1, keepdims=True) + EPS)

    r = GDN_HEADS * chunk
    lane_of = lambda a, sel: jnp.sum(a * sel, axis=-1, keepdims=True)
    gb_hi = gb.astype(BF16)
    gb_r1 = gb - gb_hi.astype(F32)
    gb_mid = gb_r1.astype(BF16)
    gb_lo = (gb_r1 - gb_mid.astype(F32)).astype(BF16)
    pieces = (gb_hi, gb_mid, gb_lo)
    cums = sum(jnp.dot(cmat_ref[...], p, preferred_element_type=F32) for p in pieces)
    total = jnp.sum(gb, axis=0, keepdims=True)
    gsel = gsel_ref[...]
    gcol = lane_of(cums, gsel)
    glast = lane_of(total, gsel)
    gstate = lane_of(total, ssel_ref[...])
    beta = lane_of(jnp.concatenate([gb] * GDN_HEADS, axis=0), bsel_ref[...])
    spread = sum(jnp.dot(p, place_ref[...], preferred_element_type=F32) for p in pieces)
    grow = jnp.sum(spread * triu_ref[...], axis=0, keepdims=True)

    incl = mask_ref[0]
    decay = jnp.exp(jnp.where(incl > 0.5, gcol - grow, -jnp.inf))
    kb = kn * beta
    knb = kn.astype(BF16)
    kk = lax.dot_general(kb.astype(BF16), knb, NT, preferred_element_type=F32)
    yield
    lower = kk * decay * mask_ref[1]
    inv = mask_ref[2] - lower * mask_ref[3]
    for lvl in range(4, mask_ref.shape[0]):
        ih = inv.astype(BF16)
        il = (inv - ih.astype(F32)).astype(BF16)
        blk = (lower * mask_ref[lvl]).astype(BF16)
        t2 = jnp.dot(jnp.concatenate([ih, il], axis=0), blk, preferred_element_type=F32)
        yield
        t = t2[:r] + t2[r:]
        th = t.astype(BF16)
        tl = (t - th.astype(F32)).astype(BF16)
        c2 = jnp.dot(jnp.concatenate([th, tl], axis=0), ih, preferred_element_type=F32)
        c1 = jnp.dot(th, il, preferred_element_type=F32)
        yield
        inv = inv - (c2[:r] + c2[r:] + c1)
    egc = jnp.exp(gcol)
    ih = inv.astype(BF16)
    il = (inv - ih.astype(F32)).astype(BF16)
    rhs = jnp.concatenate([v * beta, kb * egc], axis=1)
    rh = rhs.astype(BF16)
    rl = (rhs - rh.astype(F32)).astype(BF16)
    uw2 = jnp.dot(jnp.concatenate([ih, il], axis=0), rh, preferred_element_type=F32)
    uw1 = jnp.dot(ih, rl, preferred_element_type=F32)
    yield
    uw = uw2[:r] + uw2[r:] + uw1
    u = uw[:, :GDN_DV]
    w = uw[:, GDN_DV:]
    qk = lax.dot_general(qn.astype(BF16), knb, NT, preferred_element_type=F32) * decay

    hsel = hsel_ref[...]
    wide = lambda a: (jnp.concatenate([a] * GDN_HEADS, axis=1) * hsel).astype(BF16)
    return dict(u=u, w_qe=jnp.concatenate([wide(w), wide(qn * egc)], axis=0), qk=qk.astype(BF16),
                kdec=wide(kn * jnp.exp(glast - gcol)), sdecay=jnp.exp(gstate))


def _gdn_chunk_state(p, z, gn_ref, hsel_ref, s_scr, *, chunk):
    r = GDN_HEADS * chunk
    st = s_scr[...]
    ws_qs = jnp.dot(p['w_qe'], st.astype(BF16), preferred_element_type=F32)
    v_new = (p['u'] - ws_qs[:r]).astype(BF16)
    o = ws_qs[r:] + jnp.dot(p['qk'], v_new, preferred_element_type=F32)
    s_scr[...] = st * p['sdecay'] + lax.dot_general(p['kdec'], v_new, TN, preferred_element_type=F32)
    return _rms(o, gn_ref[...]) * _silu(_stack_gdn_heads(z, 0))


def _gdn_constants(chunk):
    r = GDN_HEADS * chunk
    ri = np.arange(r)
    head, tok = ri // chunk, ri % chunk
    ti = np.arange(chunk)
    lane = np.arange(LANE)
    srow = np.arange(GDN_HEADS * GDN_DK)
    cmat = ti[None, :] <= tok[:, None]
    gsel = lane[None, :] == GDN_HEADS + head[:, None]
    ssel = lane[None, :] == GDN_HEADS + (srow // GDN_DK)[:, None]
    bsel = lane[None, :] == head[:, None]
    place = lane[:, None] == GDN_HEADS + head[None, :]
    triu = ti[:, None] <= tok[None, :]
    same = head[:, None] == head[None, :]
    masks = [same & (tok[:, None] >= tok[None, :]), same & (tok[:, None] > tok[None, :]),
             ri[:, None] == ri[None, :]]
    s = 1
    while s < chunk:
        masks.append(same & ((tok[:, None] // s) % 2 == 1) & ((tok[None, :] // s) == (tok[:, None] // s) - 1))
        s *= 2
    hsel = (np.arange(GDN_HEADS * GDN_DK)[None, :] // GDN_DK) == head[:, None]
    f = lambda a: jnp.asarray(np.asarray(a, np.float32))
    fb = lambda a: f(a).astype(BF16)
    return fb(cmat), f(gsel), f(ssel), f(bsel), fb(place), f(triu), f(np.stack(masks)), f(hsel)


def _gdn(qkv, z, gb, s0, conv0, cw, gn, chunk):
    b, t, _ = qkv.shape
    assert t % chunk == 0 and chunk % SUBLANE == 0
    nb = 2 if b % 2 == 0 else 1
    nch = 2 if (t // chunk) % 2 == 0 else 1
    rows = nch * chunk
    consts = _gdn_constants(chunk)
    tokb = lambda w: pl.BlockSpec((nb, rows, w), lambda i, c: (i, c, 0))
    state = pl.BlockSpec((nb, GDN_HEADS * GDN_DK, GDN_DV), lambda i, c: (i, 0, 0))
    o, s_new = pl.pallas_call(
        functools.partial(_gdn_kernel, chunk=chunk),
        grid=(b // nb, t // rows),
        in_specs=[tokb(CONV_CH), tokb(GDN_V_W), tokb(LANE), state,
                  pl.BlockSpec((nb, SUBLANE, CONV_CH), lambda i, c: (i, 0, 0)),
                  _full((SUBLANE, CONV_CH)), _full((1, GDN_DV))] + [_full(a.shape) for a in consts],
        out_specs=[tokb(GDN_V_W), state],
        out_shape=(jax.ShapeDtypeStruct((b, t, GDN_V_W), BF16),
                   jax.ShapeDtypeStruct((b, GDN_HEADS * GDN_DK, GDN_DV), F32)),
        scratch_shapes=[pltpu.VMEM((nb, GDN_HEADS * GDN_DK, GDN_DV), F32),
                        pltpu.VMEM((nb, SUBLANE + rows, CONV_CH), F32)],
        compiler_params=_cparams(("arbitrary", "arbitrary")),
        name="gdn",
    )(qkv, z, gb, s0.reshape(b, GDN_HEADS * GDN_DK, GDN_DV), conv0, cw, gn, *consts)
    return o, s_new.reshape(b, GDN_HEADS, GDN_DK, GDN_DV)


def _stack_heads(q):
    return jnp.concatenate([q[:, h * QK_PAD:(h + 1) * QK_PAD] for h in range(MLA_HEADS)], axis=0)


def _online_softmax_step(s, vals, m_scr, l_scr, acc_scr):
    m_prev = m_scr[...]
    m_new = jnp.maximum(m_prev, jnp.max(s, axis=-1, keepdims=True))
    alpha = jnp.exp(m_prev - m_new)
    p = jnp.exp(s - m_new)
    l_scr[...] = alpha * l_scr[...] + jnp.sum(p, axis=-1, keepdims=True)
    pb = p.astype(BF16)
    if isinstance(vals, (list, tuple)):
        w = s.shape[1] // len(vals)
        pv = sum(jnp.dot(pb[:, i * w:(i + 1) * w], vi, preferred_element_type=F32) for i, vi in enumerate(vals))
    else:
        pv = jnp.dot(pb, vals, preferred_element_type=F32)
    acc_scr[...] = alpha * acc_scr[...] + pv
    m_scr[...] = m_new


def _unstack_store(o_ref, acc_scr, l_scr, rows):
    o = acc_scr[...] / l_scr[...]
    for h in range(MLA_HEADS):
        o_ref[:, h * KV_LORA:(h + 1) * KV_LORA] = o[h * rows:(h + 1) * rows, :].astype(o_ref.dtype)


def _prompt_attn_kernel(q_ref, kv_ref, o_ref, qs_scr, m_scr, l_scr, acc_scr, *, tq, tk):
    i = pl.program_id(1)
    j = pl.program_id(2)
    nj = pl.num_programs(2)

    @pl.when(j == 0)
    def _():
        qs_scr[...] = _stack_heads(q_ref[...])
        m_scr[...] = jnp.full(m_scr.shape, -jnp.inf, F32)
        l_scr[...] = jnp.zeros(l_scr.shape, F32)
        acc_scr[...] = jnp.zeros(acc_scr.shape, F32)

    needed = j * tk <= i * tq + tq - 1
    unmasked = j * tk + tk - 1 <= i * tq

    @pl.when(unmasked)
    def _():
        kv = kv_ref[...]
        s = lax.dot_general(qs_scr[...], kv, NT, preferred_element_type=F32)
        _online_softmax_step(s, kv[:, :KV_LORA], m_scr, l_scr, acc_scr)

    @pl.when(needed & jnp.logical_not(unmasked))
    def _():
        kv = kv_ref[...]
        s = lax.dot_general(qs_scr[...], kv, NT, preferred_element_type=F32)
        qpos = i * tq + lax.broadcasted_iota(jnp.int32, s.shape, 0) % tq
        kpos = j * tk + lax.broadcasted_iota(jnp.int32, s.shape, 1)
        s = jnp.where(kpos <= qpos, s, -jnp.inf)
        _online_softmax_step(s, kv[:, :KV_LORA], m_scr, l_scr, acc_scr)

    @pl.when(j == nj - 1)
    def _():
        _unstack_store(o_ref, acc_scr, l_scr, tq)


def _prompt_attn(q, kv, tq, tk):
    b, s, _ = q.shape
    rows = MLA_HEADS * tq
    last = lambda i: (i * tq + tq - 1) // tk
    return pl.pallas_call(
        functools.partial(_prompt_attn_kernel, tq=tq, tk=tk),
        grid=(b, s // tq, s // tk),
        in_specs=[pl.BlockSpec((None, tq, MLA_HEADS * QK_PAD), lambda bb, i, j: (bb, i, 0)),
                  pl.BlockSpec((None, tk, QK_PAD), lambda bb, i, j: (bb, jnp.minimum(j, last(i)), 0))],
        out_specs=pl.BlockSpec((None, tq, MLA_HEADS * KV_LORA), lambda bb, i, j: (bb, i, 0)),
        out_shape=jax.ShapeDtypeStruct((b, s, MLA_HEADS * KV_LORA), BF16),
        scratch_shapes=[pltpu.VMEM((rows, QK_PAD), BF16), pltpu.VMEM((rows, 1), F32),
                        pltpu.VMEM((rows, 1), F32), pltpu.VMEM((rows, KV_LORA), F32)],
        compiler_params=_cparams(("arbitrary", "arbitrary", "arbitrary")),
        name="prompt_attn",
    )(q, kv)


def _paged_pages_chain(qs_scr, ckv_refs, kr_refs, m_scr, l_scr, acc_scr):
    qs = qs_scr[...]
    q_lat = qs[:, :KV_LORA]
    q_rope = qs[:, KV_LORA:KV_LORA + QK_ROPE]
    pages, scores = [], []
    for ckv_ref, kr_ref in zip(ckv_refs, kr_refs):
        ckv = ckv_ref[...].astype(BF16)
        kr_t = kr_ref[...].astype(BF16)
        pages.append(ckv)
        scores.append(lax.dot_general(q_lat, ckv, NT, preferred_element_type=F32)
                      + jnp.dot(q_rope, kr_t, preferred_element_type=F32))
    yield
    s = jnp.concatenate(scores, axis=1)
    m_prev = m_scr[...]
    m_new = jnp.maximum(m_prev, jnp.max(s, axis=-1, keepdims=True))
    alpha = jnp.exp(m_prev - m_new)
    p = jnp.exp(s - m_new)
    l_scr[...] = alpha * l_scr[...] + jnp.sum(p, axis=-1, keepdims=True)
    m_scr[...] = m_new
    pb = p.astype(BF16)
    w = s.shape[1] // len(pages)
    pv = sum(jnp.dot(pb[:, i * w:(i + 1) * w], vi, preferred_element_type=F32) for i, vi in enumerate(pages))
    yield
    acc_scr[...] = alpha * acc_scr[...] + pv


def _paged_attn_kernel(pt_ref, q_ref, kvnew_ref, ckv_hbm, kr_hbm, o_ref,
                       qs_scr, m_scr, l_scr, acc_scr, ckv_buf, kr_buf, sem, *, pages_per_step, n_pages, t):
    nb = q_ref.shape[0]
    npg = nb * pages_per_step
    bb = pl.program_id(0)
    g = pl.program_id(1)
    ng = pl.num_programs(1)
    step = bb * ng + g
    slot = step % 2

    def fetch(bb_, g_, into):
        for bi in range(nb):
            for p in range(pages_per_step):
                pid = pt_ref[(bb_ * nb + bi) * n_pages + g_ * pages_per_step + p]
                j = bi * pages_per_step + p
                pltpu.make_async_copy(ckv_hbm.at[pid], ckv_buf.at[into, j], sem.at[0, into]).start(priority=j % 2)
                pltpu.make_async_copy(kr_hbm.at[pid], kr_buf.at[into, j], sem.at[1, into]).start(priority=j % 2)

    @pl.when(step == 0)
    def _():
        fetch(bb, g, 0)

    @pl.when(step + 1 < pl.num_programs(0) * ng)
    def _():
        wrap = g + 1 == ng
        fetch(jnp.where(wrap, bb + 1, bb), jnp.where(wrap, 0, g + 1), 1 - slot)

    @pl.when(g == 0)
    def _():
        for bi in range(nb):
            qs_scr[bi] = _stack_heads(q_ref[bi])
        m_scr[...] = jnp.full(m_scr.shape, -jnp.inf, F32)
        l_scr[...] = jnp.zeros(l_scr.shape, F32)
        acc_scr[...] = jnp.zeros(acc_scr.shape, F32)

    pltpu.make_async_copy(ckv_hbm.at[pl.ds(0, npg)], ckv_buf.at[slot], sem.at[0, slot]).wait()
    pltpu.make_async_copy(kr_hbm.at[pl.ds(0, npg)], kr_buf.at[slot], sem.at[1, slot]).wait()

    _round_robin([
        _paged_pages_chain(qs_scr.at[bi],
                           [ckv_buf.at[slot, bi * pages_per_step + p] for p in range(pages_per_step)],
                           [kr_buf.at[slot, bi * pages_per_step + p] for p in range(pages_per_step)],
                           m_scr.at[bi], l_scr.at[bi], acc_scr.at[bi])
        for bi in range(nb)])

    @pl.when(g == ng - 1)
    def _():
        for bi in range(nb):
            kvn = kvnew_ref[bi]
            s = lax.dot_general(qs_scr[bi], kvn, NT, preferred_element_type=F32)
            qpos = lax.broadcasted_iota(jnp.int32, s.shape, 0) % t
            kpos = lax.broadcasted_iota(jnp.int32, s.shape, 1)
            s = jnp.where(kpos <= qpos, s, -jnp.inf)
            _online_softmax_step(s, kvn[:, :KV_LORA], m_scr.at[bi], l_scr.at[bi], acc_scr.at[bi])
            _unstack_store(o_ref.at[bi], acc_scr.at[bi], l_scr.at[bi], t)


def _paged_attn(q, kvnew, cache_ckv, cache_krope_t, page_table, pages_per_step):
    b, t, _ = q.shape
    n_pages = page_table.shape[1]
    page = cache_ckv.shape[1]
    assert n_pages % pages_per_step == 0
    rows = MLA_HEADS * t
    nb = 2 if b % 2 == 0 else 1

    npg = nb * pages_per_step
    assert cache_ckv.shape[0] >= npg
    grid_spec = pltpu.PrefetchScalarGridSpec(
        num_scalar_prefetch=1,
        grid=(b // nb, n_pages // pages_per_step),
        in_specs=[pl.BlockSpec((nb, t, MLA_HEADS * QK_PAD), lambda bb, g, pt: (bb, 0, 0)),
                  pl.BlockSpec((nb, t, QK_PAD), lambda bb, g, pt: (bb, 0, 0)),
                  pl.BlockSpec(memory_space=pl.ANY), pl.BlockSpec(memory_space=pl.ANY)],
        out_specs=pl.BlockSpec((nb, t, MLA_HEADS * KV_LORA), lambda bb, g, pt: (bb, 0, 0)),
        scratch_shapes=[pltpu.VMEM((nb, rows, QK_PAD), BF16), pltpu.VMEM((nb, rows, 1), F32),
                        pltpu.VMEM((nb, rows, 1), F32), pltpu.VMEM((nb, rows, KV_LORA), F32),
                        pltpu.VMEM((2, npg, page, KV_LORA), cache_ckv.dtype),
                        pltpu.VMEM((2, npg, QK_ROPE, page), cache_krope_t.dtype),
                        pltpu.SemaphoreType.DMA((2, 2))],
    )
    return pl.pallas_call(
        functools.partial(_paged_attn_kernel, pages_per_step=pages_per_step, n_pages=n_pages, t=t),
        grid_spec=grid_spec,
        out_shape=jax.ShapeDtypeStruct((b, t, MLA_HEADS * KV_LORA), BF16),
        compiler_params=_cparams(("arbitrary", "arbitrary")),
        name="paged_attn",
    )(page_table.reshape(-1), q, kvnew, cache_ckv, cache_krope_t)


def _pack_bf16_pairs(x):
    w = x.shape[1] // 2
    xb = x.astype(BF16).astype(F32)
    lo = lax.shift_right_logical(lax.bitcast_convert_type(xb[:, :w], jnp.uint32), jnp.uint32(16))
    hi = lax.bitcast_convert_type(xb[:, w:], jnp.uint32) & jnp.uint32(0xFFFF0000)
    return lo | hi


def _store_row_slabs(ref, first, x):
    t, w = x.shape
    per = w // LANE
    for j in range(per):
        ref[pl.ds(first * per + j, t, stride=per), :] = x[:, j * LANE:(j + 1) * LANE]


def _load_row_slabs(ref, first, t, w):
    per = w // LANE
    return jnp.concatenate([ref[pl.ds(first * per + j, t, stride=per), :] for j in range(per)], axis=1)


def _unpack_bf16_pairs(p):
    lo = lax.bitcast_convert_type(lax.shift_left(p, jnp.uint32(16)), F32)
    hi = lax.bitcast_convert_type(p & jnp.uint32(0xFFFF0000), F32)
    return jnp.concatenate([lo, hi], axis=1)


def _post_mix_kernel(h_ref, ogdn_ref, olat_ref, wuv_ref, wout_ref, gffn_ref, wrh_ref, wrl_ref, rbias_ref,
                     wsgu_ref, wsd_ref,
                     h1_ref, xp_ref, idx_ref, rank_ref, wtok_ref, cnt_ref, carry):
    i = pl.program_id(0)
    tm = h_ref.shape[0]

    @pl.when(i == 0)
    def _():
        carry[...] = jnp.zeros(carry.shape, F32)

    parts = [ogdn_ref[...]]
    for hh in range(MLA_HEADS):
        o_h = jnp.dot(olat_ref[:, hh * KV_LORA:(hh + 1) * KV_LORA], wuv_ref[hh], preferred_element_type=F32)
        parts.append(o_h.astype(BF16))
    mix = jnp.concatenate(parts, axis=1)
    h1 = h_ref[...] + jnp.dot(mix, wout_ref[...], preferred_element_type=F32)
    u2 = _rms(h1, gffn_ref[...])
    u2b = u2.astype(BF16)
    _store_row_slabs(xp_ref, 0, _pack_bf16_pairs(u2))

    u2l = (u2 - u2b.astype(F32)).astype(BF16)
    logits = (lax.dot_general(wrh_ref[...], u2b, NT, preferred_element_type=F32)
              + lax.dot_general(wrh_ref[...], u2l, NT, preferred_element_type=F32)
              + lax.dot_general(wrl_ref[...], u2b, NT, preferred_element_type=F32))
    scores = _sigmoid(logits)
    biased = scores + rbias_ref[...]
    neg = -jnp.inf

    sub = lax.broadcasted_iota(jnp.int32, (GROUP_SIZE, tm), 0)
    gscore = []
    for gi in range(N_GROUPS):
        vg = biased[gi * GROUP_SIZE:(gi + 1) * GROUP_SIZE, :]
        m1 = jnp.max(vg, axis=0, keepdims=True)
        first = jnp.min(jnp.where(vg == m1, sub, GROUP_SIZE), axis=0, keepdims=True)
        m2 = jnp.max(jnp.where(sub == first, neg, vg), axis=0, keepdims=True)
        gscore.append(m1 + m2)
    gs = jnp.concatenate(gscore, axis=0)
    gid = lax.broadcasted_iota(jnp.int32, gs.shape, 0)
    gsel = jnp.zeros(gs.shape, F32)
    for _ in range(TOPK_GROUPS):
        m = jnp.max(gs, axis=0, keepdims=True)
        pick = jnp.min(jnp.where(gs == m, gid, N_GROUPS), axis=0, keepdims=True)
        hit = gid == pick
        gsel = jnp.where(hit, 1.0, gsel)
        gs = jnp.where(hit, neg, gs)
    emask = jnp.concatenate(
        [jnp.broadcast_to(gsel[gi:gi + 1, :], (GROUP_SIZE, tm)) for gi in range(N_GROUPS)], axis=0)
    cand = jnp.where(emask > 0.5, biased, neg)

    eid = lax.broadcasted_iota(jnp.int32, cand.shape, 0)
    chosen = jnp.zeros(cand.shape, F32)
    idx_rows, w_rows, hits = [], [], []
    for _ in range(TOP_K):
        m = jnp.max(cand, axis=0, keepdims=True)
        pick = jnp.min(jnp.where(cand == m, eid, N_EXPERTS), axis=0, keepdims=True)
        hit = eid == pick
        idx_rows.append(pick)
        w_rows.append(jnp.sum(jnp.where(hit, scores, 0.0), axis=0, keepdims=True))
        hits.append(hit)
        chosen = chosen + hit.astype(F32)
        cand = jnp.where(hit, neg, cand)
    wsel = jnp.concatenate(w_rows, axis=0)
    wsel = wsel / (jnp.sum(wsel, axis=0, keepdims=True) + 1e-20) * ROUTED_SCALE
    idx_ref[...] = jnp.concatenate(idx_rows, axis=0)

    ti = lax.broadcasted_iota(jnp.int32, (tm, tm), 0)
    tj = lax.broadcasted_iota(jnp.int32, (tm, tm), 1)
    before = (ti < tj).astype(BF16)
    prefix = jnp.dot(chosen.astype(BF16), before, preferred_element_type=F32) + carry[:, 0:1]
    rank_rows = [jnp.sum(jnp.where(hit, prefix, 0.0), axis=0, keepdims=True) for hit in hits]
    rank_ref[...] = jnp.concatenate(rank_rows, axis=0).astype(jnp.int32)
    carry[...] = carry[...] + jnp.sum(chosen, axis=1, keepdims=True)
    cnt_ref[...] = carry[...]

    wpad = jnp.concatenate([wsel, jnp.zeros((LANE - TOP_K, tm), F32)], axis=0)
    wtok_ref[...] = wpad.T

    gu = jnp.dot(u2b, wsgu_ref[...], preferred_element_type=F32)
    hs = (_silu(gu[:, :D_SHARED]) * gu[:, D_SHARED:]).astype(BF16)
    h1_ref[...] = h1 + jnp.dot(hs, wsd_ref[...], preferred_element_type=F32)


def _post_mix(h, o_gdn, o_lat, wts, tm):
    n, d = h.shape
    tok = lambda w: pl.BlockSpec((tm, w), lambda i: (i, 0))
    tr = lambda r: pl.BlockSpec((r, tm), lambda i: (0, i))
    return pl.pallas_call(
        _post_mix_kernel,
        grid=(n // tm,),
        in_specs=[tok(d), tok(GDN_V_W), tok(MLA_HEADS * KV_LORA), _full(wts['wuv'].shape), _full((d, d)),
                  _full((1, d)), _full((N_EXPERTS, d)), _full((N_EXPERTS, d)), _full((N_EXPERTS, 1)),
                  _full((d, 2 * D_SHARED)), _full((D_SHARED, d))],
        out_specs=[tok(d), pl.BlockSpec((tm * ROW_SLAB, LANE), lambda i: (i, 0)), tr(TOP_K), tr(TOP_K), tok(LANE),
                   _full((N_EXPERTS, LANE))],
        out_shape=(jax.ShapeDtypeStruct((n, d), F32),
                   jax.ShapeDtypeStruct((n * ROW_SLAB, LANE), jnp.uint32),
                   jax.ShapeDtypeStruct((TOP_K, n), jnp.int32),
                   jax.ShapeDtypeStruct((TOP_K, n), jnp.int32),
                   jax.ShapeDtypeStruct((n, LANE), F32),
                   jax.ShapeDtypeStruct((N_EXPERTS, LANE), F32)),
        scratch_shapes=[pltpu.VMEM((N_EXPERTS, LANE), F32)],
        compiler_params=_cparams(("arbitrary",)),
        name="post_mix",
    )(h, o_gdn, o_lat, wts['wuv'], wts['w_out'], wts['g_ffn'], wts['wr_hi'], wts['wr_lo'], wts['rbias'],
      wts['ws_gu'], wts['ws_d'])


def _dest_kernel(idx_ref, rank_ref, starts_ref, o_ref):
    eid = lax.broadcasted_iota(jnp.int32, (N_EXPERTS, idx_ref.shape[1]), 0)
    starts = starts_ref[...]
    rows = []
    for kk in range(TOP_K):
        hit = eid == idx_ref[kk:kk + 1, :]
        rows.append(jnp.sum(jnp.where(hit, starts, 0.0), axis=0, keepdims=True))
    o_ref[...] = jnp.concatenate(rows, axis=0).astype(jnp.int32) + rank_ref[...]


def _dest_rows(idx_t, rank_t, starts, tn):
    k, n = idx_t.shape
    assert n * k < 2 ** 24
    tr = pl.BlockSpec((k, tn), lambda i: (0, i))
    return pl.pallas_call(
        _dest_kernel,
        grid=(n // tn,),
        in_specs=[tr, tr, _full((N_EXPERTS, 1))],
        out_specs=tr,
        out_shape=jax.ShapeDtypeStruct((k, n), jnp.int32),
        compiler_params=_cparams(("arbitrary",)),
        name="dest_rows",
    )(idx_t, rank_t, starts.astype(F32).reshape(N_EXPERTS, 1))


def _slab(ref, row):
    return ref.at[pl.ds(pl.multiple_of(row * ROW_SLAB, ROW_SLAB), ROW_SLAB)]


def _scatter_kernel(dest_ref, x_ref, o_hbm, sem):
    fan, nrow = dest_ref.shape

    def start(r, carry):
        for kk in range(fan):
            pltpu.make_async_copy(_slab(x_ref, r), _slab(o_hbm, dest_ref[kk, r]), sem).start(priority=kk % 2)
        return carry

    lax.fori_loop(0, nrow, start, 0)
    whole = o_hbm.at[pl.ds(0, fan * nrow * ROW_SLAB)]
    pltpu.make_async_copy(whole, whole, sem).wait()


def _scatter_rows(x, dest_t, ts):
    w = x.shape[1]
    n = x.shape[0] // ROW_SLAB
    fan = dest_t.shape[0]
    return pl.pallas_call(
        _scatter_kernel,
        grid=(n // ts,),
        in_specs=[pl.BlockSpec((fan, ts), lambda i: (0, i), memory_space=pltpu.SMEM),
                  pl.BlockSpec((ts * ROW_SLAB, w), lambda i: (i, 0))],
        out_specs=pl.BlockSpec(memory_space=pl.ANY),
        out_shape=jax.ShapeDtypeStruct((n * fan * ROW_SLAB, w), x.dtype),
        scratch_shapes=[pltpu.SemaphoreType.DMA],
        compiler_params=pltpu.CompilerParams(dimension_semantics=("arbitrary",), has_side_effects=True),
        name="scatter_rows",
    )(dest_t, x)


def _experts_kernel(vt_ref, ve_ref, seg_ref, nv_ref, x_ref, wgu_ref, wd_ref, o_ref, *, bm):
    v = pl.program_id(0)
    tile = vt_ref[v]
    e = ve_ref[v]
    prev_tile = vt_ref[jnp.maximum(v - 1, 0)]
    first = (v == 0) | (tile != prev_tile)

    @pl.when(v < nv_ref[0])
    def _():
        w = ROW_SLAB * LANE
        x = _unpack_bf16_pairs(_load_row_slabs(x_ref, 0, bm, w)).astype(BF16)
        gu = jnp.dot(x, wgu_ref[...], preferred_element_type=F32)
        hid = (_silu(gu[:, :D_EXPERT]) * gu[:, D_EXPERT:]).astype(BF16)
        y = _pack_bf16_pairs(jnp.dot(hid, wd_ref[...], preferred_element_type=F32))
        rowid = tile * bm + lax.broadcasted_iota(jnp.int32, y.shape, 0)
        mine = (rowid >= seg_ref[e]) & (rowid < seg_ref[e + 1])

        @pl.when(first)
        def _():
            _store_row_slabs(o_ref, 0, jnp.where(mine, y, jnp.uint32(0)))

        @pl.when(jnp.logical_not(first))
        def _():
            _store_row_slabs(o_ref, 0, jnp.where(mine, y, _load_row_slabs(o_ref, 0, bm, w)))


def _experts(xs, visit_tile, visit_expert, seg, n_visits, wgu, wd, bm):
    rows, w = xs.shape
    nv_max = visit_tile.shape[0]
    d = wd.shape[2]
    grid_spec = pltpu.PrefetchScalarGridSpec(
        num_scalar_prefetch=4,
        grid=(nv_max,),
        in_specs=[pl.BlockSpec((bm * ROW_SLAB, w), lambda v, vt, ve, sg, nv: (vt[v], 0)),
                  pl.BlockSpec((None, d, 2 * D_EXPERT), lambda v, vt, ve, sg, nv: (ve[v], 0, 0)),
                  pl.BlockSpec((None, D_EXPERT, d), lambda v, vt, ve, sg, nv: (ve[v], 0, 0))],
        out_specs=pl.BlockSpec((bm * ROW_SLAB, w), lambda v, vt, ve, sg, nv: (vt[v], 0)),
    )
    return pl.pallas_call(
        functools.partial(_experts_kernel, bm=bm),
        grid_spec=grid_spec,
        out_shape=jax.ShapeDtypeStruct((rows, w), jnp.uint32),
        compiler_params=_cparams(("arbitrary",)),
        name="experts",
    )(visit_tile, visit_expert, seg, n_visits, xs, wgu, wd)


def _visit_list(counts, rows, bm):
    ntiles = rows // bm
    nv_max = ntiles + N_EXPERTS - 1
    ends = jnp.cumsum(counts)
    starts = ends - counts
    first_tile = starts // bm
    ntile_e = jnp.where(counts > 0, (ends - 1) // bm - first_tile + 1, 0)
    vend = jnp.cumsum(ntile_e)
    vstart = vend - ntile_e
    n_visits = vend[-1]
    v = jnp.arange(nv_max, dtype=jnp.int32)
    ve = jnp.minimum(jnp.sum(vend[None, :] <= v[:, None], axis=1), N_EXPERTS - 1).astype(jnp.int32)
    vt = (first_tile[ve] + v - vstart[ve]).astype(jnp.int32)
    last = jnp.maximum(n_visits - 1, 0)
    ve = jnp.where(v < n_visits, ve, ve[last])
    vt = jnp.where(v < n_visits, vt, vt[last])
    seg = jnp.concatenate([starts, ends[-1:]]).astype(jnp.int32)
    return vt, ve, seg, n_visits.reshape(1).astype(jnp.int32), starts


def _final_kernel(dcur_ref, dnext_ref, h1_ref, wtok_ref, p_ref, gple_ref, wgate_ref, wproj_ref, gfin_ref,
                  ys_hbm, o_ref, ybuf, sem):
    i = pl.program_id(0)
    nsteps = pl.num_programs(0)
    tm = h1_ref.shape[0]
    slot = i % 2

    def fetch(dest_ref, into):
        def body(r, carry):
            for kk in range(TOP_K):
                pltpu.make_async_copy(_slab(ys_hbm, dest_ref[kk, r]), _slab(ybuf.at[into], kk * tm + r),
                                      sem.at[into]).start(priority=kk % 2)
            return carry
        lax.fori_loop(0, tm, body, 0)

    @pl.when(i == 0)
    def _():
        fetch(dcur_ref, 0)

    @pl.when(i + 1 < nsteps)
    def _():
        fetch(dnext_ref, 1 - slot)

    pltpu.make_async_copy(ys_hbm.at[pl.ds(0, TOP_K * tm * ROW_SLAB)], ybuf.at[slot], sem.at[slot]).wait()

    wtok = wtok_ref[...]
    h2 = h1_ref[...]
    for kk in range(TOP_K):
        yk = _unpack_bf16_pairs(_load_row_slabs(ybuf.at[slot], kk * tm, tm, ROW_SLAB * LANE))
        h2 = h2 + yk * wtok[:, kk:kk + 1]
    n = _rms(h2, gple_ref[...]).astype(BF16)
    gate = _sigmoid(jnp.dot(n, wgate_ref[...], preferred_element_type=F32))
    pp = jnp.dot(p_ref[...].astype(BF16), wproj_ref[...], preferred_element_type=F32)
    o_ref[...] = _rms(h2 + gate * pp, gfin_ref[...])


def _final(h1, ys, dest_t, wtok, p, wts, tm):
    n, d = h1.shape
    nsteps = n // tm
    assert ys.shape[0] >= TOP_K * tm * ROW_SLAB
    tok = lambda w: pl.BlockSpec((tm, w), lambda i: (i, 0))
    dspec = lambda f: pl.BlockSpec((TOP_K, tm), f, memory_space=pltpu.SMEM)
    return pl.pallas_call(
        _final_kernel,
        grid=(nsteps,),
        in_specs=[dspec(lambda i: (0, i)), dspec(lambda i: (0, jnp.minimum(i + 1, nsteps - 1))),
                  tok(d), tok(LANE), tok(p.shape[1]), _full((1, d)), _full((d, d)),
                  _full((p.shape[1], d)), _full((1, d)), pl.BlockSpec(memory_space=pl.ANY)],
        out_specs=tok(d),
        out_shape=jax.ShapeDtypeStruct((n, d), F32),
        scratch_shapes=[pltpu.VMEM((2, TOP_K * tm * ROW_SLAB, ys.shape[1]), ys.dtype),
                        pltpu.SemaphoreType.DMA((2,))],
        compiler_params=_cparams(("arbitrary",)),
        name="final",
    )(dest_t, dest_t, h1, wtok, p, wts['g_ple'], wts['w_ple_gate'], wts['w_ple_proj'], wts['g_final'], ys)


def _rotate_half_cols(w):
    half = QK_ROPE // 2
    wh = w.reshape(w.shape[0], -1, 2, half)
    return jnp.concatenate([-wh[:, :, 1:2], wh[:, :, 0:1]], axis=2).reshape(w.shape)


def _prep_weights(g_mix, w_in, gdn_conv_w, gdn_a_log, gdn_dt_bias, gdn_norm, mla_g_q, mla_w_uq, mla_g_kv,
                  mla_w_uk, mla_w_uv, w_out, g_ffn, w_router, router_bias, w_exp_gate, w_exp_up, w_exp_down,
                  w_sh_gate, w_sh_up, w_sh_down, g_ple, w_ple_gate, w_ple_proj, g_final):
    d = w_in.shape[0]
    splits = np.cumsum([CONV_CH, GDN_V_W, GDN_HEADS, GDN_HEADS, Q_LORA, KV_LORA])
    w_qkv, w_z, w_b, w_a, w_cq, w_ckv, w_kr = jnp.split(w_in, [int(s) for s in splits], axis=1)
    ba_pad = jnp.zeros((d, LANE - 2 * GDN_HEADS), w_in.dtype)
    w1 = jnp.concatenate([w_qkv, w_z, w_cq, w_ckv, w_kr, _rotate_half_cols(w_kr), w_b, w_a, ba_pad], axis=1)
    assert w1.shape[1] == W1_WIDTH
    lane_pad = lambda v: jnp.pad(v.astype(F32), (GDN_HEADS, LANE - 2 * GDN_HEADS)).reshape(1, LANE)
    uq = mla_w_uq.reshape(Q_LORA, MLA_HEADS, QK_NOPE + QK_ROPE)
    uq_nope = uq[:, :, :QK_NOPE].reshape(Q_LORA, MLA_HEADS * QK_NOPE)
    uq_rope = uq[:, :, QK_NOPE:].reshape(Q_LORA, MLA_HEADS * QK_ROPE)
    wuq = jnp.concatenate([uq_nope, uq_rope, _rotate_half_cols(uq_rope)], axis=1)
    wr_t = w_router.T.astype(F32)
    wr_hi = wr_t.astype(BF16)
    row = lambda v: v.astype(F32).reshape(1, -1)
    return dict(
        g_mix=row(g_mix), w1=w1.astype(BF16), alog=lane_pad(gdn_a_log), dtb=lane_pad(gdn_dt_bias),
        g_q=row(mla_g_q), wuq=wuq.astype(BF16), wuk=jnp.transpose(mla_w_uk, (1, 2, 0)).astype(BF16),
        g_kv=row(mla_g_kv),
        conv_w=jnp.pad(gdn_conv_w.astype(F32), ((0, SUBLANE - CONV_W), (0, 0))), gdn_norm=row(gdn_norm),
        wuv=jnp.transpose(mla_w_uv, (1, 0, 2)).astype(BF16), w_out=w_out.astype(BF16), g_ffn=row(g_ffn),
        wr_hi=wr_hi, wr_lo=(wr_t - wr_hi.astype(F32)).astype(BF16),
        rbias=router_bias.astype(F32).reshape(N_EXPERTS, 1),
        ws_gu=jnp.concatenate([w_sh_gate, w_sh_up], axis=1).astype(BF16), ws_d=w_sh_down.astype(BF16),
        we_gu=jnp.concatenate([w_exp_gate, w_exp_up], axis=2).astype(BF16), we_d=w_exp_down.astype(BF16),
        g_ple=row(g_ple), w_ple_gate=w_ple_gate.astype(BF16), w_ple_proj=w_ple_proj.astype(BF16),
        g_final=row(g_final),
    )


def _rope_tables(pos, reps):
    half = QK_ROPE // 2
    inv = ROPE_THETA ** (-jnp.arange(half, dtype=F32) / half)
    ang = pos.astype(F32)[:, None] * inv[None, :]
    cos = jnp.concatenate([jnp.cos(ang)] * 2, axis=1)
    sin = jnp.concatenate([jnp.sin(ang)] * 2, axis=1)
    cos = jnp.tile(jnp.tile(cos, (1, MLA_HEADS)), (reps, 1))
    sin = jnp.tile(jnp.tile(sin, (1, MLA_HEADS)), (reps, 1))
    return cos, sin


def _tile(n, pref):
    t = min(n, pref)
    assert n % t == 0
    return t


def _layer(x, p, pos, gdn_state, conv_buf, attend, wts, chunk):
    b, t, d = x.shape
    n = b * t
    tm = _tile(n, 256)
    if t >= tm:
        assert t % tm == 0
        cos, sin = _rope_tables(pos, 1)
    else:
        assert tm % t == 0
        cos, sin = _rope_tables(pos, tm // t)
    x2 = x.reshape(n, d)
    qkv, z, gb, ckv, krope, kv, q = _in_proj(x2, wts, cos, sin, tm)

    conv0 = jnp.pad(conv_buf.astype(F32), ((0, 0), (SUBLANE - (CONV_W - 1), 0), (0, 0)))
    o_gdn, gdn_new = _gdn(qkv.reshape(b, t, CONV_CH), z.reshape(b, t, GDN_V_W), gb.reshape(b, t, LANE),
                          gdn_state.astype(F32), conv0, wts['conv_w'], wts['gdn_norm'], chunk)
    conv_new = qkv.reshape(b, t, CONV_CH)[:, t - (CONV_W - 1):, :]

    o_lat = attend(q.reshape(b, t, MLA_HEADS * QK_PAD), kv.reshape(b, t, QK_PAD))

    h1, xp, idx_t, rank_t, wtok, counts = _post_mix(x2, o_gdn.reshape(n, GDN_V_W),
                                                    o_lat.reshape(n, MLA_HEADS * KV_LORA), wts, tm)
    rows = n * TOP_K
    bm = _tile(rows, 512)
    cnt = counts[:, 0].astype(jnp.int32)
    vt, ve, seg, n_visits, starts = _visit_list(cnt, rows, bm)
    dest_t = _dest_rows(idx_t, rank_t, starts, _tile(n, 2048))
    xs = _scatter_rows(xp, dest_t, tm)
    ys = _experts(xs, vt, ve, seg, n_visits, wts['we_gu'], wts['we_d'], bm)
    y = _final(h1, ys, dest_t, wtok, p.reshape(n, -1), wts, tm)
    return y.reshape(b, t, d), ckv.reshape(b, t, KV_LORA), krope.reshape(b, t, QK_ROPE), gdn_new, conv_new


def kernel(x_prompt, x_sample, p_prompt, p_sample, cache_ckv, cache_krope, state_gdn, state_conv, page_table,
           g_mix, w_in, gdn_conv_w, gdn_a_log, gdn_dt_bias, gdn_norm, mla_g_q, mla_w_uq, mla_g_kv, mla_w_uk,
           mla_w_uv, w_out, g_ffn, w_router, router_bias, w_exp_gate, w_exp_up, w_exp_down, w_sh_gate,
           w_sh_up, w_sh_down, g_ple, w_ple_gate, w_ple_proj, g_final):
    depth = w_in.shape[0]
    assert depth == 1, "the final norm is fused into the (single) layer"
    bp, tp, _ = x_prompt.shape
    bs, ts, _ = x_sample.shape
    past = page_table.shape[1] * cache_ckv.shape[2]
    wts = _prep_weights(g_mix[0], w_in[0], gdn_conv_w[0], gdn_a_log[0], gdn_dt_bias[0], gdn_norm[0], mla_g_q[0],
                        mla_w_uq[0], mla_g_kv[0], mla_w_uk[0], mla_w_uv[0], w_out[0], g_ffn[0], w_router[0],
                        router_bias[0], w_exp_gate[0], w_exp_up[0], w_exp_down[0], w_sh_gate[0], w_sh_up[0],
                        w_sh_down[0], g_ple[0], w_ple_gate[0], w_ple_proj[0], g_final)

    cache_krope_t = jnp.swapaxes(cache_krope[0], 1, 2)

    def attend_s(q, kv):
        return _paged_attn(q, kv, cache_ckv[0], cache_krope_t, page_table,
                           pages_per_step=_tile(page_table.shape[1], 16))

    ys, c2, k2, s2, v2 = _layer(x_sample, p_sample[0], past + jnp.arange(ts), state_gdn[0], state_conv[0],
                                attend_s, wts, chunk=ts)

    gdn0 = jnp.zeros((bp, GDN_HEADS, GDN_DK, GDN_DV), state_gdn.dtype)
    conv0 = jnp.zeros((bp, CONV_W - 1, CONV_CH), state_conv.dtype)
    attend_p = functools.partial(_prompt_attn, tq=_tile(tp, 256), tk=_tile(tp, 512))
    yp, c1, k1, s1, v1 = _layer(x_prompt, p_prompt[0], jnp.arange(tp), gdn0, conv0, attend_p, wts,
                                chunk=_tile(tp, 64))
    st = lambda a, ref: a.astype(ref.dtype)[None]
    return (yp, ys, c1[None], k1[None], st(s1, state_gdn), st(v1, state_conv),
            c2[None], k2[None], st(s2, state_gdn), st(v2, state_conv))
```

```python
import functools

import jax
import jax.numpy as jnp
import numpy as np
from jax import lax
from jax.experimental import pallas as pl
from jax.experimental.pallas import tpu as pltpu

EPS = 1e-6
ROPE_THETA = 10000.0
GDN_HEADS = 4
GDN_DK = 128
GDN_DV = 128
CONV_W = 4
MLA_HEADS = 4
Q_LORA = 256
KV_LORA = 256
QK_NOPE = 128
QK_ROPE = 64
V_DIM = 128
N_EXPERTS = 64
TOP_K = 8
N_GROUPS = 8
TOPK_GROUPS = 4
GROUP_SIZE = N_EXPERTS // N_GROUPS
D_EXPERT = 256
D_SHARED = 256
ROUTED_SCALE = 2.5
MLA_SCALE = (QK_NOPE + QK_ROPE) ** -0.5

GDN_QK_W = GDN_HEADS * GDN_DK
GDN_V_W = GDN_HEADS * GDN_DV
CONV_CH = 2 * GDN_QK_W + GDN_V_W
LANE = 128
SUBLANE = 8
QK_PAD = KV_LORA + LANE
VMEM_LIMIT = 56 * 1024 * 1024
D_MODEL = 1024
ROW_SLAB = D_MODEL // 2 // LANE

C_QKV = 0
C_Z = C_QKV + CONV_CH
C_CQ = C_Z + GDN_V_W
C_CKV = C_CQ + Q_LORA
C_KR = C_CKV + KV_LORA
C_KRR = C_KR + QK_ROPE
C_BA = C_KRR + QK_ROPE
W1_WIDTH = C_BA + LANE

F32 = jnp.float32
BF16 = jnp.bfloat16
NT = (((1,), (1,)), ((), ()))
TN = (((0,), (0,)), ((), ()))


def _cparams(sem, **flags):
    return pltpu.CompilerParams(dimension_semantics=sem, vmem_limit_bytes=VMEM_LIMIT, flags=flags or None)


def _rms(x, g):
    return x * lax.rsqrt(jnp.mean(x * x, axis=-1, keepdims=True) + EPS) * g


def _sigmoid(x):
    return 1.0 / (1.0 + jnp.exp(-x))


def _silu(x):
    return x * _sigmoid(x)


def _full(shape):
    return pl.BlockSpec(shape, lambda *_: (0,) * len(shape))


def _in_proj_kernel(x_ref, gmix_ref, w1_ref, alog_ref, dtb_ref, gq_ref, wuq_ref, wuk_ref, gkv_ref,
                    cos_ref, sin_ref,
                    qkv_ref, z_ref, gb_ref, ckv_ref, krope_ref, kv_ref, q_ref):
    u = _rms(x_ref[...], gmix_ref[...]).astype(BF16)
    qkv_ref[...] = jnp.dot(u, w1_ref[:, C_QKV:C_Z], preferred_element_type=F32)
    z_ref[...] = jnp.dot(u, w1_ref[:, C_Z:C_CQ], preferred_element_type=F32)
    rest = jnp.dot(u, w1_ref[:, C_CQ:W1_WIDTH], preferred_element_type=F32)
    c_q = rest[:, 0:Q_LORA]
    c_kv = rest[:, C_CKV - C_CQ:C_KR - C_CQ]
    kr2 = rest[:, C_KR - C_CQ:C_BA - C_CQ]
    ba = rest[:, C_BA - C_CQ:]

    lane = lax.broadcasted_iota(jnp.int32, ba.shape, 1)
    beta = _sigmoid(ba)
    sp_in = ba + dtb_ref[...]
    softplus = jnp.maximum(sp_in, 0.0) + jnp.log(1.0 + jnp.exp(-jnp.abs(sp_in)))
    g = -jnp.exp(alog_ref[...]) * softplus
    gb_ref[...] = jnp.where(lane < GDN_HEADS, beta, g)

    cos = cos_ref[...]
    sin = sin_ref[...]
    ckv = _rms(c_kv, gkv_ref[...])
    ckv_ref[...] = ckv
    krope = kr2[:, :QK_ROPE] * cos[:, :QK_ROPE] + kr2[:, QK_ROPE:] * sin[:, :QK_ROPE]
    krope_ref[...] = krope
    zpad = jnp.zeros((ckv.shape[0], LANE - QK_ROPE), F32)
    kv_ref[...] = jnp.concatenate([ckv, krope, zpad], axis=1).astype(BF16)

    cqn = _rms(c_q, gq_ref[...]).astype(BF16)
    qh = jnp.dot(cqn, wuq_ref[...], preferred_element_type=F32)
    nope_w = MLA_HEADS * QK_NOPE
    rope_w = MLA_HEADS * QK_ROPE
    q_rope = qh[:, nope_w:nope_w + rope_w] * cos + qh[:, nope_w + rope_w:] * sin
    parts = []
    for h in range(MLA_HEADS):
        q_nope = qh[:, h * QK_NOPE:(h + 1) * QK_NOPE].astype(BF16)
        q_lat = jnp.dot(q_nope, wuk_ref[h], preferred_element_type=F32)
        parts += [q_lat, q_rope[:, h * QK_ROPE:(h + 1) * QK_ROPE], zpad]
    q_ref[...] = (jnp.concatenate(parts, axis=1) * MLA_SCALE).astype(BF16)


def _in_proj(x, wts, cos, sin, tm):
    n, d = x.shape
    period = cos.shape[0]
    nper = period // tm
    tok = lambda w: pl.BlockSpec((tm, w), lambda i: (i, 0))
    pos = pl.BlockSpec((tm, cos.shape[1]), lambda i: (i % nper, 0))
    out_shapes = (
        jax.ShapeDtypeStruct((n, CONV_CH), F32),
        jax.ShapeDtypeStruct((n, GDN_V_W), F32),
        jax.ShapeDtypeStruct((n, LANE), F32),
        jax.ShapeDtypeStruct((n, KV_LORA), F32),
        jax.ShapeDtypeStruct((n, QK_ROPE), F32),
        jax.ShapeDtypeStruct((n, QK_PAD), BF16),
        jax.ShapeDtypeStruct((n, MLA_HEADS * QK_PAD), BF16),
    )
    return pl.pallas_call(
        _in_proj_kernel,
        grid=(n // tm,),
        in_specs=[tok(d), _full((1, d)), _full(wts['w1'].shape), _full((1, LANE)), _full((1, LANE)),
                  _full((1, Q_LORA)), _full(wts['wuq'].shape), _full(wts['wuk'].shape), _full((1, KV_LORA)),
                  pos, pos],
        out_specs=[tok(CONV_CH), tok(GDN_V_W), tok(LANE), tok(KV_LORA), tok(QK_ROPE), tok(QK_PAD),
                   tok(MLA_HEADS * QK_PAD)],
        out_shape=out_shapes,
        compiler_params=_cparams(("arbitrary",)),
        name="in_proj",
    )(x, wts['g_mix'], wts['w1'], wts['alog'], wts['dtb'], wts['g_q'], wts['wuq'], wts['wuk'], wts['g_kv'],
      cos, sin)


def _gdn_kernel(qkv_ref, z_ref, gb_ref, s0_ref, conv0_ref, *rest, chunk):
    consts = rest[:-4]
    o_ref, sout_ref, s_scr, xbuf = rest[-4:]
    c = pl.program_id(1)

    @pl.when(c == 0)
    def _():
        s_scr[...] = s0_ref[...]
        xbuf[:, 0:SUBLANE, :] = conv0_ref[...]

    cw_ref, gn_ref = consts[0], consts[1]
    nb, rows = qkv_ref.shape[0], qkv_ref.shape[1]
    chains = {}
    for bi in range(nb):
        y = _gdn_conv(qkv_ref.at[bi], cw_ref, xbuf.at[bi])
        for ci in range(rows // chunk):
            tok = slice(ci * chunk, (ci + 1) * chunk)
            chains[bi, ci] = _gdn_chunk_prep(y[tok], gb_ref[bi, tok, :], *consts[2:], chunk=chunk)
    prepped = dict(zip(chains, _round_robin(list(chains.values()))))
    for ci in range(rows // chunk):
        for bi in range(nb):
            tok = slice(ci * chunk, (ci + 1) * chunk)
            out = _gdn_chunk_state(prepped[bi, ci], z_ref[bi, tok, :], gn_ref, consts[-1], s_scr.at[bi], chunk=chunk)
            for h in range(GDN_HEADS):
                o_ref[bi, tok, h * GDN_DV:(h + 1) * GDN_DV] = out[h * chunk:(h + 1) * chunk, :].astype(o_ref.dtype)

    @pl.when(c == pl.num_programs(1) - 1)
    def _():
        sout_ref[...] = s_scr[...]


def _gdn_conv(qkv_ref, cw_ref, xbuf):
    pre = SUBLANE
    tail = CONV_W - 1
    rows = qkv_ref.shape[0]
    xbuf[pre:pre + rows, :] = qkv_ref[...]
    y = xbuf[pre - tail:pre - tail + rows, :] * cw_ref[0:1, :]
    for j in range(1, CONV_W):
        y = y + xbuf[pre - tail + j:pre - tail + j + rows, :] * cw_ref[j:j + 1, :]
    xbuf[pre - tail:pre, :] = xbuf[pre + rows - tail:pre + rows, :]
    return _silu(y)


def _stack_gdn_heads(a, off):
    return jnp.concatenate([a[:, off + h * GDN_DK:off + (h + 1) * GDN_DK] for h in range(GDN_HEADS)], axis=0)


def _round_robin(gens):
    results = [None] * len(gens)
    live = list(range(len(gens)))
    while live:
        for i in list(live):
            try:
                next(gens[i])
            except StopIteration as done:
                results[i] = done.value
                live.remove(i)
    return results


def _gdn_chunk_prep(y, gb, cmat_ref, gsel_ref, ssel_ref, bsel_ref, place_ref, triu_ref, mask_ref, hsel_ref, *, chunk):
    stack = _stack_gdn_heads
    q = stack(y, 0)
    k = stack(y, GDN_QK_W)
    v = stack(y, 2 * GDN_QK_W)
    qn = q * lax.rsqrt(jnp.sum(q * q, axis=-1, keepdims=True) + EPS) * (GDN_DK ** -0.5)
    kn = k * lax.rsqrt(jnp.sum(k * k, axis=-1, keepdims=True) + EPS)

    r = GDN_HEADS * chunk
    lane_of = lambda a, sel: jnp.sum(a * sel, axis=-1, keepdims=True)
    gb_hi = gb.astype(BF16)
    gb_r1 = gb - gb_hi.astype(F32)
    gb_mid = gb_r1.astype(BF16)
    gb_lo = (gb_r1 - gb_mid.astype(F32)).astype(BF16)
    pieces = (gb_hi, gb_mid, gb_lo)
    cums = sum(jnp.dot(cmat_ref[...], p, preferred_element_type=F32) for p in pieces)
    total = jnp.sum(gb, axis=0, keepdims=True)
    gsel = gsel_ref[...]
    gcol = lane_of(cums, gsel)
    glast = lane_of(total, gsel)
    gstate = lane_of(total, ssel_ref[...])
    beta = lane_of(jnp.concatenate([gb] * GDN_HEADS, axis=0), bsel_ref[...])
    spread = sum(jnp.dot(p, place_ref[...], preferred_element_type=F32) for p in pieces)
    grow = jnp.sum(spread * triu_ref[...], axis=0, keepdims=True)

    incl = mask_ref[0]
    decay = jnp.exp(jnp.where(incl > 0.5, gcol - grow, -jnp.inf))
    kb = kn * beta
    knb = kn.astype(BF16)
    kk = lax.dot_general(kb.astype(BF16), knb, NT, preferred_element_type=F32)
    yield
    lower = kk * decay * mask_ref[1]
    inv = mask_ref[2] - lower * mask_ref[3]
    for lvl in range(4, mask_ref.shape[0]):
        ih = inv.astype(BF16)
        il = (inv - ih.astype(F32)).astype(BF16)
        blk = (lower * mask_ref[lvl]).astype(BF16)
        t2 = jnp.dot(jnp.concatenate([ih, il], axis=0), blk, preferred_element_type=F32)
        yield
        t = t2[:r] + t2[r:]
        th = t.astype(BF16)
        tl = (t - th.astype(F32)).astype(BF16)
        c2 = jnp.dot(jnp.concatenate([th, tl], axis=0), ih, preferred_element_type=F32)
        c1 = jnp.dot(th, il, preferred_element_type=F32)
        yield
        inv = inv - (c2[:r] + c2[r:] + c1)
    egc = jnp.exp(gcol)
    ih = inv.astype(BF16)
    il = (inv - ih.astype(F32)).astype(BF16)
    rhs = jnp.concatenate([v * beta, kb * egc], axis=1)
    rh = rhs.astype(BF16)
    rl = (rhs - rh.astype(F32)).astype(BF16)
    uw2 = jnp.dot(jnp.concatenate([ih, il], axis=0), rh, preferred_element_type=F32)
    uw1 = jnp.dot(ih, rl, preferred_element_type=F32)
    yield
    uw = uw2[:r] + uw2[r:] + uw1
    u = uw[:, :GDN_DV]
    w = uw[:, GDN_DV:]
    qk = lax.dot_general(qn.astype(BF16), knb, NT, preferred_element_type=F32) * decay

    hsel = hsel_ref[...]
    wide = lambda a: (jnp.concatenate([a] * GDN_HEADS, axis=1) * hsel).astype(BF16)
    return dict(u=u, w_qe=jnp.concatenate([wide(w), wide(qn * egc)], axis=0), qk=qk.astype(BF16),
                kdec=wide(kn * jnp.exp(glast - gcol)), sdecay=jnp.exp(gstate))


def _gdn_chunk_state(p, z, gn_ref, hsel_ref, s_scr, *, chunk):
    r = GDN_HEADS * chunk
    st = s_scr[...]
    ws_qs = jnp.dot(p['w_qe'], st.astype(BF16), preferred_element_type=F32)
    v_new = (p['u'] - ws_qs[:r]).astype(BF16)
    o = ws_qs[r:] + jnp.dot(p['qk'], v_new, preferred_element_type=F32)
    s_scr[...] = st * p['sdecay'] + lax.dot_general(p['kdec'], v_new, TN, preferred_element_type=F32)
    return _rms(o, gn_ref[...]) * _silu(_stack_gdn_heads(z, 0))


def _gdn_constants(chunk):
    r = GDN_HEADS * chunk
    ri = np.arange(r)
    head, tok = ri // chunk, ri % chunk
    ti = np.arange(chunk)
    lane = np.arange(LANE)
    srow = np.arange(GDN_HEADS * GDN_DK)
    cmat = ti[None, :] <= tok[:, None]
    gsel = lane[None, :] == GDN_HEADS + head[:, None]
    ssel = lane[None, :] == GDN_HEADS + (srow // GDN_DK)[:, None]
    bsel = lane[None, :] == head[:, None]
    place = lane[:, None] == GDN_HEADS + head[None, :]
    triu = ti[:, None] <= tok[None, :]
    same = head[:, None] == head[None, :]
    masks = [same & (tok[:, None] >= tok[None, :]), same & (tok[:, None] > tok[None, :]),
             ri[:, None] == ri[None, :]]
    s = 1
    while s < chunk:
        masks.append(same & ((tok[:, None] // s) % 2 == 1) & ((tok[None, :] // s) == (tok[:, None] // s) - 1))
        s *= 2
    hsel = (np.arange(GDN_HEADS * GDN_DK)[None, :] // GDN_DK) == head[:, None]
    f = lambda a: jnp.asarray(np.asarray(a, np.float32))
    fb = lambda a: f(a).astype(BF16)
    return fb(cmat), f(gsel), f(ssel), f(bsel), fb(place), f(triu), f(np.stack(masks)), f(hsel)


def _gdn(qkv, z, gb, s0, conv0, cw, gn, chunk):
    b, t, _ = qkv.shape
    assert t % chunk == 0 and chunk % SUBLANE == 0
    nb = 2 if b % 2 == 0 else 1
    nch = 2 if (t // chunk) % 2 == 0 else 1
    rows = nch * chunk
    consts = _gdn_constants(chunk)
    tokb = lambda w: pl.BlockSpec((nb, rows, w), lambda i, c: (i, c, 0))
    state = pl.BlockSpec((nb, GDN_HEADS * GDN_DK, GDN_DV), lambda i, c: (i, 0, 0))
    o, s_new = pl.pallas_call(
        functools.partial(_gdn_kernel, chunk=chunk),
        grid=(b // nb, t // rows),
        in_specs=[tokb(CONV_CH), tokb(GDN_V_W), tokb(LANE), state,
                  pl.BlockSpec((nb, SUBLANE, CONV_CH), lambda i, c: (i, 0, 0)),
                  _full((SUBLANE, CONV_CH)), _full((1, GDN_DV))] + [_full(a.shape) for a in consts],
        out_specs=[tokb(GDN_V_W), state],
        out_shape=(jax.ShapeDtypeStruct((b, t, GDN_V_W), BF16),
                   jax.ShapeDtypeStruct((b, GDN_HEADS * GDN_DK, GDN_DV), F32)),
        scratch_shapes=[pltpu.VMEM((nb, GDN_HEADS * GDN_DK, GDN_DV), F32),
                        pltpu.VMEM((nb, SUBLANE + rows, CONV_CH), F32)],
        compiler_params=_cparams(("arbitrary", "arbitrary")),
        name="gdn",
    )(qkv, z, gb, s0.reshape(b, GDN_HEADS * GDN_DK, GDN_DV), conv0, cw, gn, *consts)
    return o, s_new.reshape(b, GDN_HEADS, GDN_DK, GDN_DV)


def _stack_heads(q):
    return jnp.concatenate([q[:, h * QK_PAD:(h + 1) * QK_PAD] for h in range(MLA_HEADS)], axis=0)


def _row_reduce(x, combine, reduce):
    w = x.shape[1]
    if w > LANE and w % LANE == 0:
        tiles = [x[:, i * LANE:(i + 1) * LANE] for i in range(w // LANE)]
        while len(tiles) > 1:
            tiles = [combine(tiles[i], tiles[i + 1]) if i + 1 < len(tiles) else tiles[i]
                     for i in range(0, len(tiles), 2)]
        x = tiles[0]
    return reduce(x, axis=-1, keepdims=True)


def _online_softmax_step(s, vals, m_scr, l_scr, acc_scr):
    m_prev = m_scr[...]
    m_new = jnp.maximum(m_prev, jnp.max(s, axis=-1, keepdims=True))
    alpha = jnp.exp(m_prev - m_new)
    p = jnp.exp(s - m_new)
    l_scr[...] = alpha * l_scr[...] + jnp.sum(p, axis=-1, keepdims=True)
    pb = p.astype(BF16)
    if isinstance(vals, (list, tuple)):
        w = s.shape[1] // len(vals)
        pv = sum(jnp.dot(pb[:, i * w:(i + 1) * w], vi, preferred_element_type=F32) for i, vi in enumerate(vals))
    else:
        pv = jnp.dot(pb, vals, preferred_element_type=F32)
    acc_scr[...] = alpha * acc_scr[...] + pv
    m_scr[...] = m_new


def _unstack_store(o_ref, acc_scr, l_scr, rows):
    o = acc_scr[...] / l_scr[...]
    for h in range(MLA_HEADS):
        o_ref[:, h * KV_LORA:(h + 1) * KV_LORA] = o[h * rows:(h + 1) * rows, :].astype(o_ref.dtype)


def _prompt_attn_kernel(q_ref, kv_ref, o_ref, qs_scr, m_scr, l_scr, acc_scr, *, tq, tk):
    i = pl.program_id(1)
    j = pl.program_id(2)
    nj = pl.num_programs(2)

    @pl.when(j == 0)
    def _():
        qs_scr[...] = _stack_heads(q_ref[...])
        m_scr[...] = jnp.full(m_scr.shape, -jnp.inf, F32)
        l_scr[...] = jnp.zeros(l_scr.shape, F32)
        acc_scr[...] = jnp.zeros(acc_scr.shape, F32)

    needed = j * tk <= i * tq + tq - 1
    unmasked = j * tk + tk - 1 <= i * tq
    nchain = 2
    rows = MLA_HEADS * tq // nchain

    def chain(c, masked):
        rs = pl.ds(c * rows, rows)
        kv = kv_ref[...]
        s = lax.dot_general(qs_scr[rs, :], kv, NT, preferred_element_type=F32)
        yield
        if masked:
            qpos = i * tq + lax.broadcasted_iota(jnp.int32, s.shape, 0) % tq
            kpos = j * tk + lax.broadcasted_iota(jnp.int32, s.shape, 1)
            s = jnp.where(kpos <= qpos, s, -jnp.inf)
        m_prev = m_scr[rs, :]
        m_new = jnp.maximum(m_prev, _row_reduce(s, jnp.maximum, jnp.max))
        alpha = jnp.exp(m_prev - m_new)
        p = jnp.exp(s - m_new)
        l_scr[rs, :] = alpha * l_scr[rs, :] + _row_reduce(p, jnp.add, jnp.sum)
        m_scr[rs, :] = m_new
        pv = jnp.dot(p.astype(BF16), kv[:, :KV_LORA], preferred_element_type=F32)
        yield
        acc_scr[rs, :] = alpha * acc_scr[rs, :] + pv

    @pl.when(unmasked)
    def _():
        _round_robin([chain(c, False) for c in range(nchain)])

    @pl.when(needed & jnp.logical_not(unmasked))
    def _():
        _round_robin([chain(c, True) for c in range(nchain)])

    @pl.when(j == nj - 1)
    def _():
        _unstack_store(o_ref, acc_scr, l_scr, tq)


def _prompt_attn(q, kv, tq, tk):
    b, s, _ = q.shape
    rows = MLA_HEADS * tq
    last = lambda i: (i * tq + tq - 1) // tk
    return pl.pallas_call(
        functools.partial(_prompt_attn_kernel, tq=tq, tk=tk),
        grid=(b, s // tq, s // tk),
        in_specs=[pl.BlockSpec((None, tq, MLA_HEADS * QK_PAD), lambda bb, i, j: (bb, i, 0)),
                  pl.BlockSpec((None, tk, QK_PAD), lambda bb, i, j: (bb, jnp.minimum(j, last(i)), 0))],
        out_specs=pl.BlockSpec((None, tq, MLA_HEADS * KV_LORA), lambda bb, i, j: (bb, i, 0)),
        out_shape=jax.ShapeDtypeStruct((b, s, MLA_HEADS * KV_LORA), BF16),
        scratch_shapes=[pltpu.VMEM((rows, QK_PAD), BF16), pltpu.VMEM((rows, 1), F32),
                        pltpu.VMEM((rows, 1), F32), pltpu.VMEM((rows, KV_LORA), F32)],
        compiler_params=_cparams(("arbitrary", "arbitrary", "arbitrary")),
        name="prompt_attn",
    )(q, kv)


def _paged_pages_chain(qs_scr, ckv_refs, kr_refs, m_scr, l_scr, acc_scr):
    qs = qs_scr[...]
    q_lat = qs[:, :KV_LORA]
    q_rope = qs[:, KV_LORA:KV_LORA + QK_ROPE]
    pages, scores = [], []
    for ckv_ref, kr_ref in zip(ckv_refs, kr_refs):
        ckv = ckv_ref[...].astype(BF16)
        kr_t = kr_ref[...].astype(BF16)
        pages.append(ckv)
        scores.append(lax.dot_general(q_lat, ckv, NT, preferred_element_type=F32)
                      + jnp.dot(q_rope, kr_t, preferred_element_type=F32))
    yield
    s = jnp.concatenate(scores, axis=1)
    m_prev = m_scr[...]
    m_new = jnp.maximum(m_prev, _row_reduce(s, jnp.maximum, jnp.max))
    alpha = jnp.exp(m_prev - m_new)
    p = jnp.exp(s - m_new)
    l_scr[...] = alpha * l_scr[...] + _row_reduce(p, jnp.add, jnp.sum)
    m_scr[...] = m_new
    pb = p.astype(BF16)
    w = s.shape[1] // len(pages)
    pv = sum(jnp.dot(pb[:, i * w:(i + 1) * w], vi, preferred_element_type=F32) for i, vi in enumerate(pages))
    yield
    acc_scr[...] = alpha * acc_scr[...] + pv


def _paged_attn_kernel(pt_ref, q_ref, kvnew_ref, ckv_hbm, kr_hbm, o_ref,
                       qs_scr, m_scr, l_scr, acc_scr, ckv_buf, kr_buf, sem, *, pages_per_step, n_pages, t):
    nb = q_ref.shape[0]
    npg = nb * pages_per_step
    bb = pl.program_id(0)
    g = pl.program_id(1)
    ng = pl.num_programs(1)
    step = bb * ng + g
    slot = step % 2

    def fetch(bb_, g_, into):
        for bi in range(nb):
            for p in range(pages_per_step):
                pid = pt_ref[(bb_ * nb + bi) * n_pages + g_ * pages_per_step + p]
                j = bi * pages_per_step + p
                pltpu.make_async_copy(ckv_hbm.at[pid], ckv_buf.at[into, j], sem.at[0, into]).start(priority=j % 2)
                pltpu.make_async_copy(kr_hbm.at[pid], kr_buf.at[into, j], sem.at[1, into]).start(priority=j % 2)

    @pl.when(step == 0)
    def _():
        fetch(bb, g, 0)

    @pl.when(step + 1 < pl.num_programs(0) * ng)
    def _():
        wrap = g + 1 == ng
        fetch(jnp.where(wrap, bb + 1, bb), jnp.where(wrap, 0, g + 1), 1 - slot)

    @pl.when(g == 0)
    def _():
        for bi in range(nb):
            qs_scr[bi] = _stack_heads(q_ref[bi])
        m_scr[...] = jnp.full(m_scr.shape, -jnp.inf, F32)
        l_scr[...] = jnp.zeros(l_scr.shape, F32)
        acc_scr[...] = jnp.zeros(acc_scr.shape, F32)

    pltpu.make_async_copy(ckv_hbm.at[pl.ds(0, npg)], ckv_buf.at[slot], sem.at[0, slot]).wait()
    pltpu.make_async_copy(kr_hbm.at[pl.ds(0, npg)], kr_buf.at[slot], sem.at[1, slot]).wait()

    _round_robin([
        _paged_pages_chain(qs_scr.at[bi],
                           [ckv_buf.at[slot, bi * pages_per_step + p] for p in range(pages_per_step)],
                           [kr_buf.at[slot, bi * pages_per_step + p] for p in range(pages_per_step)],
                           m_scr.at[bi], l_scr.at[bi], acc_scr.at[bi])
        for bi in range(nb)])

    @pl.when(g == ng - 1)
    def _():
        for bi in range(nb):
            kvn = kvnew_ref[bi]
            s = lax.dot_general(qs_scr[bi], kvn, NT, preferred_element_type=F32)
            qpos = lax.broadcasted_iota(jnp.int32, s.shape, 0) % t
            kpos = lax.broadcasted_iota(jnp.int32, s.shape, 1)
            s = jnp.where(kpos <= qpos, s, -jnp.inf)
            _online_softmax_step(s, kvn[:, :KV_LORA], m_scr.at[bi], l_scr.at[bi], acc_scr.at[bi])
            _unstack_store(o_ref.at[bi], acc_scr.at[bi], l_scr.at[bi], t)


def _paged_attn(q, kvnew, cache_ckv, cache_krope_t, page_table, pages_per_step):
    b, t, _ = q.shape
    n_pages = page_table.shape[1]
    page = cache_ckv.shape[1]
    assert n_pages % pages_per_step == 0
    rows = MLA_HEADS * t
    nb = 2 if b % 2 == 0 else 1

    npg = nb * pages_per_step
    assert cache_ckv.shape[0] >= npg
    grid_spec = pltpu.PrefetchScalarGridSpec(
        num_scalar_prefetch=1,
        grid=(b // nb, n_pages // pages_per_step),
        in_specs=[pl.BlockSpec((nb, t, MLA_HEADS * QK_PAD), lambda bb, g, pt: (bb, 0, 0)),
                  pl.BlockSpec((nb, t, QK_PAD), lambda bb, g, pt: (bb, 0, 0)),
                  pl.BlockSpec(memory_space=pl.ANY), pl.BlockSpec(memory_space=pl.ANY)],
        out_specs=pl.BlockSpec((nb, t, MLA_HEADS * KV_LORA), lambda bb, g, pt: (bb, 0, 0)),
        scratch_shapes=[pltpu.VMEM((nb, rows, QK_PAD), BF16), pltpu.VMEM((nb, rows, 1), F32),
                        pltpu.VMEM((nb, rows, 1), F32), pltpu.VMEM((nb, rows, KV_LORA), F32),
                        pltpu.VMEM((2, npg, page, KV_LORA), cache_ckv.dtype),
                        pltpu.VMEM((2, npg, QK_ROPE, page), cache_krope_t.dtype),
                        pltpu.SemaphoreType.DMA((2, 2))],
    )
    return pl.pallas_call(
        functools.partial(_paged_attn_kernel, pages_per_step=pages_per_step, n_pages=n_pages, t=t),
        grid_spec=grid_spec,
        out_shape=jax.ShapeDtypeStruct((b, t, MLA_HEADS * KV_LORA), BF16),
        compiler_params=_cparams(("arbitrary", "arbitrary")),
        name="paged_attn",
    )(page_table.reshape(-1), q, kvnew, cache_ckv, cache_krope_t)


def _pack_bf16_pairs(x):
    w = x.shape[1] // 2
    xb = x.astype(BF16).astype(F32)
    lo = lax.shift_right_logical(lax.bitcast_convert_type(xb[:, :w], jnp.uint32), jnp.uint32(16))
    hi = lax.bitcast_convert_type(xb[:, w:], jnp.uint32) & jnp.uint32(0xFFFF0000)
    return lo | hi


def _store_row_slabs(ref, first, x):
    t, w = x.shape
    per = w // LANE
    for j in range(per):
        ref[pl.ds(first * per + j, t, stride=per), :] = x[:, j * LANE:(j + 1) * LANE]


def _load_row_slabs(ref, first, t, w):
    per = w // LANE
    return jnp.concatenate([ref[pl.ds(first * per + j, t, stride=per), :] for j in range(per)], axis=1)


def _unpack_bf16_pairs(p):
    lo = lax.bitcast_convert_type(lax.shift_left(p, jnp.uint32(16)), F32)
    hi = lax.bitcast_convert_type(p & jnp.uint32(0xFFFF0000), F32)
    return jnp.concatenate([lo, hi], axis=1)


def _post_mix_kernel(h_ref, ogdn_ref, olat_ref, wuv_ref, wout_ref, gffn_ref, wrh_ref, wrl_ref, rbias_ref,
                     wsgu_ref, wsd_ref,
                     h1_ref, xp_ref, idx_ref, rank_ref, wtok_ref, cnt_ref, carry):
    i = pl.program_id(0)
    tm = h_ref.shape[0]

    @pl.when(i == 0)
    def _():
        carry[...] = jnp.zeros(carry.shape, F32)

    parts = [ogdn_ref[...]]
    for hh in range(MLA_HEADS):
        o_h = jnp.dot(olat_ref[:, hh * KV_LORA:(hh + 1) * KV_LORA], wuv_ref[hh], preferred_element_type=F32)
        parts.append(o_h.astype(BF16))
    mix = jnp.concatenate(parts, axis=1)
    h1 = h_ref[...] + jnp.dot(mix, wout_ref[...], preferred_element_type=F32)
    u2 = _rms(h1, gffn_ref[...])
    u2b = u2.astype(BF16)
    _store_row_slabs(xp_ref, 0, _pack_bf16_pairs(u2))

    u2l = (u2 - u2b.astype(F32)).astype(BF16)
    logits = (lax.dot_general(wrh_ref[...], u2b, NT, preferred_element_type=F32)
              + lax.dot_general(wrh_ref[...], u2l, NT, preferred_element_type=F32)
              + lax.dot_general(wrl_ref[...], u2b, NT, preferred_element_type=F32))
    scores = _sigmoid(logits)
    biased = scores + rbias_ref[...]
    neg = -jnp.inf

    sub = lax.broadcasted_iota(jnp.int32, (GROUP_SIZE, tm), 0)
    gscore = []
    for gi in range(N_GROUPS):
        vg = biased[gi * GROUP_SIZE:(gi + 1) * GROUP_SIZE, :]
        m1 = jnp.max(vg, axis=0, keepdims=True)
        first = jnp.min(jnp.where(vg == m1, sub, GROUP_SIZE), axis=0, keepdims=True)
        m2 = jnp.max(jnp.where(sub == first, neg, vg), axis=0, keepdims=True)
        gscore.append(m1 + m2)
    gs = jnp.concatenate(gscore, axis=0)
    gid = lax.broadcasted_iota(jnp.int32, gs.shape, 0)
    gsel = jnp.zeros(gs.shape, F32)
    for _ in range(TOPK_GROUPS):
        m = jnp.max(gs, axis=0, keepdims=True)
        pick = jnp.min(jnp.where(gs == m, gid, N_GROUPS), axis=0, keepdims=True)
        hit = gid == pick
        gsel = jnp.where(hit, 1.0, gsel)
        gs = jnp.where(hit, neg, gs)
    emask = jnp.concatenate(
        [jnp.broadcast_to(gsel[gi:gi + 1, :], (GROUP_SIZE, tm)) for gi in range(N_GROUPS)], axis=0)
    cand = jnp.where(emask > 0.5, biased, neg)

    eid = lax.broadcasted_iota(jnp.int32, cand.shape, 0)
    chosen = jnp.zeros(cand.shape, F32)
    idx_rows, w_rows, hits = [], [], []
    for _ in range(TOP_K):
        m = jnp.max(cand, axis=0, keepdims=True)
        pick = jnp.min(jnp.where(cand == m, eid, N_EXPERTS), axis=0, keepdims=True)
        hit = eid == pick
        idx_rows.append(pick)
        w_rows.append(jnp.sum(jnp.where(hit, scores, 0.0), axis=0, keepdims=True))
        hits.append(hit)
        chosen = chosen + hit.astype(F32)
        cand = jnp.where(hit, neg, cand)
    wsel = jnp.concatenate(w_rows, axis=0)
    wsel = wsel / (jnp.sum(wsel, axis=0, keepdims=True) + 1e-20) * ROUTED_SCALE
    idx_ref[...] = jnp.concatenate(idx_rows, axis=0)

    ti = lax.broadcasted_iota(jnp.int32, (tm, tm), 0)
    tj = lax.broadcasted_iota(jnp.int32, (tm, tm), 1)
    before = (ti < tj).astype(BF16)
    prefix = jnp.dot(chosen.astype(BF16), before, preferred_element_type=F32) + carry[:, 0:1]
    rank_rows = [jnp.sum(jnp.where(hit, prefix, 0.0), axis=0, keepdims=True) for hit in hits]
    rank_ref[...] = jnp.concatenate(rank_rows, axis=0).astype(jnp.int32)
    carry[...] = carry[...] + jnp.sum(chosen, axis=1, keepdims=True)
    cnt_ref[...] = carry[...]

    wpad = jnp.concatenate([wsel, jnp.zeros((LANE - TOP_K, tm), F32)], axis=0)
    wtok_ref[...] = wpad.T

    gu = jnp.dot(u2b, wsgu_ref[...], preferred_element_type=F32)
    hs = (_silu(gu[:, :D_SHARED]) * gu[:, D_SHARED:]).astype(BF16)
    h1_ref[...] = h1 + jnp.dot(hs, wsd_ref[...], preferred_element_type=F32)


def _post_mix(h, o_gdn, o_lat, wts, tm):
    n, d = h.shape
    tok = lambda w: pl.BlockSpec((tm, w), lambda i: (i, 0))
    tr = lambda r: pl.BlockSpec((r, tm), lambda i: (0, i))
    return pl.pallas_call(
        _post_mix_kernel,
        grid=(n // tm,),
        in_specs=[tok(d), tok(GDN_V_W), tok(MLA_HEADS * KV_LORA), _full(wts['wuv'].shape), _full((d, d)),
                  _full((1, d)), _full((N_EXPERTS, d)), _full((N_EXPERTS, d)), _full((N_EXPERTS, 1)),
                  _full((d, 2 * D_SHARED)), _full((D_SHARED, d))],
        out_specs=[tok(d), pl.BlockSpec((tm * ROW_SLAB, LANE), lambda i: (i, 0)), tr(TOP_K), tr(TOP_K), tok(LANE),
                   _full((N_EXPERTS, LANE))],
        out_shape=(jax.ShapeDtypeStruct((n, d), F32),
                   jax.ShapeDtypeStruct((n * ROW_SLAB, LANE), jnp.uint32),
                   jax.ShapeDtypeStruct((TOP_K, n), jnp.int32),
                   jax.ShapeDtypeStruct((TOP_K, n), jnp.int32),
                   jax.ShapeDtypeStruct((n, LANE), F32),
                   jax.ShapeDtypeStruct((N_EXPERTS, LANE), F32)),
        scratch_shapes=[pltpu.VMEM((N_EXPERTS, LANE), F32)],
        compiler_params=_cparams(("arbitrary",)),
        name="post_mix",
    )(h, o_gdn, o_lat, wts['wuv'], wts['w_out'], wts['g_ffn'], wts['wr_hi'], wts['wr_lo'], wts['rbias'],
      wts['ws_gu'], wts['ws_d'])


def _dest_kernel(idx_ref, rank_ref, starts_ref, o_ref):
    eid = lax.broadcasted_iota(jnp.int32, (N_EXPERTS, idx_ref.shape[1]), 0)
    starts = starts_ref[...]
    rows = []
    for kk in range(TOP_K):
        hit = eid == idx_ref[kk:kk + 1, :]
        rows.append(jnp.sum(jnp.where(hit, starts, 0.0), axis=0, keepdims=True))
    o_ref[...] = jnp.concatenate(rows, axis=0).astype(jnp.int32) + rank_ref[...]


def _dest_rows(idx_t, rank_t, starts, tn):
    k, n = idx_t.shape
    assert n * k < 2 ** 24
    tr = pl.BlockSpec((k, tn), lambda i: (0, i))
    return pl.pallas_call(
        _dest_kernel,
        grid=(n // tn,),
        in_specs=[tr, tr, _full((N_EXPERTS, 1))],
        out_specs=tr,
        out_shape=jax.ShapeDtypeStruct((k, n), jnp.int32),
        compiler_params=_cparams(("arbitrary",)),
        name="dest_rows",
    )(idx_t, rank_t, starts.astype(F32).reshape(N_EXPERTS, 1))


def _slab(ref, row):
    return ref.at[pl.ds(pl.multiple_of(row * ROW_SLAB, ROW_SLAB), ROW_SLAB)]


def _scatter_kernel(dest_ref, x_ref, o_hbm, sem):
    fan, nrow = dest_ref.shape

    def start(r, carry):
        for kk in range(fan):
            pltpu.make_async_copy(_slab(x_ref, r), _slab(o_hbm, dest_ref[kk, r]), sem).start(priority=kk % 2)
        return carry

    lax.fori_loop(0, nrow, start, 0)
    whole = o_hbm.at[pl.ds(0, fan * nrow * ROW_SLAB)]
    pltpu.make_async_copy(whole, whole, sem).wait()


def _scatter_rows(x, dest_t, ts):
    w = x.shape[1]
    n = x.shape[0] // ROW_SLAB
    fan = dest_t.shape[0]
    return pl.pallas_call(
        _scatter_kernel,
        grid=(n // ts,),
        in_specs=[pl.BlockSpec((fan, ts), lambda i: (0, i), memory_space=pltpu.SMEM),
                  pl.BlockSpec((ts * ROW_SLAB, w), lambda i: (i, 0))],
        out_specs=pl.BlockSpec(memory_space=pl.ANY),
        out_shape=jax.ShapeDtypeStruct((n * fan * ROW_SLAB, w), x.dtype),
        scratch_shapes=[pltpu.SemaphoreType.DMA],
        compiler_params=pltpu.CompilerParams(dimension_semantics=("arbitrary",), has_side_effects=True),
        name="scatter_rows",
    )(dest_t, x)


def _experts_kernel(vt_ref, ve_ref, seg_ref, nv_ref, x_ref, wgu_ref, wd_ref, o_ref, *, bm):
    v = pl.program_id(0)
    tile = vt_ref[v]
    e = ve_ref[v]
    prev_tile = vt_ref[jnp.maximum(v - 1, 0)]
    first = (v == 0) | (tile != prev_tile)

    @pl.when(v < nv_ref[0])
    def _():
        w = ROW_SLAB * LANE
        x = _unpack_bf16_pairs(_load_row_slabs(x_ref, 0, bm, w)).astype(BF16)
        gu = jnp.dot(x, wgu_ref[...], preferred_element_type=F32)
        hid = (_silu(gu[:, :D_EXPERT]) * gu[:, D_EXPERT:]).astype(BF16)
        y = _pack_bf16_pairs(jnp.dot(hid, wd_ref[...], preferred_element_type=F32))
        rowid = tile * bm + lax.broadcasted_iota(jnp.int32, y.shape, 0)
        mine = (rowid >= seg_ref[e]) & (rowid < seg_ref[e + 1])

        @pl.when(first)
        def _():
            _store_row_slabs(o_ref, 0, jnp.where(mine, y, jnp.uint32(0)))

        @pl.when(jnp.logical_not(first))
        def _():
            _store_row_slabs(o_ref, 0, jnp.where(mine, y, _load_row_slabs(o_ref, 0, bm, w)))


def _experts(xs, visit_tile, visit_expert, seg, n_visits, wgu, wd, bm):
    rows, w = xs.shape
    nv_max = visit_tile.shape[0]
    d = wd.shape[2]
    grid_spec = pltpu.PrefetchScalarGridSpec(
        num_scalar_prefetch=4,
        grid=(nv_max,),
        in_specs=[pl.BlockSpec((bm * ROW_SLAB, w), lambda v, vt, ve, sg, nv: (vt[v], 0)),
                  pl.BlockSpec((None, d, 2 * D_EXPERT), lambda v, vt, ve, sg, nv: (ve[v], 0, 0)),
                  pl.BlockSpec((None, D_EXPERT, d), lambda v, vt, ve, sg, nv: (ve[v], 0, 0))],
        out_specs=pl.BlockSpec((bm * ROW_SLAB, w), lambda v, vt, ve, sg, nv: (vt[v], 0)),
    )
    return pl.pallas_call(
        functools.partial(_experts_kernel, bm=bm),
        grid_spec=grid_spec,
        out_shape=jax.ShapeDtypeStruct((rows, w), jnp.uint32),
        compiler_params=_cparams(("arbitrary",)),
        name="experts",
    )(visit_tile, visit_expert, seg, n_visits, xs, wgu, wd)


def _visit_list(counts, rows, bm):
    ntiles = rows // bm
    nv_max = ntiles + N_EXPERTS - 1
    ends = jnp.cumsum(counts)
    starts = ends - counts
    first_tile = starts // bm
    ntile_e = jnp.where(counts > 0, (ends - 1) // bm - first_tile + 1, 0)
    vend = jnp.cumsum(ntile_e)
    vstart = vend - ntile_e
    n_visits = vend[-1]
    v = jnp.arange(nv_max, dtype=jnp.int32)
    ve = jnp.minimum(jnp.sum(vend[None, :] <= v[:, None], axis=1), N_EXPERTS - 1).astype(jnp.int32)
    vt = (first_tile[ve] + v - vstart[ve]).astype(jnp.int32)
    last = jnp.maximum(n_visits - 1, 0)
    ve = jnp.where(v < n_visits, ve, ve[last])
    vt = jnp.where(v < n_visits, vt, vt[last])
    seg = jnp.concatenate([starts, ends[-1:]]).astype(jnp.int32)
    return vt, ve, seg, n_visits.reshape(1).astype(jnp.int32), starts


def _final_kernel(dcur_ref, dnext_ref, h1_ref, wtok_ref, p_ref, gple_ref, wgate_ref, wproj_ref, gfin_ref,
                  ys_hbm, o_ref, ybuf, h2_scr, sem):
    i = pl.program_id(0)
    nsteps = pl.num_programs(0)
    tm = h1_ref.shape[0]
    slot = i % 2
    grp = SUBLANE

    def fetch_rows(dest_ref, into, r0):
        for dr in range(grp):
            for kk in range(TOP_K):
                r = r0 + dr
                pltpu.make_async_copy(_slab(ys_hbm, dest_ref[kk, r]), _slab(ybuf.at[into], kk * tm + r),
                                      sem.at[into]).start(priority=kk % 2)

    def combine_rows(r0):
        rows = pl.ds(r0, grp)
        wtok = wtok_ref[rows, :]
        h2 = h1_ref[rows, :]
        for kk in range(TOP_K):
            yk = _unpack_bf16_pairs(_load_row_slabs(ybuf.at[slot], kk * tm + r0, grp, ROW_SLAB * LANE))
            h2 = h2 + yk * wtok[:, kk:kk + 1]
        h2_scr[rows, :] = h2

    def row_loop(body):
        def step(g, carry):
            body(pl.multiple_of(g * grp, grp))
            return carry
        lax.fori_loop(0, tm // grp, step, 0)

    @pl.when(i == 0)
    def _():
        row_loop(lambda r0: fetch_rows(dcur_ref, 0, r0))

    pltpu.make_async_copy(ys_hbm.at[pl.ds(0, TOP_K * tm * ROW_SLAB)], ybuf.at[slot], sem.at[slot]).wait()

    @pl.when(i + 1 < nsteps)
    def _():
        def both(r0):
            fetch_rows(dnext_ref, 1 - slot, r0)
            combine_rows(r0)
        row_loop(both)

    @pl.when(i + 1 == nsteps)
    def _():
        row_loop(combine_rows)

    h2 = h2_scr[...]
    n = _rms(h2, gple_ref[...]).astype(BF16)
    gate = _sigmoid(jnp.dot(n, wgate_ref[...], preferred_element_type=F32))
    pp = jnp.dot(p_ref[...].astype(BF16), wproj_ref[...], preferred_element_type=F32)
    o_ref[...] = _rms(h2 + gate * pp, gfin_ref[...])


def _final(h1, ys, dest_t, wtok, p, wts, tm):
    n, d = h1.shape
    nsteps = n // tm
    assert ys.shape[0] >= TOP_K * tm * ROW_SLAB
    tok = lambda w: pl.BlockSpec((tm, w), lambda i: (i, 0))
    dspec = lambda f: pl.BlockSpec((TOP_K, tm), f, memory_space=pltpu.SMEM)
    return pl.pallas_call(
        _final_kernel,
        grid=(nsteps,),
        in_specs=[dspec(lambda i: (0, i)), dspec(lambda i: (0, jnp.minimum(i + 1, nsteps - 1))),
                  tok(d), tok(LANE), tok(p.shape[1]), _full((1, d)), _full((d, d)),
                  _full((p.shape[1], d)), _full((1, d)), pl.BlockSpec(memory_space=pl.ANY)],
        out_specs=tok(d),
        out_shape=jax.ShapeDtypeStruct((n, d), F32),
        scratch_shapes=[pltpu.VMEM((2, TOP_K * tm * ROW_SLAB, ys.shape[1]), ys.dtype), pltpu.VMEM((tm, d), F32),
                        pltpu.SemaphoreType.DMA((2,))],
        compiler_params=_cparams(("arbitrary",)),
        name="final",
    )(dest_t, dest_t, h1, wtok, p, wts['g_ple'], wts['w_ple_gate'], wts['w_ple_proj'], wts['g_final'], ys)


def _rotate_half_cols(w):
    half = QK_ROPE // 2
    wh = w.reshape(w.shape[0], -1, 2, half)
    return jnp.concatenate([-wh[:, :, 1:2], wh[:, :, 0:1]], axis=2).reshape(w.shape)


def _prep_weights(g_mix, w_in, gdn_conv_w, gdn_a_log, gdn_dt_bias, gdn_norm, mla_g_q, mla_w_uq, mla_g_kv,
                  mla_w_uk, mla_w_uv, w_out, g_ffn, w_router, router_bias, w_exp_gate, w_exp_up, w_exp_down,
                  w_sh_gate, w_sh_up, w_sh_down, g_ple, w_ple_gate, w_ple_proj, g_final):
    d = w_in.shape[0]
    splits = np.cumsum([CONV_CH, GDN_V_W, GDN_HEADS, GDN_HEADS, Q_LORA, KV_LORA])
    w_qkv, w_z, w_b, w_a, w_cq, w_ckv, w_kr = jnp.split(w_in, [int(s) for s in splits], axis=1)
    ba_pad = jnp.zeros((d, LANE - 2 * GDN_HEADS), w_in.dtype)
    w1 = jnp.concatenate([w_qkv, w_z, w_cq, w_ckv, w_kr, _rotate_half_cols(w_kr), w_b, w_a, ba_pad], axis=1)
    assert w1.shape[1] == W1_WIDTH
    lane_pad = lambda v: jnp.pad(v.astype(F32), (GDN_HEADS, LANE - 2 * GDN_HEADS)).reshape(1, LANE)
    uq = mla_w_uq.reshape(Q_LORA, MLA_HEADS, QK_NOPE + QK_ROPE)
    uq_nope = uq[:, :, :QK_NOPE].reshape(Q_LORA, MLA_HEADS * QK_NOPE)
    uq_rope = uq[:, :, QK_NOPE:].reshape(Q_LORA, MLA_HEADS * QK_ROPE)
    wuq = jnp.concatenate([uq_nope, uq_rope, _rotate_half_cols(uq_rope)], axis=1)
    wr_t = w_router.T.astype(F32)
    wr_hi = wr_t.astype(BF16)
    row = lambda v: v.astype(F32).reshape(1, -1)
    return dict(
        g_mix=row(g_mix), w1=w1.astype(BF16), alog=lane_pad(gdn_a_log), dtb=lane_pad(gdn_dt_bias),
        g_q=row(mla_g_q), wuq=wuq.astype(BF16), wuk=jnp.transpose(mla_w_uk, (1, 2, 0)).astype(BF16),
        g_kv=row(mla_g_kv),
        conv_w=jnp.pad(gdn_conv_w.astype(F32), ((0, SUBLANE - CONV_W), (0, 0))), gdn_norm=row(gdn_norm),
        wuv=jnp.transpose(mla_w_uv, (1, 0, 2)).astype(BF16), w_out=w_out.astype(BF16), g_ffn=row(g_ffn),
        wr_hi=wr_hi, wr_lo=(wr_t - wr_hi.astype(F32)).astype(BF16),
        rbias=router_bias.astype(F32).reshape(N_EXPERTS, 1),
        ws_gu=jnp.concatenate([w_sh_gate, w_sh_up], axis=1).astype(BF16), ws_d=w_sh_down.astype(BF16),
        we_gu=jnp.concatenate([w_exp_gate, w_exp_up], axis=2).astype(BF16), we_d=w_exp_down.astype(BF16),
        g_ple=row(g_ple), w_ple_gate=w_ple_gate.astype(BF16), w_ple_proj=w_ple_proj.astype(BF16),
        g_final=row(g_final),
    )


def _rope_tables(pos, reps):
    half = QK_ROPE // 2
    inv = ROPE_THETA ** (-jnp.arange(half, dtype=F32) / half)
    ang = pos.astype(F32)[:, None] * inv[None, :]
    cos = jnp.concatenate([jnp.cos(ang)] * 2, axis=1)
    sin = jnp.concatenate([jnp.sin(ang)] * 2, axis=1)
    cos = jnp.tile(jnp.tile(cos, (1, MLA_HEADS)), (reps, 1))
    sin = jnp.tile(jnp.tile(sin, (1, MLA_HEADS)), (reps, 1))
    return cos, sin


def _tile(n, pref):
    t = min(n, pref)
    assert n % t == 0
    return t


def _layer(x, p, pos, gdn_state, conv_buf, attend, wts, chunk):
    b, t, d = x.shape
    n = b * t
    tm = _tile(n, 512)
    if t >= tm:
        assert t % tm == 0
        cos, sin = _rope_tables(pos, 1)
    else:
        assert tm % t == 0
        cos, sin = _rope_tables(pos, tm // t)
    x2 = x.reshape(n, d)
    qkv, z, gb, ckv, krope, kv, q = _in_proj(x2, wts, cos, sin, tm)

    conv0 = jnp.pad(conv_buf.astype(F32), ((0, 0), (SUBLANE - (CONV_W - 1), 0), (0, 0)))
    o_gdn, gdn_new = _gdn(qkv.reshape(b, t, CONV_CH), z.reshape(b, t, GDN_V_W), gb.reshape(b, t, LANE),
                          gdn_state.astype(F32), conv0, wts['conv_w'], wts['gdn_norm'], chunk)
    conv_new = qkv.reshape(b, t, CONV_CH)[:, t - (CONV_W - 1):, :]

    o_lat = attend(q.reshape(b, t, MLA_HEADS * QK_PAD), kv.reshape(b, t, QK_PAD))

    h1, xp, idx_t, rank_t, wtok, counts = _post_mix(x2, o_gdn.reshape(n, GDN_V_W),
                                                    o_lat.reshape(n, MLA_HEADS * KV_LORA), wts, tm)
    rows = n * TOP_K
    bm = _tile(rows, 512)
    cnt = counts[:, 0].astype(jnp.int32)
    vt, ve, seg, n_visits, starts = _visit_list(cnt, rows, bm)
    dest_t = _dest_rows(idx_t, rank_t, starts, _tile(n, 2048))
    xs = _scatter_rows(xp, dest_t, tm)
    ys = _experts(xs, vt, ve, seg, n_visits, wts['we_gu'], wts['we_d'], bm)
    y = _final(h1, ys, dest_t, wtok, p.reshape(n, -1), wts, tm)
    return y.reshape(b, t, d), ckv.reshape(b, t, KV_LORA), krope.reshape(b, t, QK_ROPE), gdn_new, conv_new


def kernel(x_prompt, x_sample, p_prompt, p_sample, cache_ckv, cache_krope, state_gdn, state_conv, page_table,
           g_mix, w_in, gdn_conv_w, gdn_a_log, gdn_dt_bias, gdn_norm, mla_g_q, mla_w_uq, mla_g_kv, mla_w_uk,
           mla_w_uv, w_out, g_ffn, w_router, router_bias, w_exp_gate, w_exp_up, w_exp_down, w_sh_gate,
           w_sh_up, w_sh_down, g_ple, w_ple_gate, w_ple_proj, g_final):
    depth = w_in.shape[0]
    assert depth == 1, "the final norm is fused into the (single) layer"
    bp, tp, _ = x_prompt.shape
    bs, ts, _ = x_sample.shape
    past = page_table.shape[1] * cache_ckv.shape[2]
    wts = _prep_weights(g_mix[0], w_in[0], gdn_conv_w[0], gdn_a_log[0], gdn_dt_bias[0], gdn_norm[0], mla_g_q[0],
                        mla_w_uq[0], mla_g_kv[0], mla_w_uk[0], mla_w_uv[0], w_out[0], g_ffn[0], w_router[0],
                        router_bias[0], w_exp_gate[0], w_exp_up[0], w_exp_down[0], w_sh_gate[0], w_sh_up[0],
                        w_sh_down[0], g_ple[0], w_ple_gate[0], w_ple_proj[0], g_final)

    cache_krope_t = jnp.swapaxes(cache_krope[0], 1, 2)

    def attend_s(q, kv):
        return _paged_attn(q, kv, cache_ckv[0], cache_krope_t, page_table,
                           pages_per_step=_tile(page_table.shape[1], 16))

    ys, c2, k2, s2, v2 = _layer(x_sample, p_sample[0], past + jnp.arange(ts), state_gdn[0], state_conv[0],
                                attend_s, wts, chunk=ts)

    gdn0 = jnp.zeros((bp, GDN_HEADS, GDN_DK, GDN_DV), state_gdn.dtype)
    conv0 = jnp.zeros((bp, CONV_W - 1, CONV_CH), state_conv.dtype)
    attend_p = functools.partial(_prompt_attn, tq=_tile(tp, 256), tk=_tile(tp, 512))
    yp, c1, k1, s1, v1 = _layer(x_prompt, p_prompt[0], jnp.arange(tp), gdn0, conv0, attend_p, wts,
                                chunk=_tile(tp, 64))
    st = lambda a, ref: a.astype(ref.dtype)[None]
    return (yp, ys, c1[None], k1[None], st(s1, state_gdn), st(v1, state_conv),
            c2[None], k2[None], st(s2, state_gdn), st(v2, state_conv))
```

```python
import functools

import jax
import jax.numpy as jnp
import numpy as np
from jax import lax
from jax.experimental import pallas as pl
from jax.experimental.pallas import tpu as pltpu

EPS = 1e-6
ROPE_THETA = 10000.0
GDN_HEADS = 4
GDN_DK = 128
GDN_DV = 128
CONV_W = 4
MLA_HEADS = 4
Q_LORA = 256
KV_LORA = 256
QK_NOPE = 128
QK_ROPE = 64
V_DIM = 128
N_EXPERTS = 64
TOP_K = 8
N_GROUPS = 8
TOPK_GROUPS = 4
GROUP_SIZE = N_EXPERTS // N_GROUPS
D_EXPERT = 256
D_SHARED = 256
ROUTED_SCALE = 2.5
MLA_SCALE = (QK_NOPE + QK_ROPE) ** -0.5

GDN_QK_W = GDN_HEADS * GDN_DK
GDN_V_W = GDN_HEADS * GDN_DV
CONV_CH = 2 * GDN_QK_W + GDN_V_W
LANE = 128
SUBLANE = 8
QK_PAD = KV_LORA + LANE
VMEM_LIMIT = 56 * 1024 * 1024
D_MODEL = 1024
ROW_SLAB = D_MODEL // 2 // LANE

C_QKV = 0
C_Z = C_QKV + CONV_CH
C_CQ = C_Z + GDN_V_W
C_CKV = C_CQ + Q_LORA
C_KR = C_CKV + KV_LORA
C_KRR = C_KR + QK_ROPE
C_BA = C_KRR + QK_ROPE
W1_WIDTH = C_BA + LANE

F32 = jnp.float32
BF16 = jnp.bfloat16
NT = (((1,), (1,)), ((), ()))
TN = (((0,), (0,)), ((), ()))


def _cparams(sem, **flags):
    return pltpu.CompilerParams(dimension_semantics=sem, vmem_limit_bytes=VMEM_LIMIT, flags=flags or None)


def _rms(x, g):
    return x * lax.rsqrt(jnp.mean(x * x, axis=-1, keepdims=True) + EPS) * g


def _sigmoid(x):
    return 1.0 / (1.0 + jnp.exp(-x))


def _silu(x):
    return x * _sigmoid(x)


def _full(shape):
    return pl.BlockSpec(shape, lambda *_: (0,) * len(shape))


def _in_proj_kernel(x_ref, gmix_ref, w1_ref, alog_ref, dtb_ref, gq_ref, wuq_ref, wuk_ref, gkv_ref,
                    cos_ref, sin_ref,
                    qkv_ref, z_ref, gb_ref, ckv_ref, krope_ref, kv_ref, q_ref):
    u = _rms(x_ref[...], gmix_ref[...]).astype(BF16)
    qkv_ref[...] = jnp.dot(u, w1_ref[:, C_QKV:C_Z], preferred_element_type=F32)
    z_ref[...] = jnp.dot(u, w1_ref[:, C_Z:C_CQ], preferred_element_type=F32)
    rest = jnp.dot(u, w1_ref[:, C_CQ:W1_WIDTH], preferred_element_type=F32)
    c_q = rest[:, 0:Q_LORA]
    c_kv = rest[:, C_CKV - C_CQ:C_KR - C_CQ]
    kr2 = rest[:, C_KR - C_CQ:C_BA - C_CQ]
    ba = rest[:, C_BA - C_CQ:]

    lane = lax.broadcasted_iota(jnp.int32, ba.shape, 1)
    beta = _sigmoid(ba)
    sp_in = ba + dtb_ref[...]
    softplus = jnp.maximum(sp_in, 0.0) + jnp.log(1.0 + jnp.exp(-jnp.abs(sp_in)))
    g = -jnp.exp(alog_ref[...]) * softplus
    gb_ref[...] = jnp.where(lane < GDN_HEADS, beta, g)

    cos = cos_ref[...]
    sin = sin_ref[...]
    ckv = _rms(c_kv, gkv_ref[...])
    ckv_ref[...] = ckv
    krope = kr2[:, :QK_ROPE] * cos[:, :QK_ROPE] + kr2[:, QK_ROPE:] * sin[:, :QK_ROPE]
    krope_ref[...] = krope
    zpad = jnp.zeros((ckv.shape[0], LANE - QK_ROPE), F32)
    kv_ref[...] = jnp.concatenate([ckv, krope, zpad], axis=1).astype(BF16)

    cqn = _rms(c_q, gq_ref[...]).astype(BF16)
    qh = jnp.dot(cqn, wuq_ref[...], preferred_element_type=F32)
    nope_w = MLA_HEADS * QK_NOPE
    rope_w = MLA_HEADS * QK_ROPE
    q_rope = qh[:, nope_w:nope_w + rope_w] * cos + qh[:, nope_w + rope_w:] * sin
    parts = []
    for h in range(MLA_HEADS):
        q_nope = qh[:, h * QK_NOPE:(h + 1) * QK_NOPE].astype(BF16)
        q_lat = jnp.dot(q_nope, wuk_ref[h], preferred_element_type=F32)
        parts += [q_lat, q_rope[:, h * QK_ROPE:(h + 1) * QK_ROPE], zpad]
    q_ref[...] = (jnp.concatenate(parts, axis=1) * MLA_SCALE).astype(BF16)


def _in_proj(x, wts, cos, sin, tm):
    n, d = x.shape
    period = cos.shape[0]
    nper = period // tm
    tok = lambda w: pl.BlockSpec((tm, w), lambda i: (i, 0))
    pos = pl.BlockSpec((tm, cos.shape[1]), lambda i: (i % nper, 0))
    out_shapes = (
        jax.ShapeDtypeStruct((n, CONV_CH), F32),
        jax.ShapeDtypeStruct((n, GDN_V_W), F32),
        jax.ShapeDtypeStruct((n, LANE), F32),
        jax.ShapeDtypeStruct((n, KV_LORA), F32),
        jax.ShapeDtypeStruct((n, QK_ROPE), F32),
        jax.ShapeDtypeStruct((n, QK_PAD), BF16),
        jax.ShapeDtypeStruct((n, MLA_HEADS * QK_PAD), BF16),
    )
    return pl.pallas_call(
        _in_proj_kernel,
        grid=(n // tm,),
        in_specs=[tok(d), _full((1, d)), _full(wts['w1'].shape), _full((1, LANE)), _full((1, LANE)),
                  _full((1, Q_LORA)), _full(wts['wuq'].shape), _full(wts['wuk'].shape), _full((1, KV_LORA)),
                  pos, pos],
        out_specs=[tok(CONV_CH), tok(GDN_V_W), tok(LANE), tok(KV_LORA), tok(QK_ROPE), tok(QK_PAD),
                   tok(MLA_HEADS * QK_PAD)],
        out_shape=out_shapes,
        compiler_params=_cparams(("arbitrary",)),
        name="in_proj",
    )(x, wts['g_mix'], wts['w1'], wts['alog'], wts['dtb'], wts['g_q'], wts['wuq'], wts['wuk'], wts['g_kv'],
      cos, sin)


def _gdn_kernel(qkv_ref, z_ref, gb_ref, s0_ref, conv0_ref, *rest, chunk):
    consts = rest[:-4]
    o_ref, sout_ref, s_scr, xbuf = rest[-4:]
    c = pl.program_id(1)

    @pl.when(c == 0)
    def _():
        s_scr[...] = s0_ref[...]
        xbuf[:, 0:SUBLANE, :] = conv0_ref[...]

    cw_ref, gn_ref = consts[0], consts[1]
    nb, rows = qkv_ref.shape[0], qkv_ref.shape[1]
    chains = {}
    for bi in range(nb):
        y = _gdn_conv(qkv_ref.at[bi], cw_ref, xbuf.at[bi])
        for ci in range(rows // chunk):
            tok = slice(ci * chunk, (ci + 1) * chunk)
            chains[bi, ci] = _gdn_chunk_prep(y[tok], gb_ref[bi, tok, :], *consts[2:], chunk=chunk)
    prepped = dict(zip(chains, _round_robin(list(chains.values()))))
    for ci in range(rows // chunk):
        for bi in range(nb):
            tok = slice(ci * chunk, (ci + 1) * chunk)
            out = _gdn_chunk_state(prepped[bi, ci], z_ref[bi, tok, :], gn_ref, consts[-1], s_scr.at[bi], chunk=chunk)
            for h in range(GDN_HEADS):
                o_ref[bi, tok, h * GDN_DV:(h + 1) * GDN_DV] = out[h * chunk:(h + 1) * chunk, :].astype(o_ref.dtype)

    @pl.when(c == pl.num_programs(1) - 1)
    def _():
        sout_ref[...] = s_scr[...]


def _gdn_conv(qkv_ref, cw_ref, xbuf):
    pre = SUBLANE
    tail = CONV_W - 1
    rows = qkv_ref.shape[0]
    xbuf[pre:pre + rows, :] = qkv_ref[...]
    y = xbuf[pre - tail:pre - tail + rows, :] * cw_ref[0:1, :]
    for j in range(1, CONV_W):
        y = y + xbuf[pre - tail + j:pre - tail + j + rows, :] * cw_ref[j:j + 1, :]
    xbuf[pre - tail:pre, :] = xbuf[pre + rows - tail:pre + rows, :]
    return _silu(y)


def _stack_gdn_heads(a, off):
    return jnp.concatenate([a[:, off + h * GDN_DK:off + (h + 1) * GDN_DK] for h in range(GDN_HEADS)], axis=0)


def _round_robin(gens):
    results = [None] * len(gens)
    live = list(range(len(gens)))
    while live:
        for i in list(live):
            try:
                next(gens[i])
            except StopIteration as done:
                results[i] = done.value
                live.remove(i)
    return results


def _gdn_chunk_prep(y, gb, cmat_ref, gsel_ref, ssel_ref, bsel_ref, place_ref, triu_ref, mask_ref, hsel_ref, *, chunk):
    stack = _stack_gdn_heads
    q = stack(y, 0)
    k = stack(y, GDN_QK_W)
    v = stack(y, 2 * GDN_QK_W)
    qn = q * lax.rsqrt(jnp.sum(q * q, axis=-1, keepdims=True) + EPS) * (GDN_DK ** -0.5)
    kn = k * lax.rsqrt(jnp.sum(k * k, axis=-1, keepdims=True) + EPS)

    r = GDN_HEADS * chunk
    lane_of = lambda a, sel: jnp.sum(a * sel, axis=-1, keepdims=True)
    gb_hi = gb.astype(BF16)
    gb_r1 = gb - gb_hi.astype(F32)
    gb_mid = gb_r1.astype(BF16)
    gb_lo = (gb_r1 - gb_mid.astype(F32)).astype(BF16)
    pieces = (gb_hi, gb_mid, gb_lo)
    cums = sum(jnp.dot(cmat_ref[...], p, preferred_element_type=F32) for p in pieces)
    total = jnp.sum(gb, axis=0, keepdims=True)
    gsel = gsel_ref[...]
    gcol = lane_of(cums, gsel)
    glast = lane_of(total, gsel)
    gstate = lane_of(total, ssel_ref[...])
    beta = lane_of(jnp.concatenate([gb] * GDN_HEADS, axis=0), bsel_ref[...])
    spread = sum(jnp.dot(p, place_ref[...], preferred_element_type=F32) for p in pieces)
    grow = jnp.sum(spread * triu_ref[...], axis=0, keepdims=True)

    incl = mask_ref[0]
    decay = jnp.exp(jnp.where(incl > 0.5, gcol - grow, -jnp.inf))
    kb = kn * beta
    knb = kn.astype(BF16)
    kk = lax.dot_general(kb.astype(BF16), knb, NT, preferred_element_type=F32)
    yield
    lower = kk * decay * mask_ref[1]
    inv = mask_ref[2] - lower * mask_ref[3]
    for lvl in range(4, mask_ref.shape[0]):
        ih = inv.astype(BF16)
        blk = (lower * mask_ref[lvl]).astype(BF16)
        t = jnp.dot(ih, blk, preferred_element_type=F32)
        yield
        c = jnp.dot(t.astype(BF16), ih, preferred_element_type=F32)
        yield
        inv = inv - c
    egc = jnp.exp(gcol)
    rhs = jnp.concatenate([v * beta, kb * egc], axis=1)
    uw = jnp.dot(inv.astype(BF16), rhs.astype(BF16), preferred_element_type=F32)
    yield
    u = uw[:, :GDN_DV]
    w = uw[:, GDN_DV:]
    qk = lax.dot_general(qn.astype(BF16), knb, NT, preferred_element_type=F32) * decay

    hsel = hsel_ref[...]
    wide = lambda a: (jnp.concatenate([a] * GDN_HEADS, axis=1) * hsel).astype(BF16)
    return dict(u=u, w_qe=jnp.concatenate([wide(w), wide(qn * egc)], axis=0), qk=qk.astype(BF16),
                kdec=wide(kn * jnp.exp(glast - gcol)), sdecay=jnp.exp(gstate))


def _gdn_chunk_state(p, z, gn_ref, hsel_ref, s_scr, *, chunk):
    r = GDN_HEADS * chunk
    st = s_scr[...]
    ws_qs = jnp.dot(p['w_qe'], st.astype(BF16), preferred_element_type=F32)
    v_new = (p['u'] - ws_qs[:r]).astype(BF16)
    o = ws_qs[r:] + jnp.dot(p['qk'], v_new, preferred_element_type=F32)
    s_scr[...] = st * p['sdecay'] + lax.dot_general(p['kdec'], v_new, TN, preferred_element_type=F32)
    return _rms(o, gn_ref[...]) * _silu(_stack_gdn_heads(z, 0))


def _gdn_constants(chunk):
    r = GDN_HEADS * chunk
    ri = np.arange(r)
    head, tok = ri // chunk, ri % chunk
    ti = np.arange(chunk)
    lane = np.arange(LANE)
    srow = np.arange(GDN_HEADS * GDN_DK)
    cmat = ti[None, :] <= tok[:, None]
    gsel = lane[None, :] == GDN_HEADS + head[:, None]
    ssel = lane[None, :] == GDN_HEADS + (srow // GDN_DK)[:, None]
    bsel = lane[None, :] == head[:, None]
    place = lane[:, None] == GDN_HEADS + head[None, :]
    triu = ti[:, None] <= tok[None, :]
    same = head[:, None] == head[None, :]
    masks = [same & (tok[:, None] >= tok[None, :]), same & (tok[:, None] > tok[None, :]),
             ri[:, None] == ri[None, :]]
    s = 1
    while s < chunk:
        masks.append(same & ((tok[:, None] // s) % 2 == 1) & ((tok[None, :] // s) == (tok[:, None] // s) - 1))
        s *= 2
    hsel = (np.arange(GDN_HEADS * GDN_DK)[None, :] // GDN_DK) == head[:, None]
    f = lambda a: jnp.asarray(np.asarray(a, np.float32))
    fb = lambda a: f(a).astype(BF16)
    return fb(cmat), f(gsel), f(ssel), f(bsel), fb(place), f(triu), f(np.stack(masks)), f(hsel)


def _gdn(qkv, z, gb, s0, conv0, cw, gn, chunk):
    b, t, _ = qkv.shape
    assert t % chunk == 0 and chunk % SUBLANE == 0
    nb = 2 if b % 2 == 0 else 1
    nch = 2 if (t // chunk) % 2 == 0 else 1
    rows = nch * chunk
    consts = _gdn_constants(chunk)
    tokb = lambda w: pl.BlockSpec((nb, rows, w), lambda i, c: (i, c, 0))
    state = pl.BlockSpec((nb, GDN_HEADS * GDN_DK, GDN_DV), lambda i, c: (i, 0, 0))
    o, s_new = pl.pallas_call(
        functools.partial(_gdn_kernel, chunk=chunk),
        grid=(b // nb, t // rows),
        in_specs=[tokb(CONV_CH), tokb(GDN_V_W), tokb(LANE), state,
                  pl.BlockSpec((nb, SUBLANE, CONV_CH), lambda i, c: (i, 0, 0)),
                  _full((SUBLANE, CONV_CH)), _full((1, GDN_DV))] + [_full(a.shape) for a in consts],
        out_specs=[tokb(GDN_V_W), state],
        out_shape=(jax.ShapeDtypeStruct((b, t, GDN_V_W), BF16),
                   jax.ShapeDtypeStruct((b, GDN_HEADS * GDN_DK, GDN_DV), F32)),
        scratch_shapes=[pltpu.VMEM((nb, GDN_HEADS * GDN_DK, GDN_DV), F32),
                        pltpu.VMEM((nb, SUBLANE + rows, CONV_CH), F32)],
        compiler_params=_cparams(("arbitrary", "arbitrary")),
        name="gdn",
    )(qkv, z, gb, s0.reshape(b, GDN_HEADS * GDN_DK, GDN_DV), conv0, cw, gn, *consts)
    return o, s_new.reshape(b, GDN_HEADS, GDN_DK, GDN_DV)


def _stack_heads(q):
    return jnp.concatenate([q[:, h * QK_PAD:(h + 1) * QK_PAD] for h in range(MLA_HEADS)], axis=0)


def _row_reduce(x, combine, reduce):
    w = x.shape[1]
    if w > LANE and w % LANE == 0:
        tiles = [x[:, i * LANE:(i + 1) * LANE] for i in range(w // LANE)]
        while len(tiles) > 1:
            tiles = [combine(tiles[i], tiles[i + 1]) if i + 1 < len(tiles) else tiles[i]
                     for i in range(0, len(tiles), 2)]
        x = tiles[0]
    return reduce(x, axis=-1, keepdims=True)


def _online_softmax_step(s, vals, m_scr, l_scr, acc_scr):
    m_prev = m_scr[...]
    m_new = jnp.maximum(m_prev, jnp.max(s, axis=-1, keepdims=True))
    alpha = jnp.exp(m_prev - m_new)
    p = jnp.exp(s - m_new)
    l_scr[...] = alpha * l_scr[...] + jnp.sum(p, axis=-1, keepdims=True)
    pb = p.astype(BF16)
    if isinstance(vals, (list, tuple)):
        w = s.shape[1] // len(vals)
        pv = sum(jnp.dot(pb[:, i * w:(i + 1) * w], vi, preferred_element_type=F32) for i, vi in enumerate(vals))
    else:
        pv = jnp.dot(pb, vals, preferred_element_type=F32)
    acc_scr[...] = alpha * acc_scr[...] + pv
    m_scr[...] = m_new


def _unstack_store(o_ref, acc_scr, l_scr, rows):
    o = acc_scr[...] / l_scr[...]
    for h in range(MLA_HEADS):
        o_ref[:, h * KV_LORA:(h + 1) * KV_LORA] = o[h * rows:(h + 1) * rows, :].astype(o_ref.dtype)


def _prompt_attn_kernel(q_ref, kv_ref, o_ref, qs_scr, m_scr, l_scr, acc_scr, *, tq, tk):
    i = pl.program_id(1)
    j = pl.program_id(2)
    nj = pl.num_programs(2)

    @pl.when(j == 0)
    def _():
        qs_scr[...] = _stack_heads(q_ref[...])
        m_scr[...] = jnp.full(m_scr.shape, -jnp.inf, F32)
        l_scr[...] = jnp.zeros(l_scr.shape, F32)
        acc_scr[...] = jnp.zeros(acc_scr.shape, F32)

    needed = j * tk <= i * tq + tq - 1
    unmasked = j * tk + tk - 1 <= i * tq
    nchain = 2
    rows = MLA_HEADS * tq // nchain

    def chain(c, masked):
        rs = pl.ds(c * rows, rows)
        kv = kv_ref[...]
        s = lax.dot_general(qs_scr[rs, :], kv, NT, preferred_element_type=F32)
        yield
        if masked:
            qpos = i * tq + lax.broadcasted_iota(jnp.int32, s.shape, 0) % tq
            kpos = j * tk + lax.broadcasted_iota(jnp.int32, s.shape, 1)
            s = jnp.where(kpos <= qpos, s, -jnp.inf)
        m_prev = m_scr[rs, :]
        m_new = jnp.maximum(m_prev, _row_reduce(s, jnp.maximum, jnp.max))
        alpha = jnp.exp(m_prev - m_new)
        p = jnp.exp(s - m_new)
        l_scr[rs, :] = alpha * l_scr[rs, :] + _row_reduce(p, jnp.add, jnp.sum)
        m_scr[rs, :] = m_new
        pv = jnp.dot(p.astype(BF16), kv[:, :KV_LORA], preferred_element_type=F32)
        yield
        acc_scr[rs, :] = alpha * acc_scr[rs, :] + pv

    @pl.when(unmasked)
    def _():
        _round_robin([chain(c, False) for c in range(nchain)])

    @pl.when(needed & jnp.logical_not(unmasked))
    def _():
        _round_robin([chain(c, True) for c in range(nchain)])

    @pl.when(j == nj - 1)
    def _():
        _unstack_store(o_ref, acc_scr, l_scr, tq)


def _prompt_attn(q, kv, tq, tk):
    b, s, _ = q.shape
    rows = MLA_HEADS * tq
    last = lambda i: (i * tq + tq - 1) // tk
    return pl.pallas_call(
        functools.partial(_prompt_attn_kernel, tq=tq, tk=tk),
        grid=(b, s // tq, s // tk),
        in_specs=[pl.BlockSpec((None, tq, MLA_HEADS * QK_PAD), lambda bb, i, j: (bb, i, 0)),
                  pl.BlockSpec((None, tk, QK_PAD), lambda bb, i, j: (bb, jnp.minimum(j, last(i)), 0))],
        out_specs=pl.BlockSpec((None, tq, MLA_HEADS * KV_LORA), lambda bb, i, j: (bb, i, 0)),
        out_shape=jax.ShapeDtypeStruct((b, s, MLA_HEADS * KV_LORA), BF16),
        scratch_shapes=[pltpu.VMEM((rows, QK_PAD), BF16), pltpu.VMEM((rows, 1), F32),
                        pltpu.VMEM((rows, 1), F32), pltpu.VMEM((rows, KV_LORA), F32)],
        compiler_params=_cparams(("arbitrary", "arbitrary", "arbitrary")),
        name="prompt_attn",
    )(q, kv)


def _paged_pages_chain(qs_scr, ckv_refs, kr_refs, m_scr, l_scr, acc_scr):
    qs = qs_scr[...]
    q_lat = qs[:, :KV_LORA]
    q_rope = qs[:, KV_LORA:KV_LORA + QK_ROPE]
    pages, scores = [], []
    for ckv_ref, kr_ref in zip(ckv_refs, kr_refs):
        ckv = ckv_ref[...].astype(BF16)
        kr_t = kr_ref[...].astype(BF16)
        pages.append(ckv)
        scores.append(lax.dot_general(q_lat, ckv, NT, preferred_element_type=F32)
                      + jnp.dot(q_rope, kr_t, preferred_element_type=F32))
    yield
    s = jnp.concatenate(scores, axis=1)
    m_prev = m_scr[...]
    m_new = jnp.maximum(m_prev, _row_reduce(s, jnp.maximum, jnp.max))
    alpha = jnp.exp(m_prev - m_new)
    p = jnp.exp(s - m_new)
    l_scr[...] = alpha * l_scr[...] + _row_reduce(p, jnp.add, jnp.sum)
    m_scr[...] = m_new
    pb = p.astype(BF16)
    w = s.shape[1] // len(pages)
    pv = sum(jnp.dot(pb[:, i * w:(i + 1) * w], vi, preferred_element_type=F32) for i, vi in enumerate(pages))
    yield
    acc_scr[...] = alpha * acc_scr[...] + pv


def _paged_attn_kernel(pt_ref, q_ref, kvnew_ref, ckv_hbm, kr_hbm, o_ref,
                       qs_scr, m_scr, l_scr, acc_scr, ckv_buf, kr_buf, sem, *, pages_per_step, n_pages, t):
    nb = q_ref.shape[0]
    npg = nb * pages_per_step
    bb = pl.program_id(0)
    g = pl.program_id(1)
    ng = pl.num_programs(1)
    step = bb * ng + g
    slot = step % 2

    def fetch(bb_, g_, into):
        for bi in range(nb):
            for p in range(pages_per_step):
                pid = pt_ref[(bb_ * nb + bi) * n_pages + g_ * pages_per_step + p]
                j = bi * pages_per_step + p
                pltpu.make_async_copy(ckv_hbm.at[pid], ckv_buf.at[into, j], sem.at[0, into]).start(priority=j % 2)
                pltpu.make_async_copy(kr_hbm.at[pid], kr_buf.at[into, j], sem.at[1, into]).start(priority=j % 2)

    @pl.when(step == 0)
    def _():
        fetch(bb, g, 0)

    @pl.when(step + 1 < pl.num_programs(0) * ng)
    def _():
        wrap = g + 1 == ng
        fetch(jnp.where(wrap, bb + 1, bb), jnp.where(wrap, 0, g + 1), 1 - slot)

    @pl.when(g == 0)
    def _():
        for bi in range(nb):
            qs_scr[bi] = _stack_heads(q_ref[bi])
        m_scr[...] = jnp.full(m_scr.shape, -jnp.inf, F32)
        l_scr[...] = jnp.zeros(l_scr.shape, F32)
        acc_scr[...] = jnp.zeros(acc_scr.shape, F32)

    pltpu.make_async_copy(ckv_hbm.at[pl.ds(0, npg)], ckv_buf.at[slot], sem.at[0, slot]).wait()
    pltpu.make_async_copy(kr_hbm.at[pl.ds(0, npg)], kr_buf.at[slot], sem.at[1, slot]).wait()

    _round_robin([
        _paged_pages_chain(qs_scr.at[bi],
                           [ckv_buf.at[slot, bi * pages_per_step + p] for p in range(pages_per_step)],
                           [kr_buf.at[slot, bi * pages_per_step + p] for p in range(pages_per_step)],
                           m_scr.at[bi], l_scr.at[bi], acc_scr.at[bi])
        for bi in range(nb)])

    @pl.when(g == ng - 1)
    def _():
        for bi in range(nb):
            kvn = kvnew_ref[bi]
            s = lax.dot_general(qs_scr[bi], kvn, NT, preferred_element_type=F32)
            qpos = lax.broadcasted_iota(jnp.int32, s.shape, 0) % t
            kpos = lax.broadcasted_iota(jnp.int32, s.shape, 1)
            s = jnp.where(kpos <= qpos, s, -jnp.inf)
            _online_softmax_step(s, kvn[:, :KV_LORA], m_scr.at[bi], l_scr.at[bi], acc_scr.at[bi])
            _unstack_store(o_ref.at[bi], acc_scr.at[bi], l_scr.at[bi], t)


def _paged_attn(q, kvnew, cache_ckv, cache_krope_t, page_table, pages_per_step):
    b, t, _ = q.shape
    n_pages = page_table.shape[1]
    page = cache_ckv.shape[1]
    assert n_pages % pages_per_step == 0
    rows = MLA_HEADS * t
    nb = next(c for c in (4, 2, 1) if b % c == 0)

    npg = nb * pages_per_step
    assert cache_ckv.shape[0] >= npg
    grid_spec = pltpu.PrefetchScalarGridSpec(
        num_scalar_prefetch=1,
        grid=(b // nb, n_pages // pages_per_step),
        in_specs=[pl.BlockSpec((nb, t, MLA_HEADS * QK_PAD), lambda bb, g, pt: (bb, 0, 0)),
                  pl.BlockSpec((nb, t, QK_PAD), lambda bb, g, pt: (bb, 0, 0)),
                  pl.BlockSpec(memory_space=pl.ANY), pl.BlockSpec(memory_space=pl.ANY)],
        out_specs=pl.BlockSpec((nb, t, MLA_HEADS * KV_LORA), lambda bb, g, pt: (bb, 0, 0)),
        scratch_shapes=[pltpu.VMEM((nb, rows, QK_PAD), BF16), pltpu.VMEM((nb, rows, 1), F32),
                        pltpu.VMEM((nb, rows, 1), F32), pltpu.VMEM((nb, rows, KV_LORA), F32),
                        pltpu.VMEM((2, npg, page, KV_LORA), cache_ckv.dtype),
                        pltpu.VMEM((2, npg, QK_ROPE, page), cache_krope_t.dtype),
                        pltpu.SemaphoreType.DMA((2, 2))],
    )
    return pl.pallas_call(
        functools.partial(_paged_attn_kernel, pages_per_step=pages_per_step, n_pages=n_pages, t=t),
        grid_spec=grid_spec,
        out_shape=jax.ShapeDtypeStruct((b, t, MLA_HEADS * KV_LORA), BF16),
        compiler_params=_cparams(("arbitrary", "arbitrary")),
        name="paged_attn",
    )(page_table.reshape(-1), q, kvnew, cache_ckv, cache_krope_t)


def _pack_bf16_pairs(x):
    w = x.shape[1] // 2
    xb = x.astype(BF16).astype(F32)
    lo = lax.shift_right_logical(lax.bitcast_convert_type(xb[:, :w], jnp.uint32), jnp.uint32(16))
    hi = lax.bitcast_convert_type(xb[:, w:], jnp.uint32) & jnp.uint32(0xFFFF0000)
    return lo | hi


def _store_row_slabs(ref, first, x):
    t, w = x.shape
    per = w // LANE
    for j in range(per):
        ref[pl.ds(first * per + j, t, stride=per), :] = x[:, j * LANE:(j + 1) * LANE]


def _load_row_slabs(ref, first, t, w):
    per = w // LANE
    return jnp.concatenate([ref[pl.ds(first * per + j, t, stride=per), :] for j in range(per)], axis=1)


def _unpack_bf16_pairs(p):
    lo = lax.bitcast_convert_type(lax.shift_left(p, jnp.uint32(16)), F32)
    hi = lax.bitcast_convert_type(p & jnp.uint32(0xFFFF0000), F32)
    return jnp.concatenate([lo, hi], axis=1)


def _post_mix_kernel(h_ref, ogdn_ref, olat_ref, wuv_ref, wout_ref, gffn_ref, wrh_ref, wrl_ref, rbias_ref,
                     wsgu_ref, wsd_ref,
                     h1_ref, xp_ref, idx_ref, rank_ref, wtok_ref, cnt_ref, carry):
    i = pl.program_id(0)
    tm = h_ref.shape[0]

    @pl.when(i == 0)
    def _():
        carry[...] = jnp.zeros(carry.shape, F32)

    parts = [ogdn_ref[...]]
    for hh in range(MLA_HEADS):
        o_h = jnp.dot(olat_ref[:, hh * KV_LORA:(hh + 1) * KV_LORA], wuv_ref[hh], preferred_element_type=F32)
        parts.append(o_h.astype(BF16))
    mix = jnp.concatenate(parts, axis=1)
    h1 = h_ref[...] + jnp.dot(mix, wout_ref[...], preferred_element_type=F32)
    u2 = _rms(h1, gffn_ref[...])
    u2b = u2.astype(BF16)
    _store_row_slabs(xp_ref, 0, _pack_bf16_pairs(u2))

    u2l = (u2 - u2b.astype(F32)).astype(BF16)
    logits = (lax.dot_general(wrh_ref[...], u2b, NT, preferred_element_type=F32)
              + lax.dot_general(wrh_ref[...], u2l, NT, preferred_element_type=F32)
              + lax.dot_general(wrl_ref[...], u2b, NT, preferred_element_type=F32))
    scores = _sigmoid(logits)
    biased = scores + rbias_ref[...]
    neg = -jnp.inf

    sub = lax.broadcasted_iota(jnp.int32, (GROUP_SIZE, tm), 0)
    gscore = []
    for gi in range(N_GROUPS):
        vg = biased[gi * GROUP_SIZE:(gi + 1) * GROUP_SIZE, :]
        m1 = jnp.max(vg, axis=0, keepdims=True)
        first = jnp.min(jnp.where(vg == m1, sub, GROUP_SIZE), axis=0, keepdims=True)
        m2 = jnp.max(jnp.where(sub == first, neg, vg), axis=0, keepdims=True)
        gscore.append(m1 + m2)
    gs = jnp.concatenate(gscore, axis=0)
    gid = lax.broadcasted_iota(jnp.int32, gs.shape, 0)
    gsel = jnp.zeros(gs.shape, F32)
    for _ in range(TOPK_GROUPS):
        m = jnp.max(gs, axis=0, keepdims=True)
        pick = jnp.min(jnp.where(gs == m, gid, N_GROUPS), axis=0, keepdims=True)
        hit = gid == pick
        gsel = jnp.where(hit, 1.0, gsel)
        gs = jnp.where(hit, neg, gs)
    emask = jnp.concatenate(
        [jnp.broadcast_to(gsel[gi:gi + 1, :], (GROUP_SIZE, tm)) for gi in range(N_GROUPS)], axis=0)
    cand = jnp.where(emask > 0.5, biased, neg)

    eid = lax.broadcasted_iota(jnp.int32, cand.shape, 0)
    chosen = jnp.zeros(cand.shape, F32)
    idx_rows, w_rows, hits = [], [], []
    for _ in range(TOP_K):
        m = jnp.max(cand, axis=0, keepdims=True)
        pick = jnp.min(jnp.where(cand == m, eid, N_EXPERTS), axis=0, keepdims=True)
        hit = eid == pick
        idx_rows.append(pick)
        w_rows.append(jnp.sum(jnp.where(hit, scores, 0.0), axis=0, keepdims=True))
        hits.append(hit)
        chosen = chosen + hit.astype(F32)
        cand = jnp.where(hit, neg, cand)
    wsel = jnp.concatenate(w_rows, axis=0)
    wsel = wsel / (jnp.sum(wsel, axis=0, keepdims=True) + 1e-20) * ROUTED_SCALE
    idx_ref[...] = jnp.concatenate(idx_rows, axis=0)

    ti = lax.broadcasted_iota(jnp.int32, (tm, tm), 0)
    tj = lax.broadcasted_iota(jnp.int32, (tm, tm), 1)
    before = (ti < tj).astype(BF16)
    prefix = jnp.dot(chosen.astype(BF16), before, preferred_element_type=F32) + carry[:, 0:1]
    rank_rows = [jnp.sum(jnp.where(hit, prefix, 0.0), axis=0, keepdims=True) for hit in hits]
    rank_ref[...] = jnp.concatenate(rank_rows, axis=0).astype(jnp.int32)
    carry[...] = carry[...] + jnp.sum(chosen, axis=1, keepdims=True)
    cnt_ref[...] = carry[...]

    wpad = jnp.concatenate([wsel, jnp.zeros((LANE - TOP_K, tm), F32)], axis=0)
    wtok_ref[...] = wpad.T

    gu = jnp.dot(u2b, wsgu_ref[...], preferred_element_type=F32)
    hs = (_silu(gu[:, :D_SHARED]) * gu[:, D_SHARED:]).astype(BF16)
    h1_ref[...] = h1 + jnp.dot(hs, wsd_ref[...], preferred_element_type=F32)


def _post_mix(h, o_gdn, o_lat, wts, tm):
    n, d = h.shape
    tok = lambda w: pl.BlockSpec((tm, w), lambda i: (i, 0))
    tr = lambda r: pl.BlockSpec((r, tm), lambda i: (0, i))
    return pl.pallas_call(
        _post_mix_kernel,
        grid=(n // tm,),
        in_specs=[tok(d), tok(GDN_V_W), tok(MLA_HEADS * KV_LORA), _full(wts['wuv'].shape), _full((d, d)),
                  _full((1, d)), _full((N_EXPERTS, d)), _full((N_EXPERTS, d)), _full((N_EXPERTS, 1)),
                  _full((d, 2 * D_SHARED)), _full((D_SHARED, d))],
        out_specs=[tok(d), pl.BlockSpec((tm * ROW_SLAB, LANE), lambda i: (i, 0)), tr(TOP_K), tr(TOP_K), tok(LANE),
                   _full((N_EXPERTS, LANE))],
        out_shape=(jax.ShapeDtypeStruct((n, d), F32),
                   jax.ShapeDtypeStruct((n * ROW_SLAB, LANE), jnp.uint32),
                   jax.ShapeDtypeStruct((TOP_K, n), jnp.int32),
                   jax.ShapeDtypeStruct((TOP_K, n), jnp.int32),
                   jax.ShapeDtypeStruct((n, LANE), F32),
                   jax.ShapeDtypeStruct((N_EXPERTS, LANE), F32)),
        scratch_shapes=[pltpu.VMEM((N_EXPERTS, LANE), F32)],
        compiler_params=_cparams(("arbitrary",)),
        name="post_mix",
    )(h, o_gdn, o_lat, wts['wuv'], wts['w_out'], wts['g_ffn'], wts['wr_hi'], wts['wr_lo'], wts['rbias'],
      wts['ws_gu'], wts['ws_d'])


def _dest_kernel(idx_ref, rank_ref, starts_ref, o_ref):
    eid = lax.broadcasted_iota(jnp.int32, (N_EXPERTS, idx_ref.shape[1]), 0)
    starts = starts_ref[...]
    rows = []
    for kk in range(TOP_K):
        hit = eid == idx_ref[kk:kk + 1, :]
        rows.append(jnp.sum(jnp.where(hit, starts, 0.0), axis=0, keepdims=True))
    o_ref[...] = jnp.concatenate(rows, axis=0).astype(jnp.int32) + rank_ref[...]


def _dest_rows(idx_t, rank_t, starts, tn):
    k, n = idx_t.shape
    assert n * k < 2 ** 24
    tr = pl.BlockSpec((k, tn), lambda i: (0, i))
    return pl.pallas_call(
        _dest_kernel,
        grid=(n // tn,),
        in_specs=[tr, tr, _full((N_EXPERTS, 1))],
        out_specs=tr,
        out_shape=jax.ShapeDtypeStruct((k, n), jnp.int32),
        compiler_params=_cparams(("arbitrary",)),
        name="dest_rows",
    )(idx_t, rank_t, starts.astype(F32).reshape(N_EXPERTS, 1))


def _slab(ref, row):
    return ref.at[pl.ds(pl.multiple_of(row * ROW_SLAB, ROW_SLAB), ROW_SLAB)]


def _scatter_kernel(dest_ref, x_ref, o_hbm, sem):
    fan, nrow = dest_ref.shape

    def start(r, carry):
        for kk in range(fan):
            pltpu.make_async_copy(_slab(x_ref, r), _slab(o_hbm, dest_ref[kk, r]), sem).start(priority=kk % 2)
        return carry

    lax.fori_loop(0, nrow, start, 0)
    whole = o_hbm.at[pl.ds(0, fan * nrow * ROW_SLAB)]
    pltpu.make_async_copy(whole, whole, sem).wait()


def _scatter_rows(x, dest_t, ts):
    w = x.shape[1]
    n = x.shape[0] // ROW_SLAB
    fan = dest_t.shape[0]
    return pl.pallas_call(
        _scatter_kernel,
        grid=(n // ts,),
        in_specs=[pl.BlockSpec((fan, ts), lambda i: (0, i), memory_space=pltpu.SMEM),
                  pl.BlockSpec((ts * ROW_SLAB, w), lambda i: (i, 0))],
        out_specs=pl.BlockSpec(memory_space=pl.ANY),
        out_shape=jax.ShapeDtypeStruct((n * fan * ROW_SLAB, w), x.dtype),
        scratch_shapes=[pltpu.SemaphoreType.DMA],
        compiler_params=pltpu.CompilerParams(dimension_semantics=("arbitrary",), has_side_effects=True),
        name="scatter_rows",
    )(dest_t, x)


def _experts_kernel(vt_ref, ve_ref, seg_ref, nv_ref, x_ref, wgu_ref, wd_ref, o_ref, *, bm):
    v = pl.program_id(0)
    tile = vt_ref[v]
    e = ve_ref[v]
    prev_tile = vt_ref[jnp.maximum(v - 1, 0)]
    first = (v == 0) | (tile != prev_tile)

    @pl.when(v < nv_ref[0])
    def _():
        w = ROW_SLAB * LANE
        nchain = 2 if bm % (2 * SUBLANE) == 0 else 1
        half = bm // nchain

        def ffn(c):
            x = _unpack_bf16_pairs(_load_row_slabs(x_ref, c * half, half, w)).astype(BF16)
            gu = jnp.dot(x, wgu_ref[...], preferred_element_type=F32)
            yield
            hid = (_silu(gu[:, :D_EXPERT]) * gu[:, D_EXPERT:]).astype(BF16)
            y = jnp.dot(hid, wd_ref[...], preferred_element_type=F32)
            yield
            return _pack_bf16_pairs(y)

        ys = _round_robin([ffn(c) for c in range(nchain)])
        lo = seg_ref[e] - tile * bm
        hi = seg_ref[e + 1] - tile * bm

        def mine(c):
            rowid = c * half + lax.broadcasted_iota(jnp.int32, (half, w), 0)
            return (rowid >= lo) & (rowid < hi)

        @pl.when(first)
        def _():
            for c in range(nchain):
                _store_row_slabs(o_ref, c * half, jnp.where(mine(c), ys[c], jnp.uint32(0)))

        @pl.when(jnp.logical_not(first))
        def _():
            for c in range(nchain):
                _store_row_slabs(o_ref, c * half, jnp.where(mine(c), ys[c], _load_row_slabs(o_ref, c * half, half, w)))


def _experts(xs, visit_tile, visit_expert, seg, n_visits, wgu, wd, bm):
    rows, w = xs.shape
    nv_max = visit_tile.shape[0]
    d = wd.shape[2]
    grid_spec = pltpu.PrefetchScalarGridSpec(
        num_scalar_prefetch=4,
        grid=(nv_max,),
        in_specs=[pl.BlockSpec((bm * ROW_SLAB, w), lambda v, vt, ve, sg, nv: (vt[v], 0)),
                  pl.BlockSpec((None, d, 2 * D_EXPERT), lambda v, vt, ve, sg, nv: (ve[v], 0, 0)),
                  pl.BlockSpec((None, D_EXPERT, d), lambda v, vt, ve, sg, nv: (ve[v], 0, 0))],
        out_specs=pl.BlockSpec((bm * ROW_SLAB, w), lambda v, vt, ve, sg, nv: (vt[v], 0)),
    )
    return pl.pallas_call(
        functools.partial(_experts_kernel, bm=bm),
        grid_spec=grid_spec,
        out_shape=jax.ShapeDtypeStruct((rows, w), jnp.uint32),
        compiler_params=_cparams(("arbitrary",)),
        name="experts",
    )(visit_tile, visit_expert, seg, n_visits, xs, wgu, wd)


def _visit_list(counts, rows, bm):
    ntiles = rows // bm
    nv_max = ntiles + N_EXPERTS - 1
    ends = jnp.cumsum(counts)
    starts = ends - counts
    first_tile = starts // bm
    ntile_e = jnp.where(counts > 0, (ends - 1) // bm - first_tile + 1, 0)
    vend = jnp.cumsum(ntile_e)
    vstart = vend - ntile_e
    n_visits = vend[-1]
    v = jnp.arange(nv_max, dtype=jnp.int32)
    ve = jnp.minimum(jnp.sum(vend[None, :] <= v[:, None], axis=1), N_EXPERTS - 1).astype(jnp.int32)
    vt = (first_tile[ve] + v - vstart[ve]).astype(jnp.int32)
    last = jnp.maximum(n_visits - 1, 0)
    ve = jnp.where(v < n_visits, ve, ve[last])
    vt = jnp.where(v < n_visits, vt, vt[last])
    seg = jnp.concatenate([starts, ends[-1:]]).astype(jnp.int32)
    return vt, ve, seg, n_visits.reshape(1).astype(jnp.int32), starts


def _final_kernel(dcur_ref, dnext_ref, h1_ref, wtok_ref, p_ref, gple_ref, wgate_ref, wproj_ref, gfin_ref,
                  ys_hbm, o_ref, ybuf, h2_scr, sem):
    i = pl.program_id(0)
    nsteps = pl.num_programs(0)
    tm = h1_ref.shape[0]
    slot = i % 2
    grp = SUBLANE

    def fetch_rows(dest_ref, into, r0):
        for dr in range(grp):
            for kk in range(TOP_K):
                r = r0 + dr
                pltpu.make_async_copy(_slab(ys_hbm, dest_ref[kk, r]), _slab(ybuf.at[into], kk * tm + r),
                                      sem.at[into]).start(priority=kk % 2)

    def combine_rows(r0):
        rows = pl.ds(r0, grp)
        wtok = wtok_ref[rows, :]
        h2 = h1_ref[rows, :]
        for kk in range(TOP_K):
            yk = _unpack_bf16_pairs(_load_row_slabs(ybuf.at[slot], kk * tm + r0, grp, ROW_SLAB * LANE))
            h2 = h2 + yk * wtok[:, kk:kk + 1]
        h2_scr[rows, :] = h2

    def row_loop(body):
        def step(g, carry):
            body(pl.multiple_of(g * grp, grp))
            return carry
        lax.fori_loop(0, tm // grp, step, 0)

    @pl.when(i == 0)
    def _():
        row_loop(lambda r0: fetch_rows(dcur_ref, 0, r0))

    pltpu.make_async_copy(ys_hbm.at[pl.ds(0, TOP_K * tm * ROW_SLAB)], ybuf.at[slot], sem.at[slot]).wait()

    @pl.when(i + 1 < nsteps)
    def _():
        def both(r0):
            fetch_rows(dnext_ref, 1 - slot, r0)
            combine_rows(r0)
        row_loop(both)

    @pl.when(i + 1 == nsteps)
    def _():
        row_loop(combine_rows)

    h2 = h2_scr[...]
    n = _rms(h2, gple_ref[...]).astype(BF16)
    gate = _sigmoid(jnp.dot(n, wgate_ref[...], preferred_element_type=F32))
    pp = jnp.dot(p_ref[...].astype(BF16), wproj_ref[...], preferred_element_type=F32)
    o_ref[...] = _rms(h2 + gate * pp, gfin_ref[...])


def _final(h1, ys, dest_t, wtok, p, wts, tm):
    n, d = h1.shape
    nsteps = n // tm
    assert ys.shape[0] >= TOP_K * tm * ROW_SLAB
    tok = lambda w: pl.BlockSpec((tm, w), lambda i: (i, 0))
    dspec = lambda f: pl.BlockSpec((TOP_K, tm), f, memory_space=pltpu.SMEM)
    return pl.pallas_call(
        _final_kernel,
        grid=(nsteps,),
        in_specs=[dspec(lambda i: (0, i)), dspec(lambda i: (0, jnp.minimum(i + 1, nsteps - 1))),
                  tok(d), tok(LANE), tok(p.shape[1]), _full((1, d)), _full((d, d)),
                  _full((p.shape[1], d)), _full((1, d)), pl.BlockSpec(memory_space=pl.ANY)],
        out_specs=tok(d),
        out_shape=jax.ShapeDtypeStruct((n, d), F32),
        scratch_shapes=[pltpu.VMEM((2, TOP_K * tm * ROW_SLAB, ys.shape[1]), ys.dtype), pltpu.VMEM((tm, d), F32),
                        pltpu.SemaphoreType.DMA((2,))],
        compiler_params=_cparams(("arbitrary",)),
        name="final",
    )(dest_t, dest_t, h1, wtok, p, wts['g_ple'], wts['w_ple_gate'], wts['w_ple_proj'], wts['g_final'], ys)


def _rotate_half_cols(w):
    half = QK_ROPE // 2
    wh = w.reshape(w.shape[0], -1, 2, half)
    return jnp.concatenate([-wh[:, :, 1:2], wh[:, :, 0:1]], axis=2).reshape(w.shape)


def _prep_weights(g_mix, w_in, gdn_conv_w, gdn_a_log, gdn_dt_bias, gdn_norm, mla_g_q, mla_w_uq, mla_g_kv,
                  mla_w_uk, mla_w_uv, w_out, g_ffn, w_router, router_bias, w_exp_gate, w_exp_up, w_exp_down,
                  w_sh_gate, w_sh_up, w_sh_down, g_ple, w_ple_gate, w_ple_proj, g_final):
    d = w_in.shape[0]
    splits = np.cumsum([CONV_CH, GDN_V_W, GDN_HEADS, GDN_HEADS, Q_LORA, KV_LORA])
    w_qkv, w_z, w_b, w_a, w_cq, w_ckv, w_kr = jnp.split(w_in, [int(s) for s in splits], axis=1)
    ba_pad = jnp.zeros((d, LANE - 2 * GDN_HEADS), w_in.dtype)
    w1 = jnp.concatenate([w_qkv, w_z, w_cq, w_ckv, w_kr, _rotate_half_cols(w_kr), w_b, w_a, ba_pad], axis=1)
    assert w1.shape[1] == W1_WIDTH
    lane_pad = lambda v: jnp.pad(v.astype(F32), (GDN_HEADS, LANE - 2 * GDN_HEADS)).reshape(1, LANE)
    uq = mla_w_uq.reshape(Q_LORA, MLA_HEADS, QK_NOPE + QK_ROPE)
    uq_nope = uq[:, :, :QK_NOPE].reshape(Q_LORA, MLA_HEADS * QK_NOPE)
    uq_rope = uq[:, :, QK_NOPE:].reshape(Q_LORA, MLA_HEADS * QK_ROPE)
    wuq = jnp.concatenate([uq_nope, uq_rope, _rotate_half_cols(uq_rope)], axis=1)
    wr_t = w_router.T.astype(F32)
    wr_hi = wr_t.astype(BF16)
    row = lambda v: v.astype(F32).reshape(1, -1)
    return dict(
        g_mix=row(g_mix), w1=w1.astype(BF16), alog=lane_pad(gdn_a_log), dtb=lane_pad(gdn_dt_bias),
        g_q=row(mla_g_q), wuq=wuq.astype(BF16), wuk=jnp.transpose(mla_w_uk, (1, 2, 0)).astype(BF16),
        g_kv=row(mla_g_kv),
        conv_w=jnp.pad(gdn_conv_w.astype(F32), ((0, SUBLANE - CONV_W), (0, 0))), gdn_norm=row(gdn_norm),
        wuv=jnp.transpose(mla_w_uv, (1, 0, 2)).astype(BF16), w_out=w_out.astype(BF16), g_ffn=row(g_ffn),
        wr_hi=wr_hi, wr_lo=(wr_t - wr_hi.astype(F32)).astype(BF16),
        rbias=router_bias.astype(F32).reshape(N_EXPERTS, 1),
        ws_gu=jnp.concatenate([w_sh_gate, w_sh_up], axis=1).astype(BF16), ws_d=w_sh_down.astype(BF16),
        we_gu=jnp.concatenate([w_exp_gate, w_exp_up], axis=2).astype(BF16), we_d=w_exp_down.astype(BF16),
        g_ple=row(g_ple), w_ple_gate=w_ple_gate.astype(BF16), w_ple_proj=w_ple_proj.astype(BF16),
        g_final=row(g_final),
    )


def _rope_tables(pos, reps):
    half = QK_ROPE // 2
    inv = ROPE_THETA ** (-jnp.arange(half, dtype=F32) / half)
    ang = pos.astype(F32)[:, None] * inv[None, :]
    cos = jnp.concatenate([jnp.cos(ang)] * 2, axis=1)
    sin = jnp.concatenate([jnp.sin(ang)] * 2, axis=1)
    cos = jnp.tile(jnp.tile(cos, (1, MLA_HEADS)), (reps, 1))
    sin = jnp.tile(jnp.tile(sin, (1, MLA_HEADS)), (reps, 1))
    return cos, sin


def _tile(n, pref):
    t = min(n, pref)
    assert n % t == 0
    return t


def _layer(x, p, pos, gdn_state, conv_buf, attend, wts, chunk):
    b, t, d = x.shape
    n = b * t
    tm = _tile(n, 512)
    if t >= tm:
        assert t % tm == 0
        cos, sin = _rope_tables(pos, 1)
    else:
        assert tm % t == 0
        cos, sin = _rope_tables(pos, tm // t)
    x2 = x.reshape(n, d)
    qkv, z, gb, ckv, krope, kv, q = _in_proj(x2, wts, cos, sin, tm)

    conv0 = jnp.pad(conv_buf.astype(F32), ((0, 0), (SUBLANE - (CONV_W - 1), 0), (0, 0)))
    o_gdn, gdn_new = _gdn(qkv.reshape(b, t, CONV_CH), z.reshape(b, t, GDN_V_W), gb.reshape(b, t, LANE),
                          gdn_state.astype(F32), conv0, wts['conv_w'], wts['gdn_norm'], chunk)
    conv_new = qkv.reshape(b, t, CONV_CH)[:, t - (CONV_W - 1):, :]

    o_lat = attend(q.reshape(b, t, MLA_HEADS * QK_PAD), kv.reshape(b, t, QK_PAD))

    h1, xp, idx_t, rank_t, wtok, counts = _post_mix(x2, o_gdn.reshape(n, GDN_V_W),
                                                    o_lat.reshape(n, MLA_HEADS * KV_LORA), wts, tm)
    rows = n * TOP_K
    bm = _tile(rows, 512)
    cnt = counts[:, 0].astype(jnp.int32)
    vt, ve, seg, n_visits, starts = _visit_list(cnt, rows, bm)
    dest_t = _dest_rows(idx_t, rank_t, starts, _tile(n, 2048))
    xs = _scatter_rows(xp, dest_t, tm)
    ys = _experts(xs, vt, ve, seg, n_visits, wts['we_gu'], wts['we_d'], bm)
    y = _final(h1, ys, dest_t, wtok, p.reshape(n, -1), wts, tm)
    return y.reshape(b, t, d), ckv.reshape(b, t, KV_LORA), krope.reshape(b, t, QK_ROPE), gdn_new, conv_new


def kernel(x_prompt, x_sample, p_prompt, p_sample, cache_ckv, cache_krope, state_gdn, state_conv, page_table,
           g_mix, w_in, gdn_conv_w, gdn_a_log, gdn_dt_bias, gdn_norm, mla_g_q, mla_w_uq, mla_g_kv, mla_w_uk,
           mla_w_uv, w_out, g_ffn, w_router, router_bias, w_exp_gate, w_exp_up, w_exp_down, w_sh_gate,
           w_sh_up, w_sh_down, g_ple, w_ple_gate, w_ple_proj, g_final):
    depth = w_in.shape[0]
    assert depth == 1, "the final norm is fused into the (single) layer"
    bp, tp, _ = x_prompt.shape
    bs, ts, _ = x_sample.shape
    past = page_table.shape[1] * cache_ckv.shape[2]
    wts = _prep_weights(g_mix[0], w_in[0], gdn_conv_w[0], gdn_a_log[0], gdn_dt_bias[0], gdn_norm[0], mla_g_q[0],
                        mla_w_uq[0], mla_g_kv[0], mla_w_uk[0], mla_w_uv[0], w_out[0], g_ffn[0], w_router[0],
                        router_bias[0], w_exp_gate[0], w_exp_up[0], w_exp_down[0], w_sh_gate[0], w_sh_up[0],
                        w_sh_down[0], g_ple[0], w_ple_gate[0], w_ple_proj[0], g_final)

    cache_krope_t = jnp.swapaxes(cache_krope[0], 1, 2)

    def attend_s(q, kv):
        return _paged_attn(q, kv, cache_ckv[0], cache_krope_t, page_table,
                           pages_per_step=_tile(page_table.shape[1], 16))

    ys, c2, k2, s2, v2 = _layer(x_sample, p_sample[0], past + jnp.arange(ts), state_gdn[0], state_conv[0],
                                attend_s, wts, chunk=ts)

    gdn0 = jnp.zeros((bp, GDN_HEADS, GDN_DK, GDN_DV), state_gdn.dtype)
    conv0 = jnp.zeros((bp, CONV_W - 1, CONV_CH), state_conv.dtype)
    attend_p = functools.partial(_prompt_attn, tq=_tile(tp, 256), tk=_tile(tp, 512))
    yp, c1, k1, s1, v1 = _layer(x_prompt, p_prompt[0], jnp.arange(tp), gdn0, conv0, attend_p, wts,
                                chunk=_tile(tp, 64))
    st = lambda a, ref: a.astype(ref.dtype)[None]
    return (yp, ys, c1[None], k1[None], st(s1, state_gdn), st(v1, state_conv),
            c2[None], k2[None], st(s2, state_gdn), st(v2, state_conv))
```

```python
import functools

import jax
import jax.numpy as jnp
import numpy as np
from jax import lax
from jax.experimental import pallas as pl
from jax.experimental.pallas import tpu as pltpu

EPS = 1e-6
ROPE_THETA = 10000.0
GDN_HEADS = 4
GDN_DK = 128
GDN_DV = 128
CONV_W = 4
MLA_HEADS = 4
Q_LORA = 256
KV_LORA = 256
QK_NOPE = 128
QK_ROPE = 64
V_DIM = 128
N_EXPERTS = 64
TOP_K = 8
N_GROUPS = 8
TOPK_GROUPS = 4
GROUP_SIZE = N_EXPERTS // N_GROUPS
D_EXPERT = 256
D_SHARED = 256
ROUTED_SCALE = 2.5
MLA_SCALE = (QK_NOPE + QK_ROPE) ** -0.5

GDN_QK_W = GDN_HEADS * GDN_DK
GDN_V_W = GDN_HEADS * GDN_DV
CONV_CH = 2 * GDN_QK_W + GDN_V_W
LANE = 128
SUBLANE = 8
QK_PAD = KV_LORA + LANE
VMEM_LIMIT = 56 * 1024 * 1024
D_MODEL = 1024
ROW_SLAB = D_MODEL // 2 // LANE

C_QKV = 0
C_Z = C_QKV + CONV_CH
C_CQ = C_Z + GDN_V_W
C_CKV = C_CQ + Q_LORA
C_KR = C_CKV + KV_LORA
C_KRR = C_KR + QK_ROPE
C_BA = C_KRR + QK_ROPE
W1_WIDTH = C_BA + LANE

F32 = jnp.float32
BF16 = jnp.bfloat16
NT = (((1,), (1,)), ((), ()))
TN = (((0,), (0,)), ((), ()))


def _cparams(sem, **flags):
    return pltpu.CompilerParams(dimension_semantics=sem, vmem_limit_bytes=VMEM_LIMIT, flags=flags or None)


def _rms(x, g):
    return x * lax.rsqrt(jnp.mean(x * x, axis=-1, keepdims=True) + EPS) * g


def _sigmoid(x):
    return 1.0 / (1.0 + jnp.exp(-x))


def _silu(x):
    return x * _sigmoid(x)


def _full(shape):
    return pl.BlockSpec(shape, lambda *_: (0,) * len(shape))


def _in_proj_kernel(x_ref, gmix_ref, w1_ref, alog_ref, dtb_ref, gq_ref, wuq_ref, wuk_ref, gkv_ref,
                    cos_ref, sin_ref,
                    qkv_ref, z_ref, gb_ref, ckv_ref, krope_ref, kv_ref, q_ref):
    u = _rms(x_ref[...], gmix_ref[...]).astype(BF16)
    qkv_ref[...] = jnp.dot(u, w1_ref[:, C_QKV:C_Z], preferred_element_type=F32)
    z_ref[...] = jnp.dot(u, w1_ref[:, C_Z:C_CQ], preferred_element_type=F32)
    rest = jnp.dot(u, w1_ref[:, C_CQ:W1_WIDTH], preferred_element_type=F32)
    c_q = rest[:, 0:Q_LORA]
    c_kv = rest[:, C_CKV - C_CQ:C_KR - C_CQ]
    kr2 = rest[:, C_KR - C_CQ:C_BA - C_CQ]
    ba = rest[:, C_BA - C_CQ:]

    lane = lax.broadcasted_iota(jnp.int32, ba.shape, 1)
    beta = _sigmoid(ba)
    sp_in = ba + dtb_ref[...]
    softplus = jnp.maximum(sp_in, 0.0) + jnp.log(1.0 + jnp.exp(-jnp.abs(sp_in)))
    g = -jnp.exp(alog_ref[...]) * softplus
    gb_ref[...] = jnp.where(lane < GDN_HEADS, beta, g)

    cos = cos_ref[...]
    sin = sin_ref[...]
    ckv = _rms(c_kv, gkv_ref[...])
    ckv_ref[...] = ckv
    krope = kr2[:, :QK_ROPE] * cos[:, :QK_ROPE] + kr2[:, QK_ROPE:] * sin[:, :QK_ROPE]
    krope_ref[...] = krope
    zpad = jnp.zeros((ckv.shape[0], LANE - QK_ROPE), F32)
    kv_ref[...] = jnp.concatenate([ckv, krope, zpad], axis=1).astype(BF16)

    cqn = _rms(c_q, gq_ref[...]).astype(BF16)
    qh = jnp.dot(cqn, wuq_ref[...], preferred_element_type=F32)
    nope_w = MLA_HEADS * QK_NOPE
    rope_w = MLA_HEADS * QK_ROPE
    q_rope = qh[:, nope_w:nope_w + rope_w] * cos + qh[:, nope_w + rope_w:] * sin
    parts = []
    for h in range(MLA_HEADS):
        q_nope = qh[:, h * QK_NOPE:(h + 1) * QK_NOPE].astype(BF16)
        q_lat = jnp.dot(q_nope, wuk_ref[h], preferred_element_type=F32)
        parts += [q_lat, q_rope[:, h * QK_ROPE:(h + 1) * QK_ROPE], zpad]
    q_ref[...] = (jnp.concatenate(parts, axis=1) * MLA_SCALE).astype(BF16)


def _in_proj(x, wts, cos, sin, tm):
    n, d = x.shape
    period = cos.shape[0]
    nper = period // tm
    tok = lambda w: pl.BlockSpec((tm, w), lambda i: (i, 0))
    pos = pl.BlockSpec((tm, cos.shape[1]), lambda i: (i % nper, 0))
    out_shapes = (
        jax.ShapeDtypeStruct((n, CONV_CH), F32),
        jax.ShapeDtypeStruct((n, GDN_V_W), F32),
        jax.ShapeDtypeStruct((n, LANE), F32),
        jax.ShapeDtypeStruct((n, KV_LORA), F32),
        jax.ShapeDtypeStruct((n, QK_ROPE), F32),
        jax.ShapeDtypeStruct((n, QK_PAD), BF16),
        jax.ShapeDtypeStruct((n, MLA_HEADS * QK_PAD), BF16),
    )
    return pl.pallas_call(
        _in_proj_kernel,
        grid=(n // tm,),
        in_specs=[tok(d), _full((1, d)), _full(wts['w1'].shape), _full((1, LANE)), _full((1, LANE)),
                  _full((1, Q_LORA)), _full(wts['wuq'].shape), _full(wts['wuk'].shape), _full((1, KV_LORA)),
                  pos, pos],
        out_specs=[tok(CONV_CH), tok(GDN_V_W), tok(LANE), tok(KV_LORA), tok(QK_ROPE), tok(QK_PAD),
                   tok(MLA_HEADS * QK_PAD)],
        out_shape=out_shapes,
        compiler_params=_cparams(("arbitrary",)),
        name="in_proj",
    )(x, wts['g_mix'], wts['w1'], wts['alog'], wts['dtb'], wts['g_q'], wts['wuq'], wts['wuk'], wts['g_kv'],
      cos, sin)


def _gdn_kernel(qkv_ref, z_ref, gb_ref, s0_ref, conv0_ref, *rest, chunk):
    consts = rest[:-4]
    o_ref, sout_ref, s_scr, xbuf = rest[-4:]
    c = pl.program_id(1)

    @pl.when(c == 0)
    def _():
        s_scr[...] = s0_ref[...]
        xbuf[:, 0:SUBLANE, :] = conv0_ref[...]

    cw_ref, gn_ref = consts[0], consts[1]
    nb, rows = qkv_ref.shape[0], qkv_ref.shape[1]
    chains = {}
    for bi in range(nb):
        y = _gdn_conv(qkv_ref.at[bi], cw_ref, xbuf.at[bi])
        for ci in range(rows // chunk):
            tok = slice(ci * chunk, (ci + 1) * chunk)
            chains[bi, ci] = _gdn_chunk_prep(y[tok], gb_ref[bi, tok, :], *consts[2:], chunk=chunk)
    prepped = dict(zip(chains, _round_robin(list(chains.values()))))
    for ci in range(rows // chunk):
        for bi in range(nb):
            tok = slice(ci * chunk, (ci + 1) * chunk)
            out = _gdn_chunk_state(prepped[bi, ci], z_ref[bi, tok, :], gn_ref, consts[-1], s_scr.at[bi], chunk=chunk)
            for h in range(GDN_HEADS):
                o_ref[bi, tok, h * GDN_DV:(h + 1) * GDN_DV] = out[h * chunk:(h + 1) * chunk, :].astype(o_ref.dtype)

    @pl.when(c == pl.num_programs(1) - 1)
    def _():
        sout_ref[...] = s_scr[...]


def _gdn_conv(qkv_ref, cw_ref, xbuf):
    pre = SUBLANE
    tail = CONV_W - 1
    rows = qkv_ref.shape[0]
    xbuf[pre:pre + rows, :] = qkv_ref[...]
    y = xbuf[pre - tail:pre - tail + rows, :] * cw_ref[0:1, :]
    for j in range(1, CONV_W):
        y = y + xbuf[pre - tail + j:pre - tail + j + rows, :] * cw_ref[j:j + 1, :]
    xbuf[pre - tail:pre, :] = xbuf[pre + rows - tail:pre + rows, :]
    return _silu(y)


def _stack_gdn_heads(a, off):
    return jnp.concatenate([a[:, off + h * GDN_DK:off + (h + 1) * GDN_DK] for h in range(GDN_HEADS)], axis=0)


def _round_robin(gens):
    results = [None] * len(gens)
    live = list(range(len(gens)))
    while live:
        for i in list(live):
            try:
                next(gens[i])
            except StopIteration as done:
                results[i] = done.value
                live.remove(i)
    return results


def _gdn_chunk_prep(y, gb, cmat_ref, gsel_ref, ssel_ref, bsel_ref, place_ref, triu_ref, mask_ref, hsel_ref, *, chunk):
    stack = _stack_gdn_heads
    q = stack(y, 0)
    k = stack(y, GDN_QK_W)
    v = stack(y, 2 * GDN_QK_W)
    qn = q * lax.rsqrt(jnp.sum(q * q, axis=-1, keepdims=True) + EPS) * (GDN_DK ** -0.5)
    kn = k * lax.rsqrt(jnp.sum(k * k, axis=-1, keepdims=True) + EPS)

    r = GDN_HEADS * chunk
    lane_of = lambda a, sel: jnp.sum(a * sel, axis=-1, keepdims=True)
    gb_hi = gb.astype(BF16)
    gb_r1 = gb - gb_hi.astype(F32)
    gb_mid = gb_r1.astype(BF16)
    gb_lo = (gb_r1 - gb_mid.astype(F32)).astype(BF16)
    pieces = (gb_hi, gb_mid, gb_lo)
    cums = sum(jnp.dot(cmat_ref[...], p, preferred_element_type=F32) for p in pieces)
    total = jnp.sum(gb, axis=0, keepdims=True)
    gsel = gsel_ref[...]
    gcol = lane_of(cums, gsel)
    glast = lane_of(total, gsel)
    gstate = lane_of(total, ssel_ref[...])
    beta = lane_of(jnp.concatenate([gb] * GDN_HEADS, axis=0), bsel_ref[...])
    spread = sum(jnp.dot(p, place_ref[...], preferred_element_type=F32) for p in pieces)
    grow = jnp.sum(spread * triu_ref[...], axis=0, keepdims=True)

    incl = mask_ref[0]
    decay = jnp.exp(jnp.where(incl > 0.5, gcol - grow, -jnp.inf))
    kb = kn * beta
    knb = kn.astype(BF16)
    kk = lax.dot_general(kb.astype(BF16), knb, NT, preferred_element_type=F32)
    yield
    lower = kk * decay * mask_ref[1]
    inv = mask_ref[2] - lower * mask_ref[3]
    for lvl in range(4, mask_ref.shape[0]):
        ih = inv.astype(BF16)
        blk = (lower * mask_ref[lvl]).astype(BF16)
        t = jnp.dot(ih, blk, preferred_element_type=F32)
        yield
        c = jnp.dot(t.astype(BF16), ih, preferred_element_type=F32)
        yield
        inv = inv - c
    egc = jnp.exp(gcol)
    rhs = jnp.concatenate([v * beta, kb * egc], axis=1)
    uw = jnp.dot(inv.astype(BF16), rhs.astype(BF16), preferred_element_type=F32)
    yield
    u = uw[:, :GDN_DV]
    w = uw[:, GDN_DV:]
    qk = lax.dot_general(qn.astype(BF16), knb, NT, preferred_element_type=F32) * decay

    hsel = hsel_ref[...]
    wide = lambda a: (jnp.concatenate([a] * GDN_HEADS, axis=1) * hsel).astype(BF16)
    return dict(u=u, w_qe=jnp.concatenate([wide(w), wide(qn * egc)], axis=0), qk=qk.astype(BF16),
                kdec=wide(kn * jnp.exp(glast - gcol)), sdecay=jnp.exp(gstate))


def _gdn_chunk_state(p, z, gn_ref, hsel_ref, s_scr, *, chunk):
    r = GDN_HEADS * chunk
    st = s_scr[...]
    ws_qs = jnp.dot(p['w_qe'], st.astype(BF16), preferred_element_type=F32)
    v_new = (p['u'] - ws_qs[:r]).astype(BF16)
    o = ws_qs[r:] + jnp.dot(p['qk'], v_new, preferred_element_type=F32)
    s_scr[...] = st * p['sdecay'] + lax.dot_general(p['kdec'], v_new, TN, preferred_element_type=F32)
    return _rms(o, gn_ref[...]) * _silu(_stack_gdn_heads(z, 0))


def _gdn_constants(chunk):
    r = GDN_HEADS * chunk
    ri = np.arange(r)
    head, tok = ri // chunk, ri % chunk
    ti = np.arange(chunk)
    lane = np.arange(LANE)
    srow = np.arange(GDN_HEADS * GDN_DK)
    cmat = ti[None, :] <= tok[:, None]
    gsel = lane[None, :] == GDN_HEADS + head[:, None]
    ssel = lane[None, :] == GDN_HEADS + (srow // GDN_DK)[:, None]
    bsel = lane[None, :] == head[:, None]
    place = lane[:, None] == GDN_HEADS + head[None, :]
    triu = ti[:, None] <= tok[None, :]
    same = head[:, None] == head[None, :]
    masks = [same & (tok[:, None] >= tok[None, :]), same & (tok[:, None] > tok[None, :]),
             ri[:, None] == ri[None, :]]
    s = 1
    while s < chunk:
        masks.append(same & ((tok[:, None] // s) % 2 == 1) & ((tok[None, :] // s) == (tok[:, None] // s) - 1))
        s *= 2
    hsel = (np.arange(GDN_HEADS * GDN_DK)[None, :] // GDN_DK) == head[:, None]
    f = lambda a: jnp.asarray(np.asarray(a, np.float32))
    fb = lambda a: f(a).astype(BF16)
    return fb(cmat), f(gsel), f(ssel), f(bsel), fb(place), f(triu), f(np.stack(masks)), f(hsel)


def _gdn(qkv, z, gb, s0, conv0, cw, gn, chunk):
    b, t, _ = qkv.shape
    assert t % chunk == 0 and chunk % SUBLANE == 0
    nb = 2 if b % 2 == 0 else 1
    nch = 2 if (t // chunk) % 2 == 0 else 1
    rows = nch * chunk
    consts = _gdn_constants(chunk)
    tokb = lambda w: pl.BlockSpec((nb, rows, w), lambda i, c: (i, c, 0))
    state = pl.BlockSpec((nb, GDN_HEADS * GDN_DK, GDN_DV), lambda i, c: (i, 0, 0))
    o, s_new = pl.pallas_call(
        functools.partial(_gdn_kernel, chunk=chunk),
        grid=(b // nb, t // rows),
        in_specs=[tokb(CONV_CH), tokb(GDN_V_W), tokb(LANE), state,
                  pl.BlockSpec((nb, SUBLANE, CONV_CH), lambda i, c: (i, 0, 0)),
                  _full((SUBLANE, CONV_CH)), _full((1, GDN_DV))] + [_full(a.shape) for a in consts],
        out_specs=[tokb(GDN_V_W), state],
        out_shape=(jax.ShapeDtypeStruct((b, t, GDN_V_W), BF16),
                   jax.ShapeDtypeStruct((b, GDN_HEADS * GDN_DK, GDN_DV), F32)),
        scratch_shapes=[pltpu.VMEM((nb, GDN_HEADS * GDN_DK, GDN_DV), F32),
                        pltpu.VMEM((nb, SUBLANE + rows, CONV_CH), F32)],
        compiler_params=_cparams(("arbitrary", "arbitrary")),
        name="gdn",
    )(qkv, z, gb, s0.reshape(b, GDN_HEADS * GDN_DK, GDN_DV), conv0, cw, gn, *consts)
    return o, s_new.reshape(b, GDN_HEADS, GDN_DK, GDN_DV)


def _stack_heads(q):
    return jnp.concatenate([q[:, h * QK_PAD:(h + 1) * QK_PAD] for h in range(MLA_HEADS)], axis=0)


def _row_reduce(x, combine, reduce):
    w = x.shape[1]
    if w > LANE and w % LANE == 0:
        tiles = [x[:, i * LANE:(i + 1) * LANE] for i in range(w // LANE)]
        while len(tiles) > 1:
            tiles = [combine(tiles[i], tiles[i + 1]) if i + 1 < len(tiles) else tiles[i]
                     for i in range(0, len(tiles), 2)]
        x = tiles[0]
    return reduce(x, axis=-1, keepdims=True)


def _online_softmax_step(s, vals, m_scr, l_scr, acc_scr):
    m_prev = m_scr[...]
    m_new = jnp.maximum(m_prev, jnp.max(s, axis=-1, keepdims=True))
    alpha = jnp.exp(m_prev - m_new)
    p = jnp.exp(s - m_new)
    l_scr[...] = alpha * l_scr[...] + jnp.sum(p, axis=-1, keepdims=True)
    pb = p.astype(BF16)
    if isinstance(vals, (list, tuple)):
        w = s.shape[1] // len(vals)
        pv = sum(jnp.dot(pb[:, i * w:(i + 1) * w], vi, preferred_element_type=F32) for i, vi in enumerate(vals))
    else:
        pv = jnp.dot(pb, vals, preferred_element_type=F32)
    acc_scr[...] = alpha * acc_scr[...] + pv
    m_scr[...] = m_new


def _unstack_store(o_ref, acc_scr, l_scr, rows):
    o = acc_scr[...] / l_scr[...]
    for h in range(MLA_HEADS):
        o_ref[:, h * KV_LORA:(h + 1) * KV_LORA] = o[h * rows:(h + 1) * rows, :].astype(o_ref.dtype)


def _prompt_attn_kernel(q_ref, kv_ref, o_ref, qs_scr, m_scr, l_scr, acc_scr, *, tq, tk):
    i = pl.program_id(1)
    j = pl.program_id(2)
    nj = pl.num_programs(2)

    @pl.when(j == 0)
    def _():
        qs_scr[...] = _stack_heads(q_ref[...])
        m_scr[...] = jnp.full(m_scr.shape, -jnp.inf, F32)
        l_scr[...] = jnp.zeros(l_scr.shape, F32)
        acc_scr[...] = jnp.zeros(acc_scr.shape, F32)

    needed = j * tk <= i * tq + tq - 1
    unmasked = j * tk + tk - 1 <= i * tq
    nchain = 2
    rows = MLA_HEADS * tq // nchain

    def chain(c, masked):
        rs = pl.ds(c * rows, rows)
        kv = kv_ref[...]
        s = lax.dot_general(qs_scr[rs, :], kv, NT, preferred_element_type=F32)
        yield
        if masked:
            qpos = i * tq + lax.broadcasted_iota(jnp.int32, s.shape, 0) % tq
            kpos = j * tk + lax.broadcasted_iota(jnp.int32, s.shape, 1)
            s = jnp.where(kpos <= qpos, s, -jnp.inf)
        m_prev = m_scr[rs, :]
        m_new = jnp.maximum(m_prev, _row_reduce(s, jnp.maximum, jnp.max))
        alpha = jnp.exp(m_prev - m_new)
        p = jnp.exp(s - m_new)
        l_scr[rs, :] = alpha * l_scr[rs, :] + _row_reduce(p, jnp.add, jnp.sum)
        m_scr[rs, :] = m_new
        pv = jnp.dot(p.astype(BF16), kv[:, :KV_LORA], preferred_element_type=F32)
        yield
        acc_scr[rs, :] = alpha * acc_scr[rs, :] + pv

    @pl.when(unmasked)
    def _():
        _round_robin([chain(c, False) for c in range(nchain)])

    @pl.when(needed & jnp.logical_not(unmasked))
    def _():
        _round_robin([chain(c, True) for c in range(nchain)])

    @pl.when(j == nj - 1)
    def _():
        _unstack_store(o_ref, acc_scr, l_scr, tq)


def _prompt_attn(q, kv, tq, tk):
    b, s, _ = q.shape
    rows = MLA_HEADS * tq
    last = lambda i: (i * tq + tq - 1) // tk
    return pl.pallas_call(
        functools.partial(_prompt_attn_kernel, tq=tq, tk=tk),
        grid=(b, s // tq, s // tk),
        in_specs=[pl.BlockSpec((None, tq, MLA_HEADS * QK_PAD), lambda bb, i, j: (bb, i, 0)),
                  pl.BlockSpec((None, tk, QK_PAD), lambda bb, i, j: (bb, jnp.minimum(j, last(i)), 0))],
        out_specs=pl.BlockSpec((None, tq, MLA_HEADS * KV_LORA), lambda bb, i, j: (bb, i, 0)),
        out_shape=jax.ShapeDtypeStruct((b, s, MLA_HEADS * KV_LORA), BF16),
        scratch_shapes=[pltpu.VMEM((rows, QK_PAD), BF16), pltpu.VMEM((rows, 1), F32),
                        pltpu.VMEM((rows, 1), F32), pltpu.VMEM((rows, KV_LORA), F32)],
        compiler_params=_cparams(("arbitrary", "arbitrary", "arbitrary")),
        name="prompt_attn",
    )(q, kv)


def _paged_pages_chain(qs_scr, ckv_refs, kr_refs, m_scr, l_scr, acc_scr):
    qs = qs_scr[...]
    q_lat = qs[:, :KV_LORA]
    q_rope = qs[:, KV_LORA:KV_LORA + QK_ROPE]
    pages, scores = [], []
    for ckv_ref, kr_ref in zip(ckv_refs, kr_refs):
        ckv = ckv_ref[...].astype(BF16)
        kr_t = kr_ref[...].astype(BF16)
        pages.append(ckv)
        scores.append(lax.dot_general(q_lat, ckv, NT, preferred_element_type=F32)
                      + jnp.dot(q_rope, kr_t, preferred_element_type=F32))
    yield
    s = jnp.concatenate(scores, axis=1)
    m_prev = m_scr[...]
    m_new = jnp.maximum(m_prev, _row_reduce(s, jnp.maximum, jnp.max))
    alpha = jnp.exp(m_prev - m_new)
    p = jnp.exp(s - m_new)
    l_scr[...] = alpha * l_scr[...] + _row_reduce(p, jnp.add, jnp.sum)
    m_scr[...] = m_new
    pb = p.astype(BF16)
    w = s.shape[1] // len(pages)
    pv = sum(jnp.dot(pb[:, i * w:(i + 1) * w], vi, preferred_element_type=F32) for i, vi in enumerate(pages))
    yield
    acc_scr[...] = alpha * acc_scr[...] + pv


def _paged_attn_kernel(pt_ref, q_ref, kvnew_ref, ckv_hbm, kr_hbm, o_ref,
                       qs_scr, m_scr, l_scr, acc_scr, ckv_buf, kr_buf, sem, *, pages_per_step, n_pages, t):
    nb = q_ref.shape[0]
    npg = nb * pages_per_step
    bb = pl.program_id(0)
    g = pl.program_id(1)
    ng = pl.num_programs(1)
    step = bb * ng + g
    slot = step % 2

    def fetch(bb_, g_, into):
        for bi in range(nb):
            for p in range(pages_per_step):
                pid = pt_ref[(bb_ * nb + bi) * n_pages + g_ * pages_per_step + p]
                j = bi * pages_per_step + p
                pltpu.make_async_copy(ckv_hbm.at[pid], ckv_buf.at[into, j], sem.at[0, into]).start(priority=j % 2)
                pltpu.make_async_copy(kr_hbm.at[pid], kr_buf.at[into, j], sem.at[1, into]).start(priority=j % 2)

    @pl.when(step == 0)
    def _():
        fetch(bb, g, 0)

    @pl.when(step + 1 < pl.num_programs(0) * ng)
    def _():
        wrap = g + 1 == ng
        fetch(jnp.where(wrap, bb + 1, bb), jnp.where(wrap, 0, g + 1), 1 - slot)

    @pl.when(g == 0)
    def _():
        for bi in range(nb):
            qs_scr[bi] = _stack_heads(q_ref[bi])
        m_scr[...] = jnp.full(m_scr.shape, -jnp.inf, F32)
        l_scr[...] = jnp.zeros(l_scr.shape, F32)
        acc_scr[...] = jnp.zeros(acc_scr.shape, F32)

    pltpu.make_async_copy(ckv_hbm.at[pl.ds(0, npg)], ckv_buf.at[slot], sem.at[0, slot]).wait()
    pltpu.make_async_copy(kr_hbm.at[pl.ds(0, npg)], kr_buf.at[slot], sem.at[1, slot]).wait()

    _round_robin([
        _paged_pages_chain(qs_scr.at[bi],
                           [ckv_buf.at[slot, bi * pages_per_step + p] for p in range(pages_per_step)],
                           [kr_buf.at[slot, bi * pages_per_step + p] for p in range(pages_per_step)],
                           m_scr.at[bi], l_scr.at[bi], acc_scr.at[bi])
        for bi in range(nb)])

    @pl.when(g == ng - 1)
    def _():
        for bi in range(nb):
            kvn = kvnew_ref[bi]
            s = lax.dot_general(qs_scr[bi], kvn, NT, preferred_element_type=F32)
            qpos = lax.broadcasted_iota(jnp.int32, s.shape, 0) % t
            kpos = lax.broadcasted_iota(jnp.int32, s.shape, 1)
            s = jnp.where(kpos <= qpos, s, -jnp.inf)
            _online_softmax_step(s, kvn[:, :KV_LORA], m_scr.at[bi], l_scr.at[bi], acc_scr.at[bi])
            _unstack_store(o_ref.at[bi], acc_scr.at[bi], l_scr.at[bi], t)


def _paged_attn(q, kvnew, cache_ckv, cache_krope_t, page_table, pages_per_step):
    b, t, _ = q.shape
    n_pages = page_table.shape[1]
    page = cache_ckv.shape[1]
    assert n_pages % pages_per_step == 0
    rows = MLA_HEADS * t
    nb = next(c for c in (4, 2, 1) if b % c == 0)

    npg = nb * pages_per_step
    assert cache_ckv.shape[0] >= npg
    grid_spec = pltpu.PrefetchScalarGridSpec(
        num_scalar_prefetch=1,
        grid=(b // nb, n_pages // pages_per_step),
        in_specs=[pl.BlockSpec((nb, t, MLA_HEADS * QK_PAD), lambda bb, g, pt: (bb, 0, 0)),
                  pl.BlockSpec((nb, t, QK_PAD), lambda bb, g, pt: (bb, 0, 0)),
                  pl.BlockSpec(memory_space=pl.ANY), pl.BlockSpec(memory_space=pl.ANY)],
        out_specs=pl.BlockSpec((nb, t, MLA_HEADS * KV_LORA), lambda bb, g, pt: (bb, 0, 0)),
        scratch_shapes=[pltpu.VMEM((nb, rows, QK_PAD), BF16), pltpu.VMEM((nb, rows, 1), F32),
                        pltpu.VMEM((nb, rows, 1), F32), pltpu.VMEM((nb, rows, KV_LORA), F32),
                        pltpu.VMEM((2, npg, page, KV_LORA), cache_ckv.dtype),
                        pltpu.VMEM((2, npg, QK_ROPE, page), cache_krope_t.dtype),
                        pltpu.SemaphoreType.DMA((2, 2))],
    )
    return pl.pallas_call(
        functools.partial(_paged_attn_kernel, pages_per_step=pages_per_step, n_pages=n_pages, t=t),
        grid_spec=grid_spec,
        out_shape=jax.ShapeDtypeStruct((b, t, MLA_HEADS * KV_LORA), BF16),
        compiler_params=_cparams(("arbitrary", "arbitrary")),
        name="paged_attn",
    )(page_table.reshape(-1), q, kvnew, cache_ckv, cache_krope_t)


def _pack_bf16_pairs(x):
    w = x.shape[1] // 2
    xb = x.astype(BF16).astype(F32)
    lo = lax.shift_right_logical(lax.bitcast_convert_type(xb[:, :w], jnp.uint32), jnp.uint32(16))
    hi = lax.bitcast_convert_type(xb[:, w:], jnp.uint32) & jnp.uint32(0xFFFF0000)
    return lo | hi


def _store_row_slabs(ref, first, x):
    t, w = x.shape
    per = w // LANE
    for j in range(per):
        ref[pl.ds(first * per + j, t, stride=per), :] = x[:, j * LANE:(j + 1) * LANE]


def _load_row_slabs(ref, first, t, w):
    per = w // LANE
    return jnp.concatenate([ref[pl.ds(first * per + j, t, stride=per), :] for j in range(per)], axis=1)


def _unpack_bf16_pairs(p):
    lo = lax.bitcast_convert_type(lax.shift_left(p, jnp.uint32(16)), F32)
    hi = lax.bitcast_convert_type(p & jnp.uint32(0xFFFF0000), F32)
    return jnp.concatenate([lo, hi], axis=1)


def _post_mix_kernel(h_ref, ogdn_ref, olat_ref, wuv_ref, wout_ref, gffn_ref, wrh_ref, wrl_ref, rbias_ref,
                     wsgu_ref, wsd_ref,
                     h1_ref, xp_ref, idx_ref, rank_ref, wtok_ref, cnt_ref, carry):
    i = pl.program_id(0)
    tm = h_ref.shape[0]

    @pl.when(i == 0)
    def _():
        carry[...] = jnp.zeros(carry.shape, F32)

    parts = [ogdn_ref[...]]
    for hh in range(MLA_HEADS):
        o_h = jnp.dot(olat_ref[:, hh * KV_LORA:(hh + 1) * KV_LORA], wuv_ref[hh], preferred_element_type=F32)
        parts.append(o_h.astype(BF16))
    mix = jnp.concatenate(parts, axis=1)
    h1 = h_ref[...] + jnp.dot(mix, wout_ref[...], preferred_element_type=F32)
    u2 = _rms(h1, gffn_ref[...])
    u2b = u2.astype(BF16)
    _store_row_slabs(xp_ref, 0, _pack_bf16_pairs(u2))

    u2l = (u2 - u2b.astype(F32)).astype(BF16)
    logits = (lax.dot_general(wrh_ref[...], u2b, NT, preferred_element_type=F32)
              + lax.dot_general(wrh_ref[...], u2l, NT, preferred_element_type=F32)
              + lax.dot_general(wrl_ref[...], u2b, NT, preferred_element_type=F32))
    scores = _sigmoid(logits)
    biased = scores + rbias_ref[...]
    neg = -jnp.inf

    sub = lax.broadcasted_iota(jnp.int32, (GROUP_SIZE, tm), 0)
    gscore = []
    for gi in range(N_GROUPS):
        vg = biased[gi * GROUP_SIZE:(gi + 1) * GROUP_SIZE, :]
        m1 = jnp.max(vg, axis=0, keepdims=True)
        first = jnp.min(jnp.where(vg == m1, sub, GROUP_SIZE), axis=0, keepdims=True)
        m2 = jnp.max(jnp.where(sub == first, neg, vg), axis=0, keepdims=True)
        gscore.append(m1 + m2)
    gs = jnp.concatenate(gscore, axis=0)
    gid = lax.broadcasted_iota(jnp.int32, gs.shape, 0)
    gsel = jnp.zeros(gs.shape, F32)
    for _ in range(TOPK_GROUPS):
        m = jnp.max(gs, axis=0, keepdims=True)
        pick = jnp.min(jnp.where(gs == m, gid, N_GROUPS), axis=0, keepdims=True)
        hit = gid == pick
        gsel = jnp.where(hit, 1.0, gsel)
        gs = jnp.where(hit, neg, gs)
    emask = jnp.concatenate(
        [jnp.broadcast_to(gsel[gi:gi + 1, :], (GROUP_SIZE, tm)) for gi in range(N_GROUPS)], axis=0)
    cand = jnp.where(emask > 0.5, biased, neg)

    eid = lax.broadcasted_iota(jnp.int32, cand.shape, 0)
    chosen = jnp.zeros(cand.shape, F32)
    idx_rows, w_rows, hits = [], [], []
    for _ in range(TOP_K):
        m = jnp.max(cand, axis=0, keepdims=True)
        pick = jnp.min(jnp.where(cand == m, eid, N_EXPERTS), axis=0, keepdims=True)
        hit = eid == pick
        idx_rows.append(pick)
        w_rows.append(jnp.sum(jnp.where(hit, scores, 0.0), axis=0, keepdims=True))
        hits.append(hit)
        chosen = chosen + hit.astype(F32)
        cand = jnp.where(hit, neg, cand)
    wsel = jnp.concatenate(w_rows, axis=0)
    wsel = wsel / (jnp.sum(wsel, axis=0, keepdims=True) + 1e-20) * ROUTED_SCALE
    idx_ref[...] = jnp.concatenate(idx_rows, axis=0)

    ti = lax.broadcasted_iota(jnp.int32, (tm, tm), 0)
    tj = lax.broadcasted_iota(jnp.int32, (tm, tm), 1)
    before = (ti < tj).astype(BF16)
    prefix = jnp.dot(chosen.astype(BF16), before, preferred_element_type=F32) + carry[:, 0:1]
    rank_rows = [jnp.sum(jnp.where(hit, prefix, 0.0), axis=0, keepdims=True) for hit in hits]
    rank_ref[...] = jnp.concatenate(rank_rows, axis=0).astype(jnp.int32)
    carry[...] = carry[...] + jnp.sum(chosen, axis=1, keepdims=True)
    cnt_ref[...] = carry[...]

    wpad = jnp.concatenate([wsel, jnp.zeros((LANE - TOP_K, tm), F32)], axis=0)
    wtok_ref[...] = wpad.T

    gu = jnp.dot(u2b, wsgu_ref[...], preferred_element_type=F32)
    hs = (_silu(gu[:, :D_SHARED]) * gu[:, D_SHARED:]).astype(BF16)
    h1_ref[...] = h1 + jnp.dot(hs, wsd_ref[...], preferred_element_type=F32)


def _post_mix(h, o_gdn, o_lat, wts, tm):
    n, d = h.shape
    tok = lambda w: pl.BlockSpec((tm, w), lambda i: (i, 0))
    tr = lambda r: pl.BlockSpec((r, tm), lambda i: (0, i))
    return pl.pallas_call(
        _post_mix_kernel,
        grid=(n // tm,),
        in_specs=[tok(d), tok(GDN_V_W), tok(MLA_HEADS * KV_LORA), _full(wts['wuv'].shape), _full((d, d)),
                  _full((1, d)), _full((N_EXPERTS, d)), _full((N_EXPERTS, d)), _full((N_EXPERTS, 1)),
                  _full((d, 2 * D_SHARED)), _full((D_SHARED, d))],
        out_specs=[tok(d), pl.BlockSpec((tm * ROW_SLAB, LANE), lambda i: (i, 0)), tr(TOP_K), tr(TOP_K), tok(LANE),
                   _full((N_EXPERTS, LANE))],
        out_shape=(jax.ShapeDtypeStruct((n, d), F32),
                   jax.ShapeDtypeStruct((n * ROW_SLAB, LANE), jnp.uint32),
                   jax.ShapeDtypeStruct((TOP_K, n), jnp.int32),
                   jax.ShapeDtypeStruct((TOP_K, n), jnp.int32),
                   jax.ShapeDtypeStruct((n, LANE), F32),
                   jax.ShapeDtypeStruct((N_EXPERTS, LANE), F32)),
        scratch_shapes=[pltpu.VMEM((N_EXPERTS, LANE), F32)],
        compiler_params=_cparams(("arbitrary",)),
        name="post_mix",
    )(h, o_gdn, o_lat, wts['wuv'], wts['w_out'], wts['g_ffn'], wts['wr_hi'], wts['wr_lo'], wts['rbias'],
      wts['ws_gu'], wts['ws_d'])


def _dest_kernel(idx_ref, rank_ref, starts_ref, o_ref):
    eid = lax.broadcasted_iota(jnp.int32, (N_EXPERTS, idx_ref.shape[1]), 0)
    starts = starts_ref[...]
    rows = []
    for kk in range(TOP_K):
        hit = eid == idx_ref[kk:kk + 1, :]
        rows.append(jnp.sum(jnp.where(hit, starts, 0.0), axis=0, keepdims=True))
    o_ref[...] = jnp.concatenate(rows, axis=0).astype(jnp.int32) + rank_ref[...]


def _dest_rows(idx_t, rank_t, starts, tn):
    k, n = idx_t.shape
    assert n * k < 2 ** 24
    tr = pl.BlockSpec((k, tn), lambda i: (0, i))
    return pl.pallas_call(
        _dest_kernel,
        grid=(n // tn,),
        in_specs=[tr, tr, _full((N_EXPERTS, 1))],
        out_specs=tr,
        out_shape=jax.ShapeDtypeStruct((k, n), jnp.int32),
        compiler_params=_cparams(("arbitrary",)),
        name="dest_rows",
    )(idx_t, rank_t, starts.astype(F32).reshape(N_EXPERTS, 1))


def _slab(ref, row):
    return ref.at[pl.ds(pl.multiple_of(row * ROW_SLAB, ROW_SLAB), ROW_SLAB)]


def _scatter_kernel(dest_ref, x_ref, o_hbm, sem, *, fan):
    nrow = x_ref.shape[0] // ROW_SLAB
    grp = SUBLANE

    def start(g, carry):
        base = pl.multiple_of(g * grp * fan, grp * fan)
        for dr in range(grp):
            for kk in range(fan):
                pltpu.make_async_copy(_slab(x_ref, g * grp + dr), _slab(o_hbm, dest_ref[0, 0, base + dr * fan + kk]),
                                      sem).start(priority=kk % 2)
        return carry

    lax.fori_loop(0, nrow // grp, start, 0)
    whole = o_hbm.at[pl.ds(0, fan * nrow * ROW_SLAB)]
    pltpu.make_async_copy(whole, whole, sem).wait()


def _scatter_rows(x, dest_tok, ts):
    w = x.shape[1]
    n = x.shape[0] // ROW_SLAB
    fan = dest_tok.shape[2] // ts
    return pl.pallas_call(
        functools.partial(_scatter_kernel, fan=fan),
        grid=(n // ts,),
        in_specs=[pl.BlockSpec((1, 1, ts * fan), lambda i: (i, 0, 0), memory_space=pltpu.SMEM),
                  pl.BlockSpec((ts * ROW_SLAB, w), lambda i: (i, 0))],
        out_specs=pl.BlockSpec(memory_space=pl.ANY),
        out_shape=jax.ShapeDtypeStruct((n * fan * ROW_SLAB, w), x.dtype),
        scratch_shapes=[pltpu.SemaphoreType.DMA],
        compiler_params=pltpu.CompilerParams(dimension_semantics=("arbitrary",), has_side_effects=True),
        name="scatter_rows",
    )(dest_tok, x)


def _experts_kernel(vt_ref, ve_ref, seg_ref, nv_ref, x_ref, wgu_ref, wd_ref, o_ref, *, bm):
    v = pl.program_id(0)
    tile = vt_ref[v]
    e = ve_ref[v]
    prev_tile = vt_ref[jnp.maximum(v - 1, 0)]
    first = (v == 0) | (tile != prev_tile)

    @pl.when(v < nv_ref[0])
    def _():
        w = ROW_SLAB * LANE
        nchain = 2 if bm % (2 * SUBLANE) == 0 else 1
        half = bm // nchain

        def ffn(c):
            x = _unpack_bf16_pairs(_load_row_slabs(x_ref, c * half, half, w)).astype(BF16)
            gu = jnp.dot(x, wgu_ref[...], preferred_element_type=F32)
            yield
            hid = (_silu(gu[:, :D_EXPERT]) * gu[:, D_EXPERT:]).astype(BF16)
            y = jnp.dot(hid, wd_ref[...], preferred_element_type=F32)
            yield
            return _pack_bf16_pairs(y)

        ys = _round_robin([ffn(c) for c in range(nchain)])
        lo = seg_ref[e] - tile * bm
        hi = seg_ref[e + 1] - tile * bm

        def mine(c):
            rowid = c * half + lax.broadcasted_iota(jnp.int32, (half, w), 0)
            return (rowid >= lo) & (rowid < hi)

        @pl.when(first)
        def _():
            for c in range(nchain):
                _store_row_slabs(o_ref, c * half, jnp.where(mine(c), ys[c], jnp.uint32(0)))

        @pl.when(jnp.logical_not(first))
        def _():
            for c in range(nchain):
                _store_row_slabs(o_ref, c * half, jnp.where(mine(c), ys[c], _load_row_slabs(o_ref, c * half, half, w)))


def _experts(xs, visit_tile, visit_expert, seg, n_visits, wgu, wd, bm):
    rows, w = xs.shape
    nv_max = visit_tile.shape[0]
    d = wd.shape[2]
    grid_spec = pltpu.PrefetchScalarGridSpec(
        num_scalar_prefetch=4,
        grid=(nv_max,),
        in_specs=[pl.BlockSpec((bm * ROW_SLAB, w), lambda v, vt, ve, sg, nv: (vt[v], 0)),
                  pl.BlockSpec((None, d, 2 * D_EXPERT), lambda v, vt, ve, sg, nv: (ve[v], 0, 0)),
                  pl.BlockSpec((None, D_EXPERT, d), lambda v, vt, ve, sg, nv: (ve[v], 0, 0))],
        out_specs=pl.BlockSpec((bm * ROW_SLAB, w), lambda v, vt, ve, sg, nv: (vt[v], 0)),
    )
    return pl.pallas_call(
        functools.partial(_experts_kernel, bm=bm),
        grid_spec=grid_spec,
        out_shape=jax.ShapeDtypeStruct((rows, w), jnp.uint32),
        compiler_params=_cparams(("arbitrary",)),
        name="experts",
    )(visit_tile, visit_expert, seg, n_visits, xs, wgu, wd)


def _visit_list(counts, rows, bm):
    ntiles = rows // bm
    nv_max = ntiles + N_EXPERTS - 1
    ends = jnp.cumsum(counts)
    starts = ends - counts
    first_tile = starts // bm
    ntile_e = jnp.where(counts > 0, (ends - 1) // bm - first_tile + 1, 0)
    vend = jnp.cumsum(ntile_e)
    vstart = vend - ntile_e
    n_visits = vend[-1]
    v = jnp.arange(nv_max, dtype=jnp.int32)
    ve = jnp.minimum(jnp.sum(vend[None, :] <= v[:, None], axis=1), N_EXPERTS - 1).astype(jnp.int32)
    vt = (first_tile[ve] + v - vstart[ve]).astype(jnp.int32)
    last = jnp.maximum(n_visits - 1, 0)
    ve = jnp.where(v < n_visits, ve, ve[last])
    vt = jnp.where(v < n_visits, vt, vt[last])
    seg = jnp.concatenate([starts, ends[-1:]]).astype(jnp.int32)
    return vt, ve, seg, n_visits.reshape(1).astype(jnp.int32), starts


def _final_kernel(dcur_ref, dnext_ref, h1_ref, wtok_ref, p_ref, gple_ref, wgate_ref, wproj_ref, gfin_ref,
                  ys_hbm, o_ref, ybuf, h2_scr, sem):
    i = pl.program_id(0)
    nsteps = pl.num_programs(0)
    tm = h1_ref.shape[0]
    slot = i % 2
    grp = SUBLANE

    def fetch_rows(dest_ref, into, r0):
        base = pl.multiple_of(r0 * TOP_K, grp * TOP_K)
        for dr in range(grp):
            for kk in range(TOP_K):
                pltpu.make_async_copy(_slab(ys_hbm, dest_ref[0, 0, base + dr * TOP_K + kk]),
                                      _slab(ybuf.at[into], kk * tm + r0 + dr), sem.at[into]).start(priority=kk % 2)

    def combine_rows(r0):
        rows = pl.ds(r0, grp)
        wtok = wtok_ref[rows, :]
        h2 = h1_ref[rows, :]
        for kk in range(TOP_K):
            yk = _unpack_bf16_pairs(_load_row_slabs(ybuf.at[slot], kk * tm + r0, grp, ROW_SLAB * LANE))
            h2 = h2 + yk * wtok[:, kk:kk + 1]
        h2_scr[rows, :] = h2

    def row_loop(body):
        def step(g, carry):
            body(pl.multiple_of(g * grp, grp))
            return carry
        lax.fori_loop(0, tm // grp, step, 0)

    @pl.when(i == 0)
    def _():
        row_loop(lambda r0: fetch_rows(dcur_ref, 0, r0))

    pltpu.make_async_copy(ys_hbm.at[pl.ds(0, TOP_K * tm * ROW_SLAB)], ybuf.at[slot], sem.at[slot]).wait()

    @pl.when(i + 1 < nsteps)
    def _():
        def both(r0):
            fetch_rows(dnext_ref, 1 - slot, r0)
            combine_rows(r0)
        row_loop(both)

    @pl.when(i + 1 == nsteps)
    def _():
        row_loop(combine_rows)

    h2 = h2_scr[...]
    n = _rms(h2, gple_ref[...]).astype(BF16)
    gate = _sigmoid(jnp.dot(n, wgate_ref[...], preferred_element_type=F32))
    pp = jnp.dot(p_ref[...].astype(BF16), wproj_ref[...], preferred_element_type=F32)
    o_ref[...] = _rms(h2 + gate * pp, gfin_ref[...])


def _final(h1, ys, dest_tok, wtok, p, wts, tm):
    n, d = h1.shape
    nsteps = n // tm
    assert ys.shape[0] >= TOP_K * tm * ROW_SLAB
    tok = lambda w: pl.BlockSpec((tm, w), lambda i: (i, 0))
    dspec = lambda f: pl.BlockSpec((1, 1, tm * TOP_K), f, memory_space=pltpu.SMEM)
    return pl.pallas_call(
        _final_kernel,
        grid=(nsteps,),
        in_specs=[dspec(lambda i: (i, 0, 0)), dspec(lambda i: (jnp.minimum(i + 1, nsteps - 1), 0, 0)),
                  tok(d), tok(LANE), tok(p.shape[1]), _full((1, d)), _full((d, d)),
                  _full((p.shape[1], d)), _full((1, d)), pl.BlockSpec(memory_space=pl.ANY)],
        out_specs=tok(d),
        out_shape=jax.ShapeDtypeStruct((n, d), F32),
        scratch_shapes=[pltpu.VMEM((2, TOP_K * tm * ROW_SLAB, ys.shape[1]), ys.dtype), pltpu.VMEM((tm, d), F32),
                        pltpu.SemaphoreType.DMA((2,))],
        compiler_params=_cparams(("arbitrary",)),
        name="final",
    )(dest_tok, dest_tok, h1, wtok, p, wts['g_ple'], wts['w_ple_gate'], wts['w_ple_proj'], wts['g_final'], ys)


def _rotate_half_cols(w):
    half = QK_ROPE // 2
    wh = w.reshape(w.shape[0], -1, 2, half)
    return jnp.concatenate([-wh[:, :, 1:2], wh[:, :, 0:1]], axis=2).reshape(w.shape)


def _prep_weights(g_mix, w_in, gdn_conv_w, gdn_a_log, gdn_dt_bias, gdn_norm, mla_g_q, mla_w_uq, mla_g_kv,
                  mla_w_uk, mla_w_uv, w_out, g_ffn, w_router, router_bias, w_exp_gate, w_exp_up, w_exp_down,
                  w_sh_gate, w_sh_up, w_sh_down, g_ple, w_ple_gate, w_ple_proj, g_final):
    d = w_in.shape[0]
    splits = np.cumsum([CONV_CH, GDN_V_W, GDN_HEADS, GDN_HEADS, Q_LORA, KV_LORA])
    w_qkv, w_z, w_b, w_a, w_cq, w_ckv, w_kr = jnp.split(w_in, [int(s) for s in splits], axis=1)
    ba_pad = jnp.zeros((d, LANE - 2 * GDN_HEADS), w_in.dtype)
    w1 = jnp.concatenate([w_qkv, w_z, w_cq, w_ckv, w_kr, _rotate_half_cols(w_kr), w_b, w_a, ba_pad], axis=1)
    assert w1.shape[1] == W1_WIDTH
    lane_pad = lambda v: jnp.pad(v.astype(F32), (GDN_HEADS, LANE - 2 * GDN_HEADS)).reshape(1, LANE)
    uq = mla_w_uq.reshape(Q_LORA, MLA_HEADS, QK_NOPE + QK_ROPE)
    uq_nope = uq[:, :, :QK_NOPE].reshape(Q_LORA, MLA_HEADS * QK_NOPE)
    uq_rope = uq[:, :, QK_NOPE:].reshape(Q_LORA, MLA_HEADS * QK_ROPE)
    wuq = jnp.concatenate([uq_nope, uq_rope, _rotate_half_cols(uq_rope)], axis=1)
    wr_t = w_router.T.astype(F32)
    wr_hi = wr_t.astype(BF16)
    row = lambda v: v.astype(F32).reshape(1, -1)
    return dict(
        g_mix=row(g_mix), w1=w1.astype(BF16), alog=lane_pad(gdn_a_log), dtb=lane_pad(gdn_dt_bias),
        g_q=row(mla_g_q), wuq=wuq.astype(BF16), wuk=jnp.transpose(mla_w_uk, (1, 2, 0)).astype(BF16),
        g_kv=row(mla_g_kv),
        conv_w=jnp.pad(gdn_conv_w.astype(F32), ((0, SUBLANE - CONV_W), (0, 0))), gdn_norm=row(gdn_norm),
        wuv=jnp.transpose(mla_w_uv, (1, 0, 2)).astype(BF16), w_out=w_out.astype(BF16), g_ffn=row(g_ffn),
        wr_hi=wr_hi, wr_lo=(wr_t - wr_hi.astype(F32)).astype(BF16),
        rbias=router_bias.astype(F32).reshape(N_EXPERTS, 1),
        ws_gu=jnp.concatenate([w_sh_gate, w_sh_up], axis=1).astype(BF16), ws_d=w_sh_down.astype(BF16),
        we_gu=jnp.concatenate([w_exp_gate, w_exp_up], axis=2).astype(BF16), we_d=w_exp_down.astype(BF16),
        g_ple=row(g_ple), w_ple_gate=w_ple_gate.astype(BF16), w_ple_proj=w_ple_proj.astype(BF16),
        g_final=row(g_final),
    )


def _rope_tables(pos, reps):
    half = QK_ROPE // 2
    inv = ROPE_THETA ** (-jnp.arange(half, dtype=F32) / half)
    ang = pos.astype(F32)[:, None] * inv[None, :]
    cos = jnp.concatenate([jnp.cos(ang)] * 2, axis=1)
    sin = jnp.concatenate([jnp.sin(ang)] * 2, axis=1)
    cos = jnp.tile(jnp.tile(cos, (1, MLA_HEADS)), (reps, 1))
    sin = jnp.tile(jnp.tile(sin, (1, MLA_HEADS)), (reps, 1))
    return cos, sin


def _tile(n, pref):
    t = min(n, pref)
    assert n % t == 0
    return t


def _layer(x, p, pos, gdn_state, conv_buf, attend, wts, chunk):
    b, t, d = x.shape
    n = b * t
    tm = _tile(n, 512)
    if t >= tm:
        assert t % tm == 0
        cos, sin = _rope_tables(pos, 1)
    else:
        assert tm % t == 0
        cos, sin = _rope_tables(pos, tm // t)
    x2 = x.reshape(n, d)
    qkv, z, gb, ckv, krope, kv, q = _in_proj(x2, wts, cos, sin, tm)

    conv0 = jnp.pad(conv_buf.astype(F32), ((0, 0), (SUBLANE - (CONV_W - 1), 0), (0, 0)))
    o_gdn, gdn_new = _gdn(qkv.reshape(b, t, CONV_CH), z.reshape(b, t, GDN_V_W), gb.reshape(b, t, LANE),
                          gdn_state.astype(F32), conv0, wts['conv_w'], wts['gdn_norm'], chunk)
    conv_new = qkv.reshape(b, t, CONV_CH)[:, t - (CONV_W - 1):, :]

    o_lat = attend(q.reshape(b, t, MLA_HEADS * QK_PAD), kv.reshape(b, t, QK_PAD))

    h1, xp, idx_t, rank_t, wtok, counts = _post_mix(x2, o_gdn.reshape(n, GDN_V_W),
                                                    o_lat.reshape(n, MLA_HEADS * KV_LORA), wts, tm)
    rows = n * TOP_K
    bm = _tile(rows, 512)
    cnt = counts[:, 0].astype(jnp.int32)
    vt, ve, seg, n_visits, starts = _visit_list(cnt, rows, bm)
    dest_t = _dest_rows(idx_t, rank_t, starts, _tile(n, 2048))
    dest_tok = dest_t.T.reshape(n // tm, 1, tm * TOP_K)
    xs = _scatter_rows(xp, dest_tok, tm)
    ys = _experts(xs, vt, ve, seg, n_visits, wts['we_gu'], wts['we_d'], bm)
    y = _final(h1, ys, dest_tok, wtok, p.reshape(n, -1), wts, tm)
    return y.reshape(b, t, d), ckv.reshape(b, t, KV_LORA), krope.reshape(b, t, QK_ROPE), gdn_new, conv_new


def kernel(x_prompt, x_sample, p_prompt, p_sample, cache_ckv, cache_krope, state_gdn, state_conv, page_table,
           g_mix, w_in, gdn_conv_w, gdn_a_log, gdn_dt_bias, gdn_norm, mla_g_q, mla_w_uq, mla_g_kv, mla_w_uk,
           mla_w_uv, w_out, g_ffn, w_router, router_bias, w_exp_gate, w_exp_up, w_exp_down, w_sh_gate,
           w_sh_up, w_sh_down, g_ple, w_ple_gate, w_ple_proj, g_final):
    depth = w_in.shape[0]
    assert depth == 1, "the final norm is fused into the (single) layer"
    bp, tp, _ = x_prompt.shape
    bs, ts, _ = x_sample.shape
    past = page_table.shape[1] * cache_ckv.shape[2]
    wts = _prep_weights(g_mix[0], w_in[0], gdn_conv_w[0], gdn_a_log[0], gdn_dt_bias[0], gdn_norm[0], mla_g_q[0],
                        mla_w_uq[0], mla_g_kv[0], mla_w_uk[0], mla_w_uv[0], w_out[0], g_ffn[0], w_router[0],
                        router_bias[0], w_exp_gate[0], w_exp_up[0], w_exp_down[0], w_sh_gate[0], w_sh_up[0],
                        w_sh_down[0], g_ple[0], w_ple_gate[0], w_ple_proj[0], g_final)

    cache_krope_t = jnp.swapaxes(cache_krope[0], 1, 2)

    def attend_s(q, kv):
        return _paged_attn(q, kv, cache_ckv[0], cache_krope_t, page_table,
                           pages_per_step=_tile(page_table.shape[1], 16))

    ys, c2, k2, s2, v2 = _layer(x_sample, p_sample[0], past + jnp.arange(ts), state_gdn[0], state_conv[0],
                                attend_s, wts, chunk=ts)

    gdn0 = jnp.zeros((bp, GDN_HEADS, GDN_DK, GDN_DV), state_gdn.dtype)
    conv0 = jnp.zeros((bp, CONV_W - 1, CONV_CH), state_conv.dtype)
    attend_p = functools.partial(_prompt_attn, tq=_tile(tp, 512), tk=_tile(tp, 512))
    yp, c1, k1, s1, v1 = _layer(x_prompt, p_prompt[0], jnp.arange(tp), gdn0, conv0, attend_p, wts,
                                chunk=_tile(tp, 64))
    st = lambda a, ref: a.astype(ref.dtype)[None]
    return (yp, ys, c1[None], k1[None], st(s1, state_gdn), st(v1, state_conv),
            c2[None], k2[None], st(s2, state_gdn), st(v2, state_conv))
```

```python
import functools

import jax
import jax.numpy as jnp
import numpy as np
from jax import lax
from jax.experimental import pallas as pl
from jax.experimental.pallas import tpu as pltpu

EPS = 1e-6
ROPE_THETA = 10000.0
GDN_HEADS = 4
GDN_DK = 128
GDN_DV = 128
CONV_W = 4
MLA_HEADS = 4
Q_LORA = 256
KV_LORA = 256
QK_NOPE = 128
QK_ROPE = 64
V_DIM = 128
N_EXPERTS = 64
TOP_K = 8
N_GROUPS = 8
TOPK_GROUPS = 4
GROUP_SIZE = N_EXPERTS // N_GROUPS
D_EXPERT = 256
D_SHARED = 256
ROUTED_SCALE = 2.5
MLA_SCALE = (QK_NOPE + QK_ROPE) ** -0.5

GDN_QK_W = GDN_HEADS * GDN_DK
GDN_V_W = GDN_HEADS * GDN_DV
CONV_CH = 2 * GDN_QK_W + GDN_V_W
LANE = 128
SUBLANE = 8
QK_PAD = KV_LORA + LANE
VMEM_LIMIT = 56 * 1024 * 1024
D_MODEL = 1024
ROW_SLAB = D_MODEL // 2 // LANE

C_QKV = 0
C_Z = C_QKV + CONV_CH
C_CQ = C_Z + GDN_V_W
C_CKV = C_CQ + Q_LORA
C_KR = C_CKV + KV_LORA
C_KRR = C_KR + QK_ROPE
C_BA = C_KRR + QK_ROPE
W1_WIDTH = C_BA + LANE

F32 = jnp.float32
BF16 = jnp.bfloat16
NT = (((1,), (1,)), ((), ()))
TN = (((0,), (0,)), ((), ()))


def _cparams(sem, **flags):
    return pltpu.CompilerParams(dimension_semantics=sem, vmem_limit_bytes=VMEM_LIMIT, flags=flags or None)


def _rms(x, g):
    return x * lax.rsqrt(jnp.mean(x * x, axis=-1, keepdims=True) + EPS) * g


def _sigmoid(x):
    return 1.0 / (1.0 + jnp.exp(-x))


def _silu(x):
    return x * _sigmoid(x)


def _full(shape):
    return pl.BlockSpec(shape, lambda *_: (0,) * len(shape))


def _in_proj_kernel(x_ref, gmix_ref, w1_ref, alog_ref, dtb_ref, gq_ref, wuq_ref, wuk_ref, gkv_ref,
                    cos_ref, sin_ref,
                    qkv_ref, z_ref, gb_ref, ckv_ref, krope_ref, kv_ref, q_ref):
    u = _rms(x_ref[...], gmix_ref[...]).astype(BF16)
    qkv_ref[...] = jnp.dot(u, w1_ref[:, C_QKV:C_Z], preferred_element_type=F32)
    z_ref[...] = jnp.dot(u, w1_ref[:, C_Z:C_CQ], preferred_element_type=F32)
    rest = jnp.dot(u, w1_ref[:, C_CQ:W1_WIDTH], preferred_element_type=F32)
    c_q = rest[:, 0:Q_LORA]
    c_kv = rest[:, C_CKV - C_CQ:C_KR - C_CQ]
    kr2 = rest[:, C_KR - C_CQ:C_BA - C_CQ]
    ba = rest[:, C_BA - C_CQ:]

    lane = lax.broadcasted_iota(jnp.int32, ba.shape, 1)
    beta = _sigmoid(ba)
    sp_in = ba + dtb_ref[...]
    softplus = jnp.maximum(sp_in, 0.0) + jnp.log(1.0 + jnp.exp(-jnp.abs(sp_in)))
    g = -jnp.exp(alog_ref[...]) * softplus
    gb_ref[...] = jnp.where(lane < GDN_HEADS, beta, g)

    cos = cos_ref[...]
    sin = sin_ref[...]
    ckv = _rms(c_kv, gkv_ref[...])
    ckv_ref[...] = ckv
    krope = kr2[:, :QK_ROPE] * cos[:, :QK_ROPE] + kr2[:, QK_ROPE:] * sin[:, :QK_ROPE]
    krope_ref[...] = krope
    zpad = jnp.zeros((ckv.shape[0], LANE - QK_ROPE), F32)
    kv_ref[...] = jnp.concatenate([ckv, krope, zpad], axis=1).astype(BF16)

    cqn = _rms(c_q, gq_ref[...]).astype(BF16)
    qh = jnp.dot(cqn, wuq_ref[...], preferred_element_type=F32)
    nope_w = MLA_HEADS * QK_NOPE
    rope_w = MLA_HEADS * QK_ROPE
    q_rope = qh[:, nope_w:nope_w + rope_w] * cos + qh[:, nope_w + rope_w:] * sin
    parts = []
    for h in range(MLA_HEADS):
        q_nope = qh[:, h * QK_NOPE:(h + 1) * QK_NOPE].astype(BF16)
        q_lat = jnp.dot(q_nope, wuk_ref[h], preferred_element_type=F32)
        parts += [q_lat, q_rope[:, h * QK_ROPE:(h + 1) * QK_ROPE], zpad]
    q_ref[...] = (jnp.concatenate(parts, axis=1) * MLA_SCALE).astype(BF16)


def _in_proj(x, wts, cos, sin, tm):
    n, d = x.shape
    period = cos.shape[0]
    nper = period // tm
    tok = lambda w: pl.BlockSpec((tm, w), lambda i: (i, 0))
    pos = pl.BlockSpec((tm, cos.shape[1]), lambda i: (i % nper, 0))
    out_shapes = (
        jax.ShapeDtypeStruct((n, CONV_CH), F32),
        jax.ShapeDtypeStruct((n, GDN_V_W), F32),
        jax.ShapeDtypeStruct((n, LANE), F32),
        jax.ShapeDtypeStruct((n, KV_LORA), F32),
        jax.ShapeDtypeStruct((n, QK_ROPE), F32),
        jax.ShapeDtypeStruct((n, QK_PAD), BF16),
        jax.ShapeDtypeStruct((n, MLA_HEADS * QK_PAD), BF16),
    )
    return pl.pallas_call(
        _in_proj_kernel,
        grid=(n // tm,),
        in_specs=[tok(d), _full((1, d)), _full(wts['w1'].shape), _full((1, LANE)), _full((1, LANE)),
                  _full((1, Q_LORA)), _full(wts['wuq'].shape), _full(wts['wuk'].shape), _full((1, KV_LORA)),
                  pos, pos],
        out_specs=[tok(CONV_CH), tok(GDN_V_W), tok(LANE), tok(KV_LORA), tok(QK_ROPE), tok(QK_PAD),
                   tok(MLA_HEADS * QK_PAD)],
        out_shape=out_shapes,
        compiler_params=_cparams(("arbitrary",)),
        name="in_proj",
    )(x, wts['g_mix'], wts['w1'], wts['alog'], wts['dtb'], wts['g_q'], wts['wuq'], wts['wuk'], wts['g_kv'],
      cos, sin)


def _gdn_kernel(qkv_ref, z_ref, gb_ref, s0_ref, conv0_ref, *rest, chunk):
    consts = rest[:-4]
    o_ref, sout_ref, s_scr, xbuf = rest[-4:]
    c = pl.program_id(1)

    @pl.when(c == 0)
    def _():
        s_scr[...] = s0_ref[...]
        xbuf[:, 0:SUBLANE, :] = conv0_ref[...]

    cw_ref, gn_ref = consts[0], consts[1]
    nb, rows = qkv_ref.shape[0], qkv_ref.shape[1]
    chains = {}
    for bi in range(nb):
        y = _gdn_conv(qkv_ref.at[bi], cw_ref, xbuf.at[bi])
        for ci in range(rows // chunk):
            tok = slice(ci * chunk, (ci + 1) * chunk)
            chains[bi, ci] = _gdn_chunk_prep(y[tok], gb_ref[bi, tok, :], *consts[2:], chunk=chunk)
    prepped = dict(zip(chains, _round_robin(list(chains.values()))))
    for ci in range(rows // chunk):
        for bi in range(nb):
            tok = slice(ci * chunk, (ci + 1) * chunk)
            out = _gdn_chunk_state(prepped[bi, ci], z_ref[bi, tok, :], gn_ref, consts[-1], s_scr.at[bi], chunk=chunk)
            for h in range(GDN_HEADS):
                o_ref[bi, tok, h * GDN_DV:(h + 1) * GDN_DV] = out[h * chunk:(h + 1) * chunk, :].astype(o_ref.dtype)

    @pl.when(c == pl.num_programs(1) - 1)
    def _():
        sout_ref[...] = s_scr[...]


def _gdn_conv(qkv_ref, cw_ref, xbuf):
    pre = SUBLANE
    tail = CONV_W - 1
    rows = qkv_ref.shape[0]
    xbuf[pre:pre + rows, :] = qkv_ref[...]
    y = xbuf[pre - tail:pre - tail + rows, :] * cw_ref[0:1, :]
    for j in range(1, CONV_W):
        y = y + xbuf[pre - tail + j:pre - tail + j + rows, :] * cw_ref[j:j + 1, :]
    xbuf[pre - tail:pre, :] = xbuf[pre + rows - tail:pre + rows, :]
    return _silu(y)


def _stack_gdn_heads(a, off):
    return jnp.concatenate([a[:, off + h * GDN_DK:off + (h + 1) * GDN_DK] for h in range(GDN_HEADS)], axis=0)


def _round_robin(gens):
    results = [None] * len(gens)
    live = list(range(len(gens)))
    while live:
        for i in list(live):
            try:
                next(gens[i])
            except StopIteration as done:
                results[i] = done.value
                live.remove(i)
    return results


def _gdn_chunk_prep(y, gb, cmat_ref, gsel_ref, ssel_ref, bsel_ref, place_ref, triu_ref, mask_ref, lmask_ref, hsel_ref,
                    *, chunk):
    stack = _stack_gdn_heads
    q = stack(y, 0)
    k = stack(y, GDN_QK_W)
    v = stack(y, 2 * GDN_QK_W)
    qn = q * lax.rsqrt(jnp.sum(q * q, axis=-1, keepdims=True) + EPS) * (GDN_DK ** -0.5)
    kn = k * lax.rsqrt(jnp.sum(k * k, axis=-1, keepdims=True) + EPS)

    r = GDN_HEADS * chunk
    lane_of = lambda a, sel: jnp.sum(a * sel, axis=-1, keepdims=True)
    gb_hi = gb.astype(BF16)
    gb_r1 = gb - gb_hi.astype(F32)
    gb_mid = gb_r1.astype(BF16)
    gb_lo = (gb_r1 - gb_mid.astype(F32)).astype(BF16)
    pieces = (gb_hi, gb_mid, gb_lo)
    cums = sum(jnp.dot(cmat_ref[...], p, preferred_element_type=F32) for p in pieces)
    total = jnp.sum(gb, axis=0, keepdims=True)
    gsel = gsel_ref[...]
    gcol = lane_of(cums, gsel)
    glast = lane_of(total, gsel)
    gstate = lane_of(total, ssel_ref[...])
    beta = lane_of(jnp.concatenate([gb] * GDN_HEADS, axis=0), bsel_ref[...])
    spread = sum(jnp.dot(p, place_ref[...], preferred_element_type=F32) for p in pieces)
    grow = jnp.sum(spread * triu_ref[...], axis=0, keepdims=True)

    incl = mask_ref[0]
    decay = jnp.exp(jnp.where(incl > 0.5, gcol - grow, -jnp.inf))
    kb = kn * beta
    knb = kn.astype(BF16)
    kk = lax.dot_general(kb.astype(BF16), knb, NT, preferred_element_type=F32)
    yield
    lower = kk * decay * mask_ref[1]
    inv = mask_ref[2] - lower * mask_ref[3]
    lower_b = lower.astype(BF16)
    for lvl in range(lmask_ref.shape[0]):
        ih = inv.astype(BF16)
        blk = lower_b * lmask_ref[lvl]
        t = jnp.dot(ih, blk, preferred_element_type=F32)
        yield
        c = jnp.dot(t.astype(BF16), ih, preferred_element_type=F32)
        yield
        inv = inv - c
    egc = jnp.exp(gcol)
    rhs = jnp.concatenate([v * beta, kb * egc], axis=1)
    uw = jnp.dot(inv.astype(BF16), rhs.astype(BF16), preferred_element_type=F32)
    yield
    u = uw[:, :GDN_DV]
    w = uw[:, GDN_DV:]
    qk = lax.dot_general(qn.astype(BF16), knb, NT, preferred_element_type=F32) * decay

    hsel = hsel_ref[...].astype(BF16)
    wide = lambda a: jnp.concatenate([a.astype(BF16)] * GDN_HEADS, axis=1) * hsel
    return dict(u=u, w_qe=jnp.concatenate([wide(w), wide(qn * egc)], axis=0), qk=qk.astype(BF16),
                kdec=wide(kn * jnp.exp(glast - gcol)), sdecay=jnp.exp(gstate))


def _gdn_chunk_state(p, z, gn_ref, hsel_ref, s_scr, *, chunk):
    r = GDN_HEADS * chunk
    st = s_scr[...]
    ws_qs = jnp.dot(p['w_qe'], st.astype(BF16), preferred_element_type=F32)
    v_new = (p['u'] - ws_qs[:r]).astype(BF16)
    o = ws_qs[r:] + jnp.dot(p['qk'], v_new, preferred_element_type=F32)
    s_scr[...] = st * p['sdecay'] + lax.dot_general(p['kdec'], v_new, TN, preferred_element_type=F32)
    return _rms(o, gn_ref[...]) * _silu(_stack_gdn_heads(z, 0))


def _gdn_constants(chunk):
    r = GDN_HEADS * chunk
    ri = np.arange(r)
    head, tok = ri // chunk, ri % chunk
    ti = np.arange(chunk)
    lane = np.arange(LANE)
    srow = np.arange(GDN_HEADS * GDN_DK)
    cmat = ti[None, :] <= tok[:, None]
    gsel = lane[None, :] == GDN_HEADS + head[:, None]
    ssel = lane[None, :] == GDN_HEADS + (srow // GDN_DK)[:, None]
    bsel = lane[None, :] == head[:, None]
    place = lane[:, None] == GDN_HEADS + head[None, :]
    triu = ti[:, None] <= tok[None, :]
    same = head[:, None] == head[None, :]
    masks = [same & (tok[:, None] >= tok[None, :]), same & (tok[:, None] > tok[None, :]),
             ri[:, None] == ri[None, :]]
    s = 1
    while s < chunk:
        masks.append(same & ((tok[:, None] // s) % 2 == 1) & ((tok[None, :] // s) == (tok[:, None] // s) - 1))
        s *= 2
    hsel = (np.arange(GDN_HEADS * GDN_DK)[None, :] // GDN_DK) == head[:, None]
    f = lambda a: jnp.asarray(np.asarray(a, np.float32))
    fb = lambda a: f(a).astype(BF16)
    return (fb(cmat), f(gsel), f(ssel), f(bsel), fb(place), f(triu), f(np.stack(masks[:4])), fb(np.stack(masks[4:])),
            f(hsel))


def _gdn(qkv, z, gb, s0, conv0, cw, gn, chunk):
    b, t, _ = qkv.shape
    assert t % chunk == 0 and chunk % SUBLANE == 0
    nb = 2 if b % 2 == 0 else 1
    nch = 2 if (t // chunk) % 2 == 0 else 1
    rows = nch * chunk
    consts = _gdn_constants(chunk)
    tokb = lambda w: pl.BlockSpec((nb, rows, w), lambda i, c: (i, c, 0))
    state = pl.BlockSpec((nb, GDN_HEADS * GDN_DK, GDN_DV), lambda i, c: (i, 0, 0))
    o, s_new = pl.pallas_call(
        functools.partial(_gdn_kernel, chunk=chunk),
        grid=(b // nb, t // rows),
        in_specs=[tokb(CONV_CH), tokb(GDN_V_W), tokb(LANE), state,
                  pl.BlockSpec((nb, SUBLANE, CONV_CH), lambda i, c: (i, 0, 0)),
                  _full((SUBLANE, CONV_CH)), _full((1, GDN_DV))] + [_full(a.shape) for a in consts],
        out_specs=[tokb(GDN_V_W), state],
        out_shape=(jax.ShapeDtypeStruct((b, t, GDN_V_W), BF16),
                   jax.ShapeDtypeStruct((b, GDN_HEADS * GDN_DK, GDN_DV), F32)),
        scratch_shapes=[pltpu.VMEM((nb, GDN_HEADS * GDN_DK, GDN_DV), F32),
                        pltpu.VMEM((nb, SUBLANE + rows, CONV_CH), F32)],
        compiler_params=_cparams(("arbitrary", "arbitrary")),
        name="gdn",
    )(qkv, z, gb, s0.reshape(b, GDN_HEADS * GDN_DK, GDN_DV), conv0, cw, gn, *consts)
    return o, s_new.reshape(b, GDN_HEADS, GDN_DK, GDN_DV)


def _stack_heads(q):
    return jnp.concatenate([q[:, h * QK_PAD:(h + 1) * QK_PAD] for h in range(MLA_HEADS)], axis=0)


def _row_reduce(x, combine, reduce):
    w = x.shape[1]
    if w > LANE and w % LANE == 0:
        tiles = [x[:, i * LANE:(i + 1) * LANE] for i in range(w // LANE)]
        while len(tiles) > 1:
            tiles = [combine(tiles[i], tiles[i + 1]) if i + 1 < len(tiles) else tiles[i]
                     for i in range(0, len(tiles), 2)]
        x = tiles[0]
    return reduce(x, axis=-1, keepdims=True)


def _online_softmax_step(s, vals, m_scr, l_scr, acc_scr):
    m_prev = m_scr[...]
    m_new = jnp.maximum(m_prev, jnp.max(s, axis=-1, keepdims=True))
    alpha = jnp.exp(m_prev - m_new)
    p = jnp.exp(s - m_new)
    l_scr[...] = alpha * l_scr[...] + jnp.sum(p, axis=-1, keepdims=True)
    pb = p.astype(BF16)
    if isinstance(vals, (list, tuple)):
        w = s.shape[1] // len(vals)
        pv = sum(jnp.dot(pb[:, i * w:(i + 1) * w], vi, preferred_element_type=F32) for i, vi in enumerate(vals))
    else:
        pv = jnp.dot(pb, vals, preferred_element_type=F32)
    acc_scr[...] = alpha * acc_scr[...] + pv
    m_scr[...] = m_new


def _unstack_store(o_ref, acc_scr, l_scr, rows):
    o = acc_scr[...] / l_scr[...]
    for h in range(MLA_HEADS):
        o_ref[:, h * KV_LORA:(h + 1) * KV_LORA] = o[h * rows:(h + 1) * rows, :].astype(o_ref.dtype)


def _prompt_attn_kernel(q_ref, kv_ref, o_ref, qs_scr, m_scr, l_scr, acc_scr, *, tq, tk):
    i = pl.program_id(1)
    j = pl.program_id(2)
    nj = pl.num_programs(2)

    @pl.when(j == 0)
    def _():
        qs_scr[...] = _stack_heads(q_ref[...])
        m_scr[...] = jnp.full(m_scr.shape, -jnp.inf, F32)
        l_scr[...] = jnp.zeros(l_scr.shape, F32)
        acc_scr[...] = jnp.zeros(acc_scr.shape, F32)

    needed = j * tk <= i * tq + tq - 1
    unmasked = j * tk + tk - 1 <= i * tq
    nchain = 2
    rows = MLA_HEADS * tq // nchain

    def chain(c, masked):
        rs = pl.ds(c * rows, rows)
        kv = kv_ref[...]
        s = lax.dot_general(qs_scr[rs, :], kv, NT, preferred_element_type=F32)
        yield
        if masked:
            qpos = i * tq + lax.broadcasted_iota(jnp.int32, s.shape, 0) % tq
            kpos = j * tk + lax.broadcasted_iota(jnp.int32, s.shape, 1)
            s = jnp.where(kpos <= qpos, s, -jnp.inf)
        m_prev = m_scr[rs, :]
        m_new = jnp.maximum(m_prev, _row_reduce(s, jnp.maximum, jnp.max))
        alpha = jnp.exp(m_prev - m_new)
        p = jnp.exp(s - m_new)
        l_scr[rs, :] = alpha * l_scr[rs, :] + _row_reduce(p, jnp.add, jnp.sum)
        m_scr[rs, :] = m_new
        pv = jnp.dot(p.astype(BF16), kv[:, :KV_LORA], preferred_element_type=F32)
        yield
        acc_scr[rs, :] = alpha * acc_scr[rs, :] + pv

    @pl.when(unmasked)
    def _():
        _round_robin([chain(c, False) for c in range(nchain)])

    @pl.when(needed & jnp.logical_not(unmasked))
    def _():
        _round_robin([chain(c, True) for c in range(nchain)])

    @pl.when(j == nj - 1)
    def _():
        _unstack_store(o_ref, acc_scr, l_scr, tq)


def _prompt_attn(q, kv, tq, tk):
    b, s, _ = q.shape
    rows = MLA_HEADS * tq
    last = lambda i: (i * tq + tq - 1) // tk
    return pl.pallas_call(
        functools.partial(_prompt_attn_kernel, tq=tq, tk=tk),
        grid=(b, s // tq, s // tk),
        in_specs=[pl.BlockSpec((None, tq, MLA_HEADS * QK_PAD), lambda bb, i, j: (bb, i, 0)),
                  pl.BlockSpec((None, tk, QK_PAD), lambda bb, i, j: (bb, jnp.minimum(j, last(i)), 0))],
        out_specs=pl.BlockSpec((None, tq, MLA_HEADS * KV_LORA), lambda bb, i, j: (bb, i, 0)),
        out_shape=jax.ShapeDtypeStruct((b, s, MLA_HEADS * KV_LORA), BF16),
        scratch_shapes=[pltpu.VMEM((rows, QK_PAD), BF16), pltpu.VMEM((rows, 1), F32),
                        pltpu.VMEM((rows, 1), F32), pltpu.VMEM((rows, KV_LORA), F32)],
        compiler_params=_cparams(("arbitrary", "arbitrary", "arbitrary")),
        name="prompt_attn",
    )(q, kv)


def _paged_pages_chain(qs_scr, ckv_refs, kr_refs, m_scr, l_scr, acc_scr):
    qs = qs_scr[...]
    q_lat = qs[:, :KV_LORA]
    q_rope = qs[:, KV_LORA:KV_LORA + QK_ROPE]
    pages, scores = [], []
    for ckv_ref, kr_ref in zip(ckv_refs, kr_refs):
        ckv = ckv_ref[...].astype(BF16)
        kr_t = kr_ref[...].astype(BF16)
        pages.append(ckv)
        scores.append(lax.dot_general(q_lat, ckv, NT, preferred_element_type=F32)
                      + jnp.dot(q_rope, kr_t, preferred_element_type=F32))
    yield
    s = jnp.concatenate(scores, axis=1)
    m_prev = m_scr[...]
    m_new = jnp.maximum(m_prev, _row_reduce(s, jnp.maximum, jnp.max))
    alpha = jnp.exp(m_prev - m_new)
    p = jnp.exp(s - m_new)
    l_scr[...] = alpha * l_scr[...] + _row_reduce(p, jnp.add, jnp.sum)
    m_scr[...] = m_new
    pb = p.astype(BF16)
    w = s.shape[1] // len(pages)
    pv = sum(jnp.dot(pb[:, i * w:(i + 1) * w], vi, preferred_element_type=F32) for i, vi in enumerate(pages))
    yield
    acc_scr[...] = alpha * acc_scr[...] + pv


def _paged_attn_kernel(pt_ref, q_ref, kvnew_ref, ckv_hbm, kr_hbm, o_ref,
                       qs_scr, m_scr, l_scr, acc_scr, ckv_buf, kr_buf, sem, *, pages_per_step, n_pages, t):
    nb = q_ref.shape[0]
    npg = nb * pages_per_step
    bb = pl.program_id(0)
    g = pl.program_id(1)
    ng = pl.num_programs(1)
    step = bb * ng + g
    slot = step % 2

    def fetch(bb_, g_, into):
        for bi in range(nb):
            for p in range(pages_per_step):
                pid = pt_ref[(bb_ * nb + bi) * n_pages + g_ * pages_per_step + p]
                j = bi * pages_per_step + p
                pltpu.make_async_copy(ckv_hbm.at[pid], ckv_buf.at[into, j], sem.at[0, into]).start(priority=j % 2)
                pltpu.make_async_copy(kr_hbm.at[pid], kr_buf.at[into, j], sem.at[1, into]).start(priority=j % 2)

    @pl.when(step == 0)
    def _():
        fetch(bb, g, 0)

    @pl.when(step + 1 < pl.num_programs(0) * ng)
    def _():
        wrap = g + 1 == ng
        fetch(jnp.where(wrap, bb + 1, bb), jnp.where(wrap, 0, g + 1), 1 - slot)

    @pl.when(g == 0)
    def _():
        for bi in range(nb):
            qs_scr[bi] = _stack_heads(q_ref[bi])
        m_scr[...] = jnp.full(m_scr.shape, -jnp.inf, F32)
        l_scr[...] = jnp.zeros(l_scr.shape, F32)
        acc_scr[...] = jnp.zeros(acc_scr.shape, F32)

    pltpu.make_async_copy(ckv_hbm.at[pl.ds(0, npg)], ckv_buf.at[slot], sem.at[0, slot]).wait()
    pltpu.make_async_copy(kr_hbm.at[pl.ds(0, npg)], kr_buf.at[slot], sem.at[1, slot]).wait()

    _round_robin([
        _paged_pages_chain(qs_scr.at[bi],
                           [ckv_buf.at[slot, bi * pages_per_step + p] for p in range(pages_per_step)],
                           [kr_buf.at[slot, bi * pages_per_step + p] for p in range(pages_per_step)],
                           m_scr.at[bi], l_scr.at[bi], acc_scr.at[bi])
        for bi in range(nb)])

    @pl.when(g == ng - 1)
    def _():
        for bi in range(nb):
            kvn = kvnew_ref[bi]
            s = lax.dot_general(qs_scr[bi], kvn, NT, preferred_element_type=F32)
            qpos = lax.broadcasted_iota(jnp.int32, s.shape, 0) % t
            kpos = lax.broadcasted_iota(jnp.int32, s.shape, 1)
            s = jnp.where(kpos <= qpos, s, -jnp.inf)
            _online_softmax_step(s, kvn[:, :KV_LORA], m_scr.at[bi], l_scr.at[bi], acc_scr.at[bi])
            _unstack_store(o_ref.at[bi], acc_scr.at[bi], l_scr.at[bi], t)


def _paged_attn(q, kvnew, cache_ckv, cache_krope_t, page_table, pages_per_step):
    b, t, _ = q.shape
    n_pages = page_table.shape[1]
    page = cache_ckv.shape[1]
    assert n_pages % pages_per_step == 0
    rows = MLA_HEADS * t
    nb = next(c for c in (4, 2, 1) if b % c == 0)

    npg = nb * pages_per_step
    assert cache_ckv.shape[0] >= npg
    grid_spec = pltpu.PrefetchScalarGridSpec(
        num_scalar_prefetch=1,
        grid=(b // nb, n_pages // pages_per_step),
        in_specs=[pl.BlockSpec((nb, t, MLA_HEADS * QK_PAD), lambda bb, g, pt: (bb, 0, 0)),
                  pl.BlockSpec((nb, t, QK_PAD), lambda bb, g, pt: (bb, 0, 0)),
                  pl.BlockSpec(memory_space=pl.ANY), pl.BlockSpec(memory_space=pl.ANY)],
        out_specs=pl.BlockSpec((nb, t, MLA_HEADS * KV_LORA), lambda bb, g, pt: (bb, 0, 0)),
        scratch_shapes=[pltpu.VMEM((nb, rows, QK_PAD), BF16), pltpu.VMEM((nb, rows, 1), F32),
                        pltpu.VMEM((nb, rows, 1), F32), pltpu.VMEM((nb, rows, KV_LORA), F32),
                        pltpu.VMEM((2, npg, page, KV_LORA), cache_ckv.dtype),
                        pltpu.VMEM((2, npg, QK_ROPE, page), cache_krope_t.dtype),
                        pltpu.SemaphoreType.DMA((2, 2))],
    )
    return pl.pallas_call(
        functools.partial(_paged_attn_kernel, pages_per_step=pages_per_step, n_pages=n_pages, t=t),
        grid_spec=grid_spec,
        out_shape=jax.ShapeDtypeStruct((b, t, MLA_HEADS * KV_LORA), BF16),
        compiler_params=_cparams(("arbitrary", "arbitrary")),
        name="paged_attn",
    )(page_table.reshape(-1), q, kvnew, cache_ckv, cache_krope_t)


def _pack_bf16_pairs(x):
    w = x.shape[1] // 2
    xb = x.astype(BF16).astype(F32)
    lo = lax.shift_right_logical(lax.bitcast_convert_type(xb[:, :w], jnp.uint32), jnp.uint32(16))
    hi = lax.bitcast_convert_type(xb[:, w:], jnp.uint32) & jnp.uint32(0xFFFF0000)
    return lo | hi


def _store_row_slabs(ref, first, x):
    t, w = x.shape
    per = w // LANE
    for j in range(per):
        ref[pl.ds(first * per + j, t, stride=per), :] = x[:, j * LANE:(j + 1) * LANE]


def _load_row_slabs(ref, first, t, w):
    per = w // LANE
    return jnp.concatenate([ref[pl.ds(first * per + j, t, stride=per), :] for j in range(per)], axis=1)


def _unpack_bf16_pairs(p):
    lo = lax.bitcast_convert_type(lax.shift_left(p, jnp.uint32(16)), F32)
    hi = lax.bitcast_convert_type(p & jnp.uint32(0xFFFF0000), F32)
    return jnp.concatenate([lo, hi], axis=1)


def _post_mix_kernel(h_ref, ogdn_ref, olat_ref, wuv_ref, wout_ref, gffn_ref, wrh_ref, wrl_ref, rbias_ref,
                     wsgu_ref, wsd_ref,
                     h1_ref, xp_ref, idx_ref, rank_ref, wtok_ref, cnt_ref, carry):
    i = pl.program_id(0)
    tm = h_ref.shape[0]

    @pl.when(i == 0)
    def _():
        carry[...] = jnp.zeros(carry.shape, F32)

    parts = [ogdn_ref[...]]
    for hh in range(MLA_HEADS):
        o_h = jnp.dot(olat_ref[:, hh * KV_LORA:(hh + 1) * KV_LORA], wuv_ref[hh], preferred_element_type=F32)
        parts.append(o_h.astype(BF16))
    mix = jnp.concatenate(parts, axis=1)
    h1 = h_ref[...] + jnp.dot(mix, wout_ref[...], preferred_element_type=F32)
    u2 = _rms(h1, gffn_ref[...])
    u2b = u2.astype(BF16)
    _store_row_slabs(xp_ref, 0, _pack_bf16_pairs(u2))

    u2l = (u2 - u2b.astype(F32)).astype(BF16)
    logits = (lax.dot_general(wrh_ref[...], u2b, NT, preferred_element_type=F32)
              + lax.dot_general(wrh_ref[...], u2l, NT, preferred_element_type=F32)
              + lax.dot_general(wrl_ref[...], u2b, NT, preferred_element_type=F32))
    scores = _sigmoid(logits)
    biased = scores + rbias_ref[...]
    neg = -jnp.inf

    sub = lax.broadcasted_iota(jnp.int32, (GROUP_SIZE, tm), 0)
    gscore = []
    for gi in range(N_GROUPS):
        vg = biased[gi * GROUP_SIZE:(gi + 1) * GROUP_SIZE, :]
        m1 = jnp.max(vg, axis=0, keepdims=True)
        first = jnp.min(jnp.where(vg == m1, sub, GROUP_SIZE), axis=0, keepdims=True)
        m2 = jnp.max(jnp.where(sub == first, neg, vg), axis=0, keepdims=True)
        gscore.append(m1 + m2)
    gs = jnp.concatenate(gscore, axis=0)
    gid = lax.broadcasted_iota(jnp.int32, gs.shape, 0)
    gsel = jnp.zeros(gs.shape, F32)
    for _ in range(TOPK_GROUPS):
        m = jnp.max(gs, axis=0, keepdims=True)
        pick = jnp.min(jnp.where(gs == m, gid, N_GROUPS), axis=0, keepdims=True)
        hit = gid == pick
        gsel = jnp.where(hit, 1.0, gsel)
        gs = jnp.where(hit, neg, gs)
    emask = jnp.concatenate(
        [jnp.broadcast_to(gsel[gi:gi + 1, :], (GROUP_SIZE, tm)) for gi in range(N_GROUPS)], axis=0)
    cand = jnp.where(emask > 0.5, biased, neg)

    eid = lax.broadcasted_iota(jnp.int32, cand.shape, 0)
    chosen = jnp.zeros(cand.shape, F32)
    idx_rows, w_rows, hits = [], [], []
    for _ in range(TOP_K):
        m = jnp.max(cand, axis=0, keepdims=True)
        pick = jnp.min(jnp.where(cand == m, eid, N_EXPERTS), axis=0, keepdims=True)
        hit = eid == pick
        idx_rows.append(pick)
        w_rows.append(jnp.sum(jnp.where(hit, scores, 0.0), axis=0, keepdims=True))
        hits.append(hit)
        chosen = chosen + hit.astype(F32)
        cand = jnp.where(hit, neg, cand)
    wsel = jnp.concatenate(w_rows, axis=0)
    wsel = wsel / (jnp.sum(wsel, axis=0, keepdims=True) + 1e-20) * ROUTED_SCALE
    idx_ref[...] = jnp.concatenate(idx_rows, axis=0)

    ti = lax.broadcasted_iota(jnp.int32, (tm, tm), 0)
    tj = lax.broadcasted_iota(jnp.int32, (tm, tm), 1)
    before = (ti < tj).astype(BF16)
    prefix = jnp.dot(chosen.astype(BF16), before, preferred_element_type=F32) + carry[:, 0:1]
    rank_rows = [jnp.sum(jnp.where(hit, prefix, 0.0), axis=0, keepdims=True) for hit in hits]
    rank_ref[...] = jnp.concatenate(rank_rows, axis=0).astype(jnp.int32)
    carry[...] = carry[...] + jnp.sum(chosen, axis=1, keepdims=True)
    cnt_ref[...] = carry[...]

    wpad = jnp.concatenate([wsel, jnp.zeros((LANE - TOP_K, tm), F32)], axis=0)
    wtok_ref[...] = wpad.T

    gu = jnp.dot(u2b, wsgu_ref[...], preferred_element_type=F32)
    hs = (_silu(gu[:, :D_SHARED]) * gu[:, D_SHARED:]).astype(BF16)
    h1_ref[...] = h1 + jnp.dot(hs, wsd_ref[...], preferred_element_type=F32)


def _post_mix(h, o_gdn, o_lat, wts, tm):
    n, d = h.shape
    tok = lambda w: pl.BlockSpec((tm, w), lambda i: (i, 0))
    tr = lambda r: pl.BlockSpec((r, tm), lambda i: (0, i))
    return pl.pallas_call(
        _post_mix_kernel,
        grid=(n // tm,),
        in_specs=[tok(d), tok(GDN_V_W), tok(MLA_HEADS * KV_LORA), _full(wts['wuv'].shape), _full((d, d)),
                  _full((1, d)), _full((N_EXPERTS, d)), _full((N_EXPERTS, d)), _full((N_EXPERTS, 1)),
                  _full((d, 2 * D_SHARED)), _full((D_SHARED, d))],
        out_specs=[tok(d), pl.BlockSpec((tm * ROW_SLAB, LANE), lambda i: (i, 0)), tr(TOP_K), tr(TOP_K), tok(LANE),
                   _full((N_EXPERTS, LANE))],
        out_shape=(jax.ShapeDtypeStruct((n, d), F32),
                   jax.ShapeDtypeStruct((n * ROW_SLAB, LANE), jnp.uint32),
                   jax.ShapeDtypeStruct((TOP_K, n), jnp.int32),
                   jax.ShapeDtypeStruct((TOP_K, n), jnp.int32),
                   jax.ShapeDtypeStruct((n, LANE), F32),
                   jax.ShapeDtypeStruct((N_EXPERTS, LANE), F32)),
        scratch_shapes=[pltpu.VMEM((N_EXPERTS, LANE), F32)],
        compiler_params=_cparams(("arbitrary",)),
        name="post_mix",
    )(h, o_gdn, o_lat, wts['wuv'], wts['w_out'], wts['g_ffn'], wts['wr_hi'], wts['wr_lo'], wts['rbias'],
      wts['ws_gu'], wts['ws_d'])


def _dest_kernel(idx_ref, rank_ref, starts_ref, o_ref):
    eid = lax.broadcasted_iota(jnp.int32, (N_EXPERTS, idx_ref.shape[1]), 0)
    starts = starts_ref[...]
    rows = []
    for kk in range(TOP_K):
        hit = eid == idx_ref[kk:kk + 1, :]
        rows.append(jnp.sum(jnp.where(hit, starts, 0.0), axis=0, keepdims=True))
    o_ref[...] = jnp.concatenate(rows, axis=0).astype(jnp.int32) + rank_ref[...]


def _dest_rows(idx_t, rank_t, starts, tn):
    k, n = idx_t.shape
    assert n * k < 2 ** 24
    tr = pl.BlockSpec((k, tn), lambda i: (0, i))
    return pl.pallas_call(
        _dest_kernel,
        grid=(n // tn,),
        in_specs=[tr, tr, _full((N_EXPERTS, 1))],
        out_specs=tr,
        out_shape=jax.ShapeDtypeStruct((k, n), jnp.int32),
        compiler_params=_cparams(("arbitrary",)),
        name="dest_rows",
    )(idx_t, rank_t, starts.astype(F32).reshape(N_EXPERTS, 1))


def _slab(ref, row):
    return ref.at[pl.ds(pl.multiple_of(row * ROW_SLAB, ROW_SLAB), ROW_SLAB)]


def _scatter_kernel(dest_ref, x_ref, o_hbm, sem, *, fan):
    nrow = x_ref.shape[0] // ROW_SLAB
    grp = SUBLANE

    def start(g, carry):
        base = pl.multiple_of(g * grp * fan, grp * fan)
        for dr in range(grp):
            for kk in range(fan):
                pltpu.make_async_copy(_slab(x_ref, g * grp + dr), _slab(o_hbm, dest_ref[0, 0, base + dr * fan + kk]),
                                      sem).start(priority=kk % 2)
        return carry

    lax.fori_loop(0, nrow // grp, start, 0)
    whole = o_hbm.at[pl.ds(0, fan * nrow * ROW_SLAB)]
    pltpu.make_async_copy(whole, whole, sem).wait()


def _scatter_rows(x, dest_tok, ts):
    w = x.shape[1]
    n = x.shape[0] // ROW_SLAB
    fan = dest_tok.shape[2] // ts
    return pl.pallas_call(
        functools.partial(_scatter_kernel, fan=fan),
        grid=(n // ts,),
        in_specs=[pl.BlockSpec((1, 1, ts * fan), lambda i: (i, 0, 0), memory_space=pltpu.SMEM),
                  pl.BlockSpec((ts * ROW_SLAB, w), lambda i: (i, 0))],
        out_specs=pl.BlockSpec(memory_space=pl.ANY),
        out_shape=jax.ShapeDtypeStruct((n * fan * ROW_SLAB, w), x.dtype),
        scratch_shapes=[pltpu.SemaphoreType.DMA],
        compiler_params=pltpu.CompilerParams(dimension_semantics=("arbitrary",), has_side_effects=True),
        name="scatter_rows",
    )(dest_tok, x)


def _experts_kernel(vt_ref, ve_ref, seg_ref, nv_ref, x_ref, wgu_ref, wd_ref, o_ref, *, bm):
    v = pl.program_id(0)
    tile = vt_ref[v]
    e = ve_ref[v]
    prev_tile = vt_ref[jnp.maximum(v - 1, 0)]
    first = (v == 0) | (tile != prev_tile)

    @pl.when(v < nv_ref[0])
    def _():
        w = ROW_SLAB * LANE
        nchain = 2 if bm % (2 * SUBLANE) == 0 else 1
        half = bm // nchain

        def ffn(c):
            x = _unpack_bf16_pairs(_load_row_slabs(x_ref, c * half, half, w)).astype(BF16)
            gu = jnp.dot(x, wgu_ref[...], preferred_element_type=F32)
            yield
            hid = (_silu(gu[:, :D_EXPERT]) * gu[:, D_EXPERT:]).astype(BF16)
            y = jnp.dot(hid, wd_ref[...], preferred_element_type=F32)
            yield
            return _pack_bf16_pairs(y)

        ys = _round_robin([ffn(c) for c in range(nchain)])
        lo = seg_ref[e] - tile * bm
        hi = seg_ref[e + 1] - tile * bm

        def mine(c):
            rowid = c * half + lax.broadcasted_iota(jnp.int32, (half, w), 0)
            return (rowid >= lo) & (rowid < hi)

        @pl.when(first)
        def _():
            for c in range(nchain):
                _store_row_slabs(o_ref, c * half, jnp.where(mine(c), ys[c], jnp.uint32(0)))

        @pl.when(jnp.logical_not(first))
        def _():
            for c in range(nchain):
                _store_row_slabs(o_ref, c * half, jnp.where(mine(c), ys[c], _load_row_slabs(o_ref, c * half, half, w)))


def _experts(xs, visit_tile, visit_expert, seg, n_visits, wgu, wd, bm):
    rows, w = xs.shape
    nv_max = visit_tile.shape[0]
    d = wd.shape[2]
    grid_spec = pltpu.PrefetchScalarGridSpec(
        num_scalar_prefetch=4,
        grid=(nv_max,),
        in_specs=[pl.BlockSpec((bm * ROW_SLAB, w), lambda v, vt, ve, sg, nv: (vt[v], 0)),
                  pl.BlockSpec((None, d, 2 * D_EXPERT), lambda v, vt, ve, sg, nv: (ve[v], 0, 0)),
                  pl.BlockSpec((None, D_EXPERT, d), lambda v, vt, ve, sg, nv: (ve[v], 0, 0))],
        out_specs=pl.BlockSpec((bm * ROW_SLAB, w), lambda v, vt, ve, sg, nv: (vt[v], 0)),
    )
    return pl.pallas_call(
        functools.partial(_experts_kernel, bm=bm),
        grid_spec=grid_spec,
        out_shape=jax.ShapeDtypeStruct((rows, w), jnp.uint32),
        compiler_params=_cparams(("arbitrary",)),
        name="experts",
    )(visit_tile, visit_expert, seg, n_visits, xs, wgu, wd)


def _visit_list(counts, rows, bm):
    ntiles = rows // bm
    nv_max = ntiles + N_EXPERTS - 1
    ends = jnp.cumsum(counts)
    starts = ends - counts
    first_tile = starts // bm
    ntile_e = jnp.where(counts > 0, (ends - 1) // bm - first_tile + 1, 0)
    vend = jnp.cumsum(ntile_e)
    vstart = vend - ntile_e
    n_visits = vend[-1]
    v = jnp.arange(nv_max, dtype=jnp.int32)
    ve = jnp.minimum(jnp.sum(vend[None, :] <= v[:, None], axis=1), N_EXPERTS - 1).astype(jnp.int32)
    vt = (first_tile[ve] + v - vstart[ve]).astype(jnp.int32)
    last = jnp.maximum(n_visits - 1, 0)
    ve = jnp.where(v < n_visits, ve, ve[last])
    vt = jnp.where(v < n_visits, vt, vt[last])
    seg = jnp.concatenate([starts, ends[-1:]]).astype(jnp.int32)
    return vt, ve, seg, n_visits.reshape(1).astype(jnp.int32), starts


def _final_kernel(dcur_ref, dnext_ref, h1_ref, wtok_ref, p_ref, gple_ref, wgate_ref, wproj_ref, gfin_ref,
                  ys_hbm, o_ref, ybuf, h2_scr, sem):
    i = pl.program_id(0)
    nsteps = pl.num_programs(0)
    tm = h1_ref.shape[0]
    slot = i % 2
    grp = SUBLANE

    def fetch_rows(dest_ref, into, r0):
        base = pl.multiple_of(r0 * TOP_K, grp * TOP_K)
        for dr in range(grp):
            for kk in range(TOP_K):
                pltpu.make_async_copy(_slab(ys_hbm, dest_ref[0, 0, base + dr * TOP_K + kk]),
                                      _slab(ybuf.at[into], kk * tm + r0 + dr), sem.at[into]).start(priority=kk % 2)

    def combine_rows(r0):
        rows = pl.ds(r0, grp)
        wtok = wtok_ref[rows, :]
        h2 = h1_ref[rows, :]
        for kk in range(TOP_K):
            yk = _unpack_bf16_pairs(_load_row_slabs(ybuf.at[slot], kk * tm + r0, grp, ROW_SLAB * LANE))
            h2 = h2 + yk * wtok[:, kk:kk + 1]
        h2_scr[rows, :] = h2

    def row_loop(body):
        def step(g, carry):
            body(pl.multiple_of(g * grp, grp))
            return carry
        lax.fori_loop(0, tm // grp, step, 0)

    @pl.when(i == 0)
    def _():
        row_loop(lambda r0: fetch_rows(dcur_ref, 0, r0))

    pltpu.make_async_copy(ys_hbm.at[pl.ds(0, TOP_K * tm * ROW_SLAB)], ybuf.at[slot], sem.at[slot]).wait()

    @pl.when(i + 1 < nsteps)
    def _():
        def both(r0):
            fetch_rows(dnext_ref, 1 - slot, r0)
            combine_rows(r0)
        row_loop(both)

    @pl.when(i + 1 == nsteps)
    def _():
        row_loop(combine_rows)

    h2 = h2_scr[...]
    n = _rms(h2, gple_ref[...]).astype(BF16)
    gate = _sigmoid(jnp.dot(n, wgate_ref[...], preferred_element_type=F32))
    pp = jnp.dot(p_ref[...].astype(BF16), wproj_ref[...], preferred_element_type=F32)
    o_ref[...] = _rms(h2 + gate * pp, gfin_ref[...])


def _final(h1, ys, dest_tok, wtok, p, wts, tm):
    n, d = h1.shape
    nsteps = n // tm
    assert ys.shape[0] >= TOP_K * tm * ROW_SLAB
    tok = lambda w: pl.BlockSpec((tm, w), lambda i: (i, 0))
    dspec = lambda f: pl.BlockSpec((1, 1, tm * TOP_K), f, memory_space=pltpu.SMEM)
    return pl.pallas_call(
        _final_kernel,
        grid=(nsteps,),
        in_specs=[dspec(lambda i: (i, 0, 0)), dspec(lambda i: (jnp.minimum(i + 1, nsteps - 1), 0, 0)),
                  tok(d), tok(LANE), tok(p.shape[1]), _full((1, d)), _full((d, d)),
                  _full((p.shape[1], d)), _full((1, d)), pl.BlockSpec(memory_space=pl.ANY)],
        out_specs=tok(d),
        out_shape=jax.ShapeDtypeStruct((n, d), F32),
        scratch_shapes=[pltpu.VMEM((2, TOP_K * tm * ROW_SLAB, ys.shape[1]), ys.dtype), pltpu.VMEM((tm, d), F32),
                        pltpu.SemaphoreType.DMA((2,))],
        compiler_params=_cparams(("arbitrary",)),
        name="final",
    )(dest_tok, dest_tok, h1, wtok, p, wts['g_ple'], wts['w_ple_gate'], wts['w_ple_proj'], wts['g_final'], ys)


def _rotate_half_cols(w):
    half = QK_ROPE // 2
    wh = w.reshape(w.shape[0], -1, 2, half)
    return jnp.concatenate([-wh[:, :, 1:2], wh[:, :, 0:1]], axis=2).reshape(w.shape)


def _prep_weights(g_mix, w_in, gdn_conv_w, gdn_a_log, gdn_dt_bias, gdn_norm, mla_g_q, mla_w_uq, mla_g_kv,
                  mla_w_uk, mla_w_uv, w_out, g_ffn, w_router, router_bias, w_exp_gate, w_exp_up, w_exp_down,
                  w_sh_gate, w_sh_up, w_sh_down, g_ple, w_ple_gate, w_ple_proj, g_final):
    d = w_in.shape[0]
    splits = np.cumsum([CONV_CH, GDN_V_W, GDN_HEADS, GDN_HEADS, Q_LORA, KV_LORA])
    w_qkv, w_z, w_b, w_a, w_cq, w_ckv, w_kr = jnp.split(w_in, [int(s) for s in splits], axis=1)
    ba_pad = jnp.zeros((d, LANE - 2 * GDN_HEADS), w_in.dtype)
    w1 = jnp.concatenate([w_qkv, w_z, w_cq, w_ckv, w_kr, _rotate_half_cols(w_kr), w_b, w_a, ba_pad], axis=1)
    assert w1.shape[1] == W1_WIDTH
    lane_pad = lambda v: jnp.pad(v.astype(F32), (GDN_HEADS, LANE - 2 * GDN_HEADS)).reshape(1, LANE)
    uq = mla_w_uq.reshape(Q_LORA, MLA_HEADS, QK_NOPE + QK_ROPE)
    uq_nope = uq[:, :, :QK_NOPE].reshape(Q_LORA, MLA_HEADS * QK_NOPE)
    uq_rope = uq[:, :, QK_NOPE:].reshape(Q_LORA, MLA_HEADS * QK_ROPE)
    wuq = jnp.concatenate([uq_nope, uq_rope, _rotate_half_cols(uq_rope)], axis=1)
    wr_t = w_router.T.astype(F32)
    wr_hi = wr_t.astype(BF16)
    row = lambda v: v.astype(F32).reshape(1, -1)
    return dict(
        g_mix=row(g_mix), w1=w1.astype(BF16), alog=lane_pad(gdn_a_log), dtb=lane_pad(gdn_dt_bias),
        g_q=row(mla_g_q), wuq=wuq.astype(BF16), wuk=jnp.transpose(mla_w_uk, (1, 2, 0)).astype(BF16),
        g_kv=row(mla_g_kv),
        conv_w=jnp.pad(gdn_conv_w.astype(F32), ((0, SUBLANE - CONV_W), (0, 0))), gdn_norm=row(gdn_norm),
        wuv=jnp.transpose(mla_w_uv, (1, 0, 2)).astype(BF16), w_out=w_out.astype(BF16), g_ffn=row(g_ffn),
        wr_hi=wr_hi, wr_lo=(wr_t - wr_hi.astype(F32)).astype(BF16),
        rbias=router_bias.astype(F32).reshape(N_EXPERTS, 1),
        ws_gu=jnp.concatenate([w_sh_gate, w_sh_up], axis=1).astype(BF16), ws_d=w_sh_down.astype(BF16),
        we_gu=jnp.concatenate([w_exp_gate, w_exp_up], axis=2).astype(BF16), we_d=w_exp_down.astype(BF16),
        g_ple=row(g_ple), w_ple_gate=w_ple_gate.astype(BF16), w_ple_proj=w_ple_proj.astype(BF16),
        g_final=row(g_final),
    )


def _rope_tables(pos, reps):
    half = QK_ROPE // 2
    inv = ROPE_THETA ** (-jnp.arange(half, dtype=F32) / half)
    ang = pos.astype(F32)[:, None] * inv[None, :]
    cos = jnp.concatenate([jnp.cos(ang)] * 2, axis=1)
    sin = jnp.concatenate([jnp.sin(ang)] * 2, axis=1)
    cos = jnp.tile(jnp.tile(cos, (1, MLA_HEADS)), (reps, 1))
    sin = jnp.tile(jnp.tile(sin, (1, MLA_HEADS)), (reps, 1))
    return cos, sin


def _tile(n, pref):
    t = min(n, pref)
    assert n % t == 0
    return t


def _layer(x, p, pos, gdn_state, conv_buf, attend, wts, chunk):
    b, t, d = x.shape
    n = b * t
    tm = _tile(n, 512)
    if t >= tm:
        assert t % tm == 0
        cos, sin = _rope_tables(pos, 1)
    else:
        assert tm % t == 0
        cos, sin = _rope_tables(pos, tm // t)
    x2 = x.reshape(n, d)
    qkv, z, gb, ckv, krope, kv, q = _in_proj(x2, wts, cos, sin, tm)

    conv0 = jnp.pad(conv_buf.astype(F32), ((0, 0), (SUBLANE - (CONV_W - 1), 0), (0, 0)))
    o_gdn, gdn_new = _gdn(qkv.reshape(b, t, CONV_CH), z.reshape(b, t, GDN_V_W), gb.reshape(b, t, LANE),
                          gdn_state.astype(F32), conv0, wts['conv_w'], wts['gdn_norm'], chunk)
    conv_new = qkv.reshape(b, t, CONV_CH)[:, t - (CONV_W - 1):, :]

    o_lat = attend(q.reshape(b, t, MLA_HEADS * QK_PAD), kv.reshape(b, t, QK_PAD))

    h1, xp, idx_t, rank_t, wtok, counts = _post_mix(x2, o_gdn.reshape(n, GDN_V_W),
                                                    o_lat.reshape(n, MLA_HEADS * KV_LORA), wts, tm)
    rows = n * TOP_K
    bm = _tile(rows, 1024)
    cnt = counts[:, 0].astype(jnp.int32)
    vt, ve, seg, n_visits, starts = _visit_list(cnt, rows, bm)
    dest_t = _dest_rows(idx_t, rank_t, starts, _tile(n, 2048))
    dest_tok = dest_t.T.reshape(n // tm, 1, tm * TOP_K)
    xs = _scatter_rows(xp, dest_tok, tm)
    ys = _experts(xs, vt, ve, seg, n_visits, wts['we_gu'], wts['we_d'], bm)
    y = _final(h1, ys, dest_tok, wtok, p.reshape(n, -1), wts, tm)
    return y.reshape(b, t, d), ckv.reshape(b, t, KV_LORA), krope.reshape(b, t, QK_ROPE), gdn_new, conv_new


def kernel(x_prompt, x_sample, p_prompt, p_sample, cache_ckv, cache_krope, state_gdn, state_conv, page_table,
           g_mix, w_in, gdn_conv_w, gdn_a_log, gdn_dt_bias, gdn_norm, mla_g_q, mla_w_uq, mla_g_kv, mla_w_uk,
           mla_w_uv, w_out, g_ffn, w_router, router_bias, w_exp_gate, w_exp_up, w_exp_down, w_sh_gate,
           w_sh_up, w_sh_down, g_ple, w_ple_gate, w_ple_proj, g_final):
    depth = w_in.shape[0]
    assert depth == 1, "the final norm is fused into the (single) layer"
    bp, tp, _ = x_prompt.shape
    bs, ts, _ = x_sample.shape
    past = page_table.shape[1] * cache_ckv.shape[2]
    wts = _prep_weights(g_mix[0], w_in[0], gdn_conv_w[0], gdn_a_log[0], gdn_dt_bias[0], gdn_norm[0], mla_g_q[0],
                        mla_w_uq[0], mla_g_kv[0], mla_w_uk[0], mla_w_uv[0], w_out[0], g_ffn[0], w_router[0],
                        router_bias[0], w_exp_gate[0], w_exp_up[0], w_exp_down[0], w_sh_gate[0], w_sh_up[0],
                        w_sh_down[0], g_ple[0], w_ple_gate[0], w_ple_proj[0], g_final)

    cache_krope_t = jnp.swapaxes(cache_krope[0], 1, 2)

    def attend_s(q, kv):
        return _paged_attn(q, kv, cache_ckv[0], cache_krope_t, page_table,
                           pages_per_step=_tile(page_table.shape[1], 16))

    ys, c2, k2, s2, v2 = _layer(x_sample, p_sample[0], past + jnp.arange(ts), state_gdn[0], state_conv[0],
                                attend_s, wts, chunk=ts)

    gdn0 = jnp.zeros((bp, GDN_HEADS, GDN_DK, GDN_DV), state_gdn.dtype)
    conv0 = jnp.zeros((bp, CONV_W - 1, CONV_CH), state_conv.dtype)
    attend_p = functools.partial(_prompt_attn, tq=_tile(tp, 512), tk=_tile(tp, 512))
    yp, c1, k1, s1, v1 = _layer(x_prompt, p_prompt[0], jnp.arange(tp), gdn0, conv0, attend_p, wts,
                                chunk=_tile(tp, 64))
    st = lambda a, ref: a.astype(ref.dtype)[None]
    return (yp, ys, c1[None], k1[None], st(s1, state_gdn), st(v1, state_conv),
            c2[None], k2[None], st(s2, state_gdn), st(v2, state_conv))
```
